```python
import math
import jax, jax.numpy as jnp
from jax import lax
import numpy as np

D_MODEL = 1024
BATCH = 2
SEQ = 8192
DEPTH = 1

CHUNK = 64
Q_BLOCK = 128
ROPE_THETA = 10000.0
NORM_EPS = 1e-6
SUBLN_EPS = 1e-5

DA_WIDTH = D_MODEL
DA_QK_DIM = 64
DA_V_DIM = 2 * DA_QK_DIM
DA_HEADS = DA_WIDTH // DA_V_DIM

FOX_WIDTH = D_MODEL
FOX_HEAD_DIM = 128
FOX_HEADS = FOX_WIDTH // FOX_HEAD_DIM

IN_SIZES = [DA_WIDTH, DA_WIDTH, DA_WIDTH, DA_WIDTH,
            FOX_WIDTH, FOX_WIDTH, FOX_WIDTH, FOX_WIDTH,
            FOX_HEADS,
            D_MODEL, D_MODEL]
N_IN = sum(IN_SIZES)
IN_SPLITS = [int(v) for v in np.cumsum(IN_SIZES)[:-1]]

kernel_name = "hybrid_diff_fox_gated_block"


def rms_norm(x, g, eps=NORM_EPS):
    xf = x.astype(jnp.float32)
    y = xf * lax.rsqrt(jnp.mean(xf * xf, axis=-1, keepdims=True) + eps)
    return (y * g.astype(jnp.float32)).astype(x.dtype)


def to_heads(t, n):
    b, s, _ = t.shape
    return t.reshape(b, s, n, -1).transpose(0, 2, 1, 3)


def merge_heads(t):
    b, h, s, d = t.shape
    return t.transpose(0, 2, 1, 3).reshape(b, s, h * d)


def rope(t, positions):
    half = t.shape[-1] // 2
    inv_freq = ROPE_THETA ** (-jnp.arange(half, dtype=jnp.float32) / half)
    ang = positions.astype(jnp.float32)[:, None, :, None] * inv_freq
    cos, sin = jnp.cos(ang), jnp.sin(ang)
    tf = t.astype(jnp.float32)
    t1, t2 = tf[..., :half], tf[..., half:]
    return jnp.concatenate([t1 * cos - t2 * sin, t1 * sin + t2 * cos], axis=-1).astype(t.dtype)


def diff_attention(q1, q2, k1, k2, v, lam):
    s_len = q1.shape[2]
    scale = DA_QK_DIM ** -0.5
    outs = []
    for qb in range(s_len // Q_BLOCK):
        q0, kend = qb * Q_BLOCK, (qb + 1) * Q_BLOCK
        mask = (jnp.arange(kend) // CHUNK)[None, :] <= (jnp.arange(q0, kend) // CHUNK)[:, None]
        s1 = jnp.einsum('bhqd,bhkd->bhqk', q1[:, :, q0:kend], k1[:, :, :kend]).astype(jnp.float32) * scale
        s2 = jnp.einsum('bhqd,bhkd->bhqk', q2[:, :, q0:kend], k2[:, :, :kend]).astype(jnp.float32) * scale
        p1 = jax.nn.softmax(jnp.where(mask, s1, -jnp.inf), axis=-1)
        p2 = jax.nn.softmax(jnp.where(mask, s2, -jnp.inf), axis=-1)
        a = (p1 - lam * p2).astype(v.dtype)
        outs.append(jnp.einsum('bhqk,bhkd->bhqd', a, v[:, :, :kend]))
    return jnp.concatenate(outs, axis=2)


def forgetting_attention(q, k, v, cum_logf):
    s_len = q.shape[2]
    scale = FOX_HEAD_DIM ** -0.5
    outs = []
    for qb in range(s_len // Q_BLOCK):
        q0, kend = qb * Q_BLOCK, (qb + 1) * Q_BLOCK
        mask = jnp.arange(kend)[None, :] <= jnp.arange(q0, kend)[:, None]
        s = (jnp.einsum('bhqd,bhkd->bhqk', q[:, :, q0:kend], k[:, :, :kend]).astype(jnp.float32) * scale
             + cum_logf[:, :, q0:kend, None] - cum_logf[:, :, None, :kend])
        p = jax.nn.softmax(jnp.where(mask, s, -jnp.inf), axis=-1).astype(v.dtype)
        outs.append(jnp.einsum('bhqk,bhkd->bhqd', p, v[:, :, :kend]))
    return jnp.concatenate(outs, axis=2)


def setup_inputs(seed: int = 0) -> dict:
    key = jax.random.key(seed)
    ks = jax.random.split(key, 20)
    f32 = jnp.float32
    d = D_MODEL
    x = jax.random.normal(ks[0], (BATCH, SEQ, d), f32)
    c = jax.random.normal(ks[1], (BATCH, d), f32)
    offset = jax.random.randint(ks[2], (BATCH, 1), 0, 4096, dtype=jnp.int32)
    positions = (offset + jnp.arange(SEQ, dtype=jnp.int32)[None, :]).astype(jnp.int32)
    w_ada = jax.random.normal(ks[3], (DEPTH, d, 3 * d), f32) * (0.5 * d ** -0.5)
    b_ada = jax.random.normal(ks[4], (DEPTH, 3 * d), f32) * 0.02
    g_norm = 1.0 + 0.05 * jax.random.normal(ks[5], (DEPTH, d), f32)
    w_in = jax.random.normal(ks[6], (DEPTH, d, N_IN), f32) * d ** -0.5
    b_forget = 2.0 + 0.5 * jax.random.normal(ks[7], (DEPTH, FOX_HEADS), f32)
    lambda_q1 = 0.1 * jax.random.normal(ks[8], (DEPTH, DA_QK_DIM), f32)
    lambda_k1 = 0.1 * jax.random.normal(ks[9], (DEPTH, DA_QK_DIM), f32)
    lambda_q2 = 0.1 * jax.random.normal(ks[10], (DEPTH, DA_QK_DIM), f32)
    lambda_k2 = 0.1 * jax.random.normal(ks[11], (DEPTH, DA_QK_DIM), f32)
    g_subln = 1.0 + 0.05 * jax.random.normal(ks[12], (DEPTH, DA_V_DIM), f32)
    w_branch_a = jax.random.normal(ks[13], (DEPTH, DA_WIDTH, d), f32) * DA_WIDTH ** -0.5
    w_branch_b = jax.random.normal(ks[14], (DEPTH, FOX_WIDTH, d), f32) * FOX_WIDTH ** -0.5
    w_out = jax.random.normal(ks[15], (DEPTH, d, d), f32) * d ** -0.5
    g_final = 1.0 + 0.05 * jax.random.normal(ks[16], (d,), f32)
    return {"x": x, "c": c, "positions": positions, "w_ada": w_ada, "b_ada": b_ada,
            "g_norm": g_norm, "w_in": w_in, "b_forget": b_forget,
            "lambda_q1": lambda_q1, "lambda_k1": lambda_k1, "lambda_q2": lambda_q2,
            "lambda_k2": lambda_k2, "g_subln": g_subln, "w_branch_a": w_branch_a,
            "w_branch_b": w_branch_b, "w_out": w_out, "g_final": g_final}


def reference(x, c, positions, w_ada, b_ada, g_norm, w_in, b_forget, lambda_q1, lambda_k1,
              lambda_q2, lambda_k2, g_subln, w_branch_a, w_branch_b, w_out, g_final):
    c_act = jax.nn.silu(c)
    for l in range(DEPTH):
        lambda_init = 0.8 - 0.6 * math.exp(-0.3 * l)
        mod = jnp.einsum('bd,de->be', c_act, w_ada[l]) + b_ada[l]
        shift, scale, gate = jnp.split(mod, 3, axis=-1)
        h = rms_norm(x, g_norm[l]) * (1.0 + scale[:, None, :]) + shift[:, None, :]

        proj = jnp.einsum('bsd,de->bse', h, w_in[l])
        (q_a, k_a, v_a, z_a, q_b, k_b, v_b, z_b, f_logit, m_a, m_b) = jnp.split(proj, IN_SPLITS, axis=-1)

        qa = to_heads(q_a, DA_HEADS)
        ka = to_heads(k_a, DA_HEADS)
        q1 = rope(qa[..., :DA_QK_DIM], positions)
        q2 = rope(qa[..., DA_QK_DIM:], positions)
        k1 = rope(ka[..., :DA_QK_DIM], positions)
        k2 = rope(ka[..., DA_QK_DIM:], positions)
        va = to_heads(v_a, DA_HEADS)
        lam = (jnp.exp(jnp.sum(lambda_q1[l].astype(jnp.float32) * lambda_k1[l].astype(jnp.float32)))
               - jnp.exp(jnp.sum(lambda_q2[l].astype(jnp.float32) * lambda_k2[l].astype(jnp.float32)))
               + lambda_init)
        oa = diff_attention(q1, q2, k1, k2, va, lam)
        oa = rms_norm(oa, g_subln[l], SUBLN_EPS) * (1.0 - lambda_init)
        y_a = jnp.einsum('bse,ed->bsd', merge_heads(oa) * jax.nn.silu(z_a), w_branch_a[l])

        log_f = jax.nn.log_sigmoid(f_logit.astype(jnp.float32) + b_forget[l].astype(jnp.float32))
        cum_logf = jnp.cumsum(log_f, axis=1).transpose(0, 2, 1)
        ob = forgetting_attention(to_heads(q_b, FOX_HEADS), to_heads(k_b, FOX_HEADS),
                                  to_heads(v_b, FOX_HEADS), cum_logf)
        y_b = jnp.einsum('bse,ed->bsd', merge_heads(ob) * jax.nn.silu(z_b), w_branch_b[l])

        merged = jax.nn.sigmoid(m_a) * y_a + jax.nn.sigmoid(m_b) * y_b
        out = jnp.einsum('bsd,de->bse', merged, w_out[l])
        x = x + gate[:, None, :] * out
    return rms_norm(x, g_final)
```

```python
import functools
import math

import jax
import jax.numpy as jnp
from jax import lax
from jax.experimental import pallas as pl
from jax.experimental.pallas import tpu as pltpu

F32 = jnp.float32
BF16 = jnp.bfloat16

D_MODEL = 1024
N_HEADS = 8
HEAD_DIM = 128
DA_QK_DIM = 64
CHUNK = 64
ROPE_THETA = 10000.0
NORM_EPS = 1e-6
SUBLN_EPS = 1e-5
LAMBDA_INIT = 0.8 - 0.6 * math.exp(-0.3 * 0)

LANES = 128
SUBLANES = 8
ATT_TILE = 512
VMEM_LIMIT = 48 * 1024 * 1024
NEG_BIG = -1e30


def _params(*sem):
    return pltpu.CompilerParams(dimension_semantics=sem, vmem_limit_bytes=VMEM_LIMIT)


def _mod_kernel(c_ref, w_ref, b_ref, lq1_ref, lk1_ref, lq2_ref, lk2_ref, mod_ref, lam_ref):
    c = c_ref[...]
    c_act = c * jax.nn.sigmoid(c)
    mod_ref[...] = jnp.dot(c_act, w_ref[...], precision=lax.Precision.HIGHEST,
                           preferred_element_type=F32) + b_ref[...]
    s1 = jnp.sum(lq1_ref[...] * lk1_ref[...], axis=-1, keepdims=True)
    s2 = jnp.sum(lq2_ref[...] * lk2_ref[...], axis=-1, keepdims=True)
    lam = jnp.exp(s1) - jnp.exp(s2) + LAMBDA_INIT
    lam_ref[...] = jnp.broadcast_to(lam, lam_ref.shape)


def _mod_call(c_pad, w_ada, b_ada, lq1, lk1, lq2, lk2):
    d = D_MODEL
    lam_spec = pl.BlockSpec((1, DA_QK_DIM), lambda j: (0, 0))
    return pl.pallas_call(
        _mod_kernel,
        grid=(3,),
        in_specs=[pl.BlockSpec((SUBLANES, d), lambda j: (0, 0)),
                  pl.BlockSpec((d, d), lambda j: (0, j)),
                  pl.BlockSpec((1, d), lambda j: (0, j)),
                  lam_spec, lam_spec, lam_spec, lam_spec],
        out_specs=[pl.BlockSpec((SUBLANES, d), lambda j: (0, j)),
                   pl.BlockSpec((SUBLANES, LANES), lambda j: (0, 0))],
        out_shape=[jax.ShapeDtypeStruct((SUBLANES, 3 * d), F32),
                   jax.ShapeDtypeStruct((SUBLANES, LANES), F32)],
        compiler_params=_params("arbitrary"),
    )(c_pad, w_ada, b_ada, lq1, lk1, lq2, lk2)


def _hnorm_kernel(x_ref, g_ref, scale_ref, shift_ref, h_ref):
    x = x_ref[0]
    ms = jnp.mean(x * x, axis=-1, keepdims=True)
    y = x * lax.rsqrt(ms + NORM_EPS) * g_ref[...]
    h_ref[0] = (y * (1.0 + scale_ref[0]) + shift_ref[0]).astype(BF16)


def _hnorm_call(x, g, scale, shift, bm):
    b, s, d = x.shape
    row = pl.BlockSpec((1, bm, d), lambda bi, i: (bi, i, 0))
    vec = pl.BlockSpec((1, 1, d), lambda bi, i: (bi, 0, 0))
    return pl.pallas_call(
        _hnorm_kernel,
        grid=(b, s // bm),
        in_specs=[row, pl.BlockSpec((1, d), lambda bi, i: (0, 0)), vec, vec],
        out_specs=row,
        out_shape=jax.ShapeDtypeStruct((b, s, d), BF16),
        compiler_params=_params("parallel", "parallel"),
    )(x, g, scale, shift)


def _rope_table_kernel(pos_ref, invf_ref, sgn_ref, cos_ref, sin_ref):
    ang = pos_ref[0] * invf_ref[...]
    cos_ref[0] = jnp.cos(ang)
    sin_ref[0] = jnp.sin(ang) * sgn_ref[...]


def _rope_table_call(pos_b, invf, sgn, bm):
    b, s, _ = pos_b.shape
    row = pl.BlockSpec((1, bm, LANES), lambda bi, i: (bi, i, 0))
    vec = pl.BlockSpec((1, LANES), lambda bi, i: (0, 0))
    shape = jax.ShapeDtypeStruct((b, s, LANES), F32)
    return pl.pallas_call(
        _rope_table_kernel,
        grid=(b, s // bm),
        in_specs=[row, vec, vec],
        out_specs=[row, row],
        out_shape=[shape, shape],
        compiler_params=_params("parallel", "parallel"),
    )(pos_b, invf, sgn)


def _proj_kernel(*refs, rope, scale, act, layout):
    if rope:
        h_ref, w_ref, cos_ref, sin_ref, o_ref = refs
    else:
        h_ref, w_ref, o_ref = refs
    acc = jnp.dot(h_ref[0], w_ref[...], preferred_element_type=F32)
    if act == "silu":
        acc = acc * jax.nn.sigmoid(acc)
    elif act == "sigmoid":
        acc = jax.nn.sigmoid(acc)
    if layout == "nat":
        o_ref[0] = acc.astype(BF16)
        return
    if rope:
        cos = cos_ref[0]
        sin = sin_ref[0]
        lane = lax.broadcasted_iota(jnp.int32, cos.shape, 1)
        first_half = (lane % DA_QK_DIM) < (DA_QK_DIM // 2)
    for hh in range(N_HEADS):
        t = acc[:, hh * HEAD_DIM:(hh + 1) * HEAD_DIM]
        if rope:
            partner = jnp.where(first_half,
                                pltpu.roll(t, HEAD_DIM - DA_QK_DIM // 2, 1),
                                pltpu.roll(t, DA_QK_DIM // 2, 1))
            t = t * cos + partner * sin
        if scale != 1.0:
            t = t * scale
        if layout == "heads":
            o_ref[0, hh] = t.astype(BF16)
        else:
            o_ref[0, hh, 0] = t.T.astype(BF16)


def _proj_call(h, w_all, col_block, tables, bm, *, rope=False, scale=1.0, act=None, layout="nat"):
    b, s, d = h.shape
    n = D_MODEL
    in_specs = [pl.BlockSpec((1, bm, d), lambda bi, i: (bi, i, 0)),
                pl.BlockSpec((d, n), lambda bi, i: (0, col_block))]
    args = [h, w_all]
    if rope:
        tab = pl.BlockSpec((1, bm, LANES), lambda bi, i: (bi, i, 0))
        in_specs += [tab, tab]
        args += list(tables)
    if layout == "nat":
        out_spec = pl.BlockSpec((1, bm, n), lambda bi, i: (bi, i, 0))
        out_shape = jax.ShapeDtypeStruct((b, s, n), BF16)
    elif layout == "heads":
        out_spec = pl.BlockSpec((1, N_HEADS, bm, HEAD_DIM), lambda bi, i: (bi, 0, i, 0))
        out_shape = jax.ShapeDtypeStruct((b, N_HEADS, s, HEAD_DIM), BF16)
    else:
        out_spec = pl.BlockSpec((1, N_HEADS, 1, HEAD_DIM, bm), lambda bi, i: (bi, 0, i, 0, 0))
        out_shape = jax.ShapeDtypeStruct((b, N_HEADS, s // bm, HEAD_DIM, bm), BF16)
    return pl.pallas_call(
        functools.partial(_proj_kernel, rope=rope, scale=scale, act=act, layout=layout),
        grid=(b, s // bm),
        in_specs=in_specs,
        out_specs=out_spec,
        out_shape=out_shape,
        compiler_params=_params("parallel", "parallel"),
    )(*args)


def _split3(v):
    hi = v.astype(BF16)
    r1 = v - hi.astype(F32)
    mid = r1.astype(BF16)
    lo = (r1 - mid.astype(F32)).astype(BF16)
    return hi, mid, lo


def _cum_kernel(h_ref, wf_ref, bf_ref, o_ref, carry_ref):
    @pl.when(pl.program_id(1) == 0)
    def _():
        carry_ref[...] = jnp.zeros_like(carry_ref)

    bm = h_ref.shape[1]
    logit = jnp.dot(h_ref[0], wf_ref[...], preferred_element_type=F32) + bf_ref[...]
    log_f = jnp.minimum(logit, 0.0) - jnp.log(1.0 + jnp.exp(-jnp.abs(logit)))
    r = lax.broadcasted_iota(jnp.int32, (bm, bm), 0)
    c = lax.broadcasted_iota(jnp.int32, (bm, bm), 1)
    tri = jnp.where(c <= r, 1.0, 0.0).astype(BF16)
    hi, mid, lo = _split3(log_f)
    cum = (jnp.dot(tri, hi, preferred_element_type=F32)
           + jnp.dot(tri, mid, preferred_element_type=F32)
           + jnp.dot(tri, lo, preferred_element_type=F32)) + carry_ref[0:1, :]
    carry_ref[...] = jnp.broadcast_to(cum[bm - 1:bm, :], carry_ref.shape)
    for hh in range(N_HEADS):
        o_ref[0, hh] = -jnp.broadcast_to(cum[:, hh:hh + 1], (bm, LANES))


def _cum_call(h, wf_pad, bf_pad, bm):
    b, s, d = h.shape
    return pl.pallas_call(
        _cum_kernel,
        grid=(b, s // bm),
        in_specs=[pl.BlockSpec((1, bm, d), lambda bi, i: (bi, i, 0)),
                  pl.BlockSpec((d, LANES), lambda bi, i: (0, 0)),
                  pl.BlockSpec((1, LANES), lambda bi, i: (0, 0))],
        out_specs=pl.BlockSpec((1, N_HEADS, bm, LANES), lambda bi, i: (bi, 0, i, 0)),
        out_shape=jax.ShapeDtypeStruct((b, N_HEADS, s, LANES), F32),
        scratch_shapes=[pltpu.VMEM((SUBLANES, LANES), F32)],
        compiler_params=_params("parallel", "arbitrary"),
    )(h, wf_pad, bf_pad)


def _online_update(s, vt, m_ref, l_ref, acc_ref):
    m_prev = m_ref[...]
    m_new = jnp.maximum(m_prev, jnp.max(s, axis=0, keepdims=True))
    alpha = jnp.exp(m_prev - m_new)
    p = jnp.exp(s - m_new)
    l_ref[...] = alpha * l_ref[...] + jnp.sum(p, axis=0, keepdims=True)
    acc_ref[...] = alpha * acc_ref[...] + jnp.dot(vt, p.astype(BF16), preferred_element_type=F32)
    m_ref[...] = m_new


def _init_stats(m_ref, l_ref, acc_ref):
    m_ref[...] = jnp.full(m_ref.shape, NEG_BIG, F32)
    l_ref[...] = jnp.zeros(l_ref.shape, F32)
    acc_ref[...] = jnp.zeros(acc_ref.shape, F32)


def _diff_kernel(lam_ref, qt_ref, k_ref, vt_ref, g_ref, o_ref, m1, l1, acc1, m2, l2, acc2):
    i = pl.program_id(2)
    t = qt_ref.shape[-1]
    qt = qt_ref[0, 0, 0]
    row = lax.broadcasted_iota(jnp.int32, qt.shape, 0)
    zero = jnp.zeros_like(qt)
    q1 = jnp.where(row < DA_QK_DIM, qt, zero)
    q2 = jnp.where(row >= DA_QK_DIM, qt, zero)
    _init_stats(m1, l1, acc1)
    _init_stats(m2, l2, acc2)

    def tile(j, diagonal):
        k = k_ref[0, 0, j]
        vt = vt_ref[0, 0, j]
        s1 = jnp.dot(k, q1, preferred_element_type=F32)
        s2 = jnp.dot(k, q2, preferred_element_type=F32)
        if diagonal:
            kc = lax.broadcasted_iota(jnp.int32, s1.shape, 0) // CHUNK
            qc = lax.broadcasted_iota(jnp.int32, s1.shape, 1) // CHUNK
            visible = kc <= qc
            s1 = jnp.where(visible, s1, -jnp.inf)
            s2 = jnp.where(visible, s2, -jnp.inf)
        _online_update(s1, vt, m1, l1, acc1)
        _online_update(s2, vt, m2, l2, acc2)

    def body(j, carry):
        tile(j, False)
        return carry

    lax.fori_loop(0, i, body, 0)
    tile(i, True)

    lam = lam_ref[0]
    o = acc1[...] / l1[...] - lam * (acc2[...] / l2[...])
    ms = jnp.mean(o * o, axis=0, keepdims=True)
    g = jnp.concatenate([g_ref[...]] * (t // LANES), axis=1)
    y = o * lax.rsqrt(ms + SUBLN_EPS) * g
    o_ref[0, 0] = y.T.astype(BF16)


def _diff_call(lam, qt, k, vt, g_rep):
    b, nh, nq, hd, t = qt.shape
    stat = pltpu.VMEM((1, t), F32)
    acc = pltpu.VMEM((hd, t), F32)
    return pl.pallas_call(
        _diff_kernel,
        grid=(b, nh, nq),
        in_specs=[pl.BlockSpec(memory_space=pltpu.SMEM),
                  pl.BlockSpec((1, 1, 1, hd, t), lambda bi, h, i: (bi, h, i, 0, 0)),
                  pl.BlockSpec((1, 1, nq, t, hd), lambda bi, h, i: (bi, h, 0, 0, 0)),
                  pl.BlockSpec((1, 1, nq, hd, t), lambda bi, h, i: (bi, h, 0, 0, 0)),
                  pl.BlockSpec((hd, LANES), lambda bi, h, i: (0, 0))],
        out_specs=pl.BlockSpec((1, 1, t, hd), lambda bi, h, i: (bi, h, i, 0)),
        out_shape=jax.ShapeDtypeStruct((b, nh, nq * t, hd), BF16),
        scratch_shapes=[stat, stat, acc, stat, stat, acc],
        compiler_params=_params("parallel", "parallel", "arbitrary"),
    )(lam, qt, k, vt, g_rep)


def _fox_kernel(qt_ref, k_ref, vt_ref, bias_ref, o_ref, m, l, acc):
    i = pl.program_id(2)
    t = qt_ref.shape[-1]
    qt = qt_ref[0, 0, 0]
    _init_stats(m, l, acc)

    def tile(j, diagonal):
        k = k_ref[0, 0, j]
        vt = vt_ref[0, 0, j]
        bias = jnp.concatenate([bias_ref[0, 0, j]] * (t // LANES), axis=1)
        s = jnp.dot(k, qt, preferred_element_type=F32) + bias
        if diagonal:
            ki = lax.broadcasted_iota(jnp.int32, s.shape, 0)
            qi = lax.broadcasted_iota(jnp.int32, s.shape, 1)
            s = jnp.where(ki <= qi, s, -jnp.inf)
        _online_update(s, vt, m, l, acc)

    def body(j, carry):
        tile(j, False)
        return carry

    lax.fori_loop(0, i, body, 0)
    tile(i, True)
    o = acc[...] / l[...]
    o_ref[0, 0] = o.T.astype(BF16)


def _fox_call(qt, k, vt, bias):
    b, nh, nq, hd, t = qt.shape
    return pl.pallas_call(
        _fox_kernel,
        grid=(b, nh, nq),
        in_specs=[pl.BlockSpec((1, 1, 1, hd, t), lambda bi, h, i: (bi, h, i, 0, 0)),
                  pl.BlockSpec((1, 1, nq, t, hd), lambda bi, h, i: (bi, h, 0, 0, 0)),
                  pl.BlockSpec((1, 1, nq, hd, t), lambda bi, h, i: (bi, h, 0, 0, 0)),
                  pl.BlockSpec((1, 1, nq, t, LANES), lambda bi, h, i: (bi, h, 0, 0, 0))],
        out_specs=pl.BlockSpec((1, 1, t, hd), lambda bi, h, i: (bi, h, i, 0)),
        out_shape=jax.ShapeDtypeStruct((b, nh, nq * t, hd), BF16),
        scratch_shapes=[pltpu.VMEM((1, t), F32), pltpu.VMEM((1, t), F32), pltpu.VMEM((hd, t), F32)],
        compiler_params=_params("parallel", "parallel", "arbitrary"),
    )(qt, k, vt, bias)


def _out_kernel(oa_ref, za_ref, ob_ref, zb_ref, ma_ref, mb_ref, x_ref, gate_ref,
                wa_ref, wb_ref, wo_ref, gf_ref, o_ref):
    def gated(o_ref_, z_ref_):
        o = jnp.concatenate([o_ref_[0, hh] for hh in range(N_HEADS)], axis=1)
        return (o.astype(F32) * z_ref_[0].astype(F32)).astype(BF16)

    ya = jnp.dot(gated(oa_ref, za_ref), wa_ref[...], preferred_element_type=F32)
    yb = jnp.dot(gated(ob_ref, zb_ref), wb_ref[...], preferred_element_type=F32)
    merged = ma_ref[0].astype(F32) * ya + mb_ref[0].astype(F32) * yb
    out = jnp.dot(merged.astype(BF16), wo_ref[...], preferred_element_type=F32)
    xn = x_ref[0] + gate_ref[0] * out
    ms = jnp.mean(xn * xn, axis=-1, keepdims=True)
    o_ref[0] = xn * lax.rsqrt(ms + NORM_EPS) * gf_ref[...]


def _out_call(oa, za, ob, zb, ma, mb, x, gate, wa, wb, wo, gf, bm):
    b, s, d = x.shape
    heads = pl.BlockSpec((1, N_HEADS, bm, HEAD_DIM), lambda bi, i: (bi, 0, i, 0))
    row = pl.BlockSpec((1, bm, d), lambda bi, i: (bi, i, 0))
    wspec = pl.BlockSpec((d, d), lambda bi, i: (0, 0))
    return pl.pallas_call(
        _out_kernel,
        grid=(b, s // bm),
        in_specs=[heads, row, heads, row, row, row, row,
                  pl.BlockSpec((1, 1, d), lambda bi, i: (bi, 0, 0)),
                  wspec, wspec, wspec,
                  pl.BlockSpec((1, d), lambda bi, i: (0, 0))],
        out_specs=row,
        out_shape=jax.ShapeDtypeStruct((b, s, d), F32),
        compiler_params=_params("parallel", "parallel"),
    )(oa, za, ob, zb, ma, mb, x, gate, wa, wb, wo, gf)


def kernel(x, c, positions, w_ada, b_ada, g_norm, w_in, b_forget, lambda_q1, lambda_k1, lambda_q2,
           lambda_k2, g_subln, w_branch_a, w_branch_b, w_out, g_final):
    b, s, d = x.shape
    assert d == D_MODEL and s % ATT_TILE == 0 and b <= SUBLANES
    assert w_ada.shape[0] == 1, "single-layer block"
    t = ATT_TILE
    n_main = 8 * D_MODEL

    c_pad = jnp.zeros((SUBLANES, d), F32).at[:b].set(c)
    w_cat = jnp.concatenate([w_in[0][:, :n_main], w_in[0][:, n_main + N_HEADS:]], axis=1).astype(BF16)
    wf_pad = jnp.zeros((d, LANES), F32).at[:, :N_HEADS].set(w_in[0][:, n_main:n_main + N_HEADS]).astype(BF16)
    bf_pad = jnp.zeros((1, LANES), F32).at[0, :N_HEADS].set(b_forget[0])
    half = DA_QK_DIM // 2
    inv_freq = ROPE_THETA ** (-jnp.arange(half, dtype=F32) / half)
    invf = jnp.tile(inv_freq, LANES // half)[None, :]
    sgn = jnp.tile(jnp.concatenate([-jnp.ones((half,), F32), jnp.ones((half,), F32)]),
                   LANES // DA_QK_DIM)[None, :]
    pos_b = jnp.broadcast_to(positions.astype(F32)[:, :, None], (b, s, LANES))
    g_rep = jnp.broadcast_to((g_subln[0] * (1.0 - LAMBDA_INIT))[:, None], (HEAD_DIM, LANES))

    mod, lam_tile = _mod_call(c_pad, w_ada[0], b_ada[0][None, :], lambda_q1[0][None, :],
                              lambda_k1[0][None, :], lambda_q2[0][None, :], lambda_k2[0][None, :])
    shift = mod[:b, None, 0:d]
    scale = mod[:b, None, d:2 * d]
    gate = mod[:b, None, 2 * d:3 * d]
    lam = lam_tile[0, :1]

    h = _hnorm_call(x, g_norm[0][None, :], scale, shift, t)
    tables = _rope_table_call(pos_b, invf, sgn, t)

    proj = functools.partial(_proj_call, h, w_cat)
    qa_t = proj(0, tables, t, rope=True, scale=DA_QK_DIM ** -0.5, layout="heads_t")
    ka = proj(1, tables, t, rope=True, layout="heads")
    va_t = proj(2, None, t, layout="heads_t")
    za = proj(3, None, t, act="silu")
    qb_t = proj(4, None, t, scale=HEAD_DIM ** -0.5, layout="heads_t")
    kb = proj(5, None, t, layout="heads")
    vb_t = proj(6, None, t, layout="heads_t")
    zb = proj(7, None, t, act="silu")
    ma = proj(8, None, t, act="sigmoid")
    mb = proj(9, None, t, act="sigmoid")

    bias = _cum_call(h, wf_pad, bf_pad, t)

    nq = s // t
    oa = _diff_call(lam, qa_t, ka.reshape(b, N_HEADS, nq, t, HEAD_DIM), va_t, g_rep)
    ob = _fox_call(qb_t, kb.reshape(b, N_HEADS, nq, t, HEAD_DIM), vb_t,
                   bias.reshape(b, N_HEADS, nq, t, LANES))

    return _out_call(oa, za, ob, zb, ma, mb, x, gate,
                     w_branch_a[0].astype(BF16), w_branch_b[0].astype(BF16), w_out[0].astype(BF16),
                     g_final[None, :], t)
```

```python
import functools
import math

import jax
import jax.numpy as jnp
from jax import lax
from jax.experimental import pallas as pl
from jax.experimental.pallas import tpu as pltpu

F32 = jnp.float32
BF16 = jnp.bfloat16

D_MODEL = 1024
N_HEADS = 8
HEAD_DIM = 128
DA_QK_DIM = 64
CHUNK = 64
ROPE_THETA = 10000.0
NORM_EPS = 1e-6
SUBLN_EPS = 1e-5
LAMBDA_INIT = 0.8 - 0.6 * math.exp(-0.3 * 0)
LOG2E = math.log2(math.e)

LANES = 128
SUBLANES = 8
ATT_TILE = 512
BIAS_TERMS = 3
VMEM_LIMIT = 48 * 1024 * 1024
NEG_BIG = -1e30


def _params(*sem):
    return pltpu.CompilerParams(dimension_semantics=sem, vmem_limit_bytes=VMEM_LIMIT)


def _mod_kernel(c_ref, w_ref, b_ref, lq1_ref, lk1_ref, lq2_ref, lk2_ref, mod_ref, lam_ref):
    c = c_ref[...]
    c_act = c * jax.nn.sigmoid(c)
    mod_ref[...] = jnp.dot(c_act, w_ref[...], precision=lax.Precision.HIGHEST,
                           preferred_element_type=F32) + b_ref[...]
    s1 = jnp.sum(lq1_ref[...] * lk1_ref[...], axis=-1, keepdims=True)
    s2 = jnp.sum(lq2_ref[...] * lk2_ref[...], axis=-1, keepdims=True)
    lam = jnp.exp(s1) - jnp.exp(s2) + LAMBDA_INIT
    lam_ref[...] = jnp.broadcast_to(lam, lam_ref.shape)


def _mod_call(c_pad, w_ada, b_ada, lq1, lk1, lq2, lk2):
    d = D_MODEL
    lam_spec = pl.BlockSpec((1, DA_QK_DIM), lambda j: (0, 0))
    return pl.pallas_call(
        _mod_kernel,
        grid=(3,),
        in_specs=[pl.BlockSpec((SUBLANES, d), lambda j: (0, 0)),
                  pl.BlockSpec((d, d), lambda j: (0, j)),
                  pl.BlockSpec((1, d), lambda j: (0, j)),
                  lam_spec, lam_spec, lam_spec, lam_spec],
        out_specs=[pl.BlockSpec((SUBLANES, d), lambda j: (0, j)),
                   pl.BlockSpec((SUBLANES, LANES), lambda j: (0, 0))],
        out_shape=[jax.ShapeDtypeStruct((SUBLANES, 3 * d), F32),
                   jax.ShapeDtypeStruct((SUBLANES, LANES), F32)],
        compiler_params=_params("arbitrary"),
    )(c_pad, w_ada, b_ada, lq1, lk1, lq2, lk2)


def _hnorm_kernel(x_ref, g_ref, scale_ref, shift_ref, h_ref):
    x = x_ref[0]
    ms = jnp.mean(x * x, axis=-1, keepdims=True)
    y = x * lax.rsqrt(ms + NORM_EPS) * g_ref[...]
    h_ref[0] = (y * (1.0 + scale_ref[0]) + shift_ref[0]).astype(BF16)


def _hnorm_call(x, g, scale, shift, bm):
    b, s, d = x.shape
    row = pl.BlockSpec((1, bm, d), lambda bi, i: (bi, i, 0))
    vec = pl.BlockSpec((1, 1, d), lambda bi, i: (bi, 0, 0))
    return pl.pallas_call(
        _hnorm_kernel,
        grid=(b, s // bm),
        in_specs=[row, pl.BlockSpec((1, d), lambda bi, i: (0, 0)), vec, vec],
        out_specs=row,
        out_shape=jax.ShapeDtypeStruct((b, s, d), BF16),
        compiler_params=_params("parallel", "parallel"),
    )(x, g, scale, shift)


def _rope_table_kernel(pos_ref, invf_ref, sgn_ref, cos_ref, sin_ref):
    ang = pos_ref[0] * invf_ref[...]
    cos_ref[0] = jnp.cos(ang)
    sin_ref[0] = jnp.sin(ang) * sgn_ref[...]


def _rope_table_call(pos_b, invf, sgn, bm):
    b, s, _ = pos_b.shape
    row = pl.BlockSpec((1, bm, LANES), lambda bi, i: (bi, i, 0))
    vec = pl.BlockSpec((1, LANES), lambda bi, i: (0, 0))
    shape = jax.ShapeDtypeStruct((b, s, LANES), F32)
    return pl.pallas_call(
        _rope_table_kernel,
        grid=(b, s // bm),
        in_specs=[row, vec, vec],
        out_specs=[row, row],
        out_shape=[shape, shape],
        compiler_params=_params("parallel", "parallel"),
    )(pos_b, invf, sgn)


def _proj_kernel(*refs, rope, scale, act, layout):
    if rope:
        h_ref, w_ref, cos_ref, sin_ref, o_ref = refs
    else:
        h_ref, w_ref, o_ref = refs
    acc = jnp.dot(h_ref[0], w_ref[...], preferred_element_type=F32)
    if act == "silu":
        acc = acc * jax.nn.sigmoid(acc)
    elif act == "sigmoid":
        acc = jax.nn.sigmoid(acc)
    if layout == "nat":
        o_ref[0] = acc.astype(BF16)
        return
    if rope:
        cos = cos_ref[0]
        sin = sin_ref[0]
        lane = lax.broadcasted_iota(jnp.int32, cos.shape, 1)
        first_half = (lane % DA_QK_DIM) < (DA_QK_DIM // 2)
    for hh in range(N_HEADS):
        t = acc[:, hh * HEAD_DIM:(hh + 1) * HEAD_DIM]
        if rope:
            partner = jnp.where(first_half,
                                pltpu.roll(t, HEAD_DIM - DA_QK_DIM // 2, 1),
                                pltpu.roll(t, DA_QK_DIM // 2, 1))
            t = t * cos + partner * sin
        if scale != 1.0:
            t = t * scale
        if layout == "heads":
            o_ref[0, hh] = t.astype(BF16)
        else:
            o_ref[0, hh, 0] = t.T.astype(BF16)


def _proj_call(h, w_all, col_block, tables, bm, *, rope=False, scale=1.0, act=None, layout="nat"):
    b, s, d = h.shape
    n = D_MODEL
    in_specs = [pl.BlockSpec((1, bm, d), lambda bi, i: (bi, i, 0)),
                pl.BlockSpec((d, n), lambda bi, i: (0, col_block))]
    args = [h, w_all]
    if rope:
        tab = pl.BlockSpec((1, bm, LANES), lambda bi, i: (bi, i, 0))
        in_specs += [tab, tab]
        args += list(tables)
    if layout == "nat":
        out_spec = pl.BlockSpec((1, bm, n), lambda bi, i: (bi, i, 0))
        out_shape = jax.ShapeDtypeStruct((b, s, n), BF16)
    elif layout == "heads":
        out_spec = pl.BlockSpec((1, N_HEADS, bm, HEAD_DIM), lambda bi, i: (bi, 0, i, 0))
        out_shape = jax.ShapeDtypeStruct((b, N_HEADS, s, HEAD_DIM), BF16)
    else:
        out_spec = pl.BlockSpec((1, N_HEADS, 1, HEAD_DIM, bm), lambda bi, i: (bi, 0, i, 0, 0))
        out_shape = jax.ShapeDtypeStruct((b, N_HEADS, s // bm, HEAD_DIM, bm), BF16)
    return pl.pallas_call(
        functools.partial(_proj_kernel, rope=rope, scale=scale, act=act, layout=layout),
        grid=(b, s // bm),
        in_specs=in_specs,
        out_specs=out_spec,
        out_shape=out_shape,
        compiler_params=_params("parallel", "parallel"),
    )(*args)


def _split3(v):
    hi = v.astype(BF16).astype(F32)
    r1 = v - hi
    mid = r1.astype(BF16).astype(F32)
    return hi, mid, r1 - mid


def _cum_kernel(h_ref, wf_ref, bf_ref, kext_ref, tot_ref):
    bm = h_ref.shape[1]
    logit = jnp.dot(h_ref[0], wf_ref[...], preferred_element_type=F32) + bf_ref[...]
    log_f = jnp.minimum(logit, 0.0) - jnp.log(1.0 + jnp.exp(-jnp.abs(logit)))
    r = lax.broadcasted_iota(jnp.int32, (bm, bm), 0)
    c = lax.broadcasted_iota(jnp.int32, (bm, bm), 1)
    tri = jnp.where(c <= r, 1.0, 0.0).astype(BF16)
    cum = sum(jnp.dot(tri, part.astype(BF16), preferred_element_type=F32) for part in _split3(log_f))
    bias = cum * (-LOG2E)
    tot_ref[0, 0] = jnp.broadcast_to(-bias[bm - 1:bm, :], tot_ref.shape[2:])
    lane = lax.broadcasted_iota(jnp.int32, (bm, LANES), 1)
    for hh in range(N_HEADS):
        parts = _split3(jnp.broadcast_to(bias[:, hh:hh + 1], (bm, LANES)))
        ext = jnp.zeros((bm, LANES), F32)
        for n, part in enumerate(parts):
            ext = jnp.where(lane == n, part, ext)
        kext_ref[0, hh] = ext.astype(BF16)


def _cum_call(h, wf_pad, bf_pad, bm):
    b, s, d = h.shape
    return pl.pallas_call(
        _cum_kernel,
        grid=(b, s // bm),
        in_specs=[pl.BlockSpec((1, bm, d), lambda bi, i: (bi, i, 0)),
                  pl.BlockSpec((d, LANES), lambda bi, i: (0, 0)),
                  pl.BlockSpec((1, LANES), lambda bi, i: (0, 0))],
        out_specs=[pl.BlockSpec((1, N_HEADS, bm, LANES), lambda bi, i: (bi, 0, i, 0)),
                   pl.BlockSpec((1, 1, SUBLANES, LANES), lambda bi, i: (bi, i, 0, 0))],
        out_shape=[jax.ShapeDtypeStruct((b, N_HEADS, s, LANES), BF16),
                   jax.ShapeDtypeStruct((b, s // bm, SUBLANES, LANES), F32)],
        compiler_params=_params("parallel", "parallel"),
    )(h, wf_pad, bf_pad)


def _attend(i, n_maps, scores, vt_ref, visible, frame_shift, finish,
            m, l, acc, s_buf0, s_buf1, p_buf0, p_buf1, a_buf0, a_buf1):
    s_buf, p_buf, a_buf = (s_buf0, s_buf1), (p_buf0, p_buf1), (a_buf0, a_buf1)
    m[...] = jnp.full(m.shape, NEG_BIG, F32)
    l[...] = jnp.zeros(l.shape, F32)
    acc[...] = jnp.zeros(acc.shape, F32)
    p_buf1[...] = jnp.zeros(p_buf1.shape, BF16)
    a_buf1[...] = jnp.ones(a_buf1.shape, F32)

    def put_scores(slot, j):
        for n, s in enumerate(scores(j)):
            s_buf[slot][n] = s

    def softmax(s, n):
        m_prev = m[n]
        m_new = jnp.maximum(m_prev, jnp.max(s, axis=0, keepdims=True))
        alpha = jnp.exp2(m_prev - m_new)
        p = jnp.exp2(s - m_new)
        l[n] = alpha * l[n] + jnp.sum(p, axis=0, keepdims=True)
        return m_new, alpha, p.astype(BF16)

    def value_product(j, slot, n):
        return a_buf[slot][n] * acc[n] + jnp.dot(vt_ref[0, 0, j], p_buf[slot][n],
                                                 preferred_element_type=F32)

    def step(j, slot):
        put_scores(1 - slot, j + 1)
        for n in range(n_maps):
            acc[n] = value_product(jnp.maximum(j - 1, 0), 1 - slot, n)
            m_new, alpha, p = softmax(s_buf[slot][n], n)
            p_buf[slot][n] = p
            a_buf[slot][n] = alpha
            m[n] = m_new + frame_shift(j)

    def last(slot):
        results = []
        for n in range(n_maps):
            acc_prev = value_product(jnp.maximum(i - 1, 0), 1 - slot, n)
            s = jnp.where(visible, s_buf[slot][n], -jnp.inf)
            _, alpha, p = softmax(s, n)
            out = alpha * acc_prev + jnp.dot(vt_ref[0, 0, i], p, preferred_element_type=F32)
            results.append((out, l[n]))
        finish(results)

    put_scores(0, 0)

    def pair(jj, carry):
        step(2 * jj, 0)
        step(2 * jj + 1, 1)
        return carry

    lax.fori_loop(0, i // 2, pair, 0)

    @pl.when(i % 2 == 0)
    def _():
        last(0)

    @pl.when(i % 2 == 1)
    def _():
        step(i - 1, 0)
        last(1)


def _attend_scratch(n_maps, hd, t):
    return [pltpu.VMEM((n_maps, 1, t), F32),
            pltpu.VMEM((n_maps, 1, t), F32),
            pltpu.VMEM((n_maps, hd, t), F32),
            pltpu.VMEM((n_maps, t, t), F32),
            pltpu.VMEM((n_maps, t, t), F32),
            pltpu.VMEM((n_maps, t, t), BF16),
            pltpu.VMEM((n_maps, t, t), BF16),
            pltpu.VMEM((n_maps, 1, t), F32),
            pltpu.VMEM((n_maps, 1, t), F32)]


def _diff_kernel(lam_ref, qt_ref, k_ref, vt_ref, g_ref, o_ref, *scratch):
    i = pl.program_id(2)
    t = qt_ref.shape[-1]
    qt = qt_ref[0, 0, 0]
    row = lax.broadcasted_iota(jnp.int32, qt.shape, 0)
    zero = jnp.zeros_like(qt)
    q1 = jnp.where(row < DA_QK_DIM, qt, zero)
    q2 = jnp.where(row >= DA_QK_DIM, qt, zero)

    def scores(j):
        k = k_ref[0, 0, j]
        return (jnp.dot(k, q1, preferred_element_type=F32),
                jnp.dot(k, q2, preferred_element_type=F32))

    kc = lax.broadcasted_iota(jnp.int32, (t, t), 0) // CHUNK
    qc = lax.broadcasted_iota(jnp.int32, (t, t), 1) // CHUNK

    def finish(results):
        (o1, l1), (o2, l2) = results
        o = o1 / l1 - lam_ref[0] * (o2 / l2)
        ms = jnp.mean(o * o, axis=0, keepdims=True)
        g = jnp.concatenate([g_ref[...]] * (t // LANES), axis=1)
        y = o * lax.rsqrt(ms + SUBLN_EPS) * g
        o_ref[0, 0] = y.T.astype(BF16)

    _attend(i, 2, scores, vt_ref, kc <= qc, lambda j: 0.0, finish, *scratch)


def _diff_call(lam, qt, k, vt, g_rep):
    b, nh, nq, hd, t = qt.shape
    return pl.pallas_call(
        _diff_kernel,
        grid=(b, nh, nq),
        in_specs=[pl.BlockSpec(memory_space=pltpu.SMEM),
                  pl.BlockSpec((1, 1, 1, hd, t), lambda bi, h, i: (bi, h, i, 0, 0)),
                  pl.BlockSpec((1, 1, nq, t, hd), lambda bi, h, i: (bi, h, 0, 0, 0)),
                  pl.BlockSpec((1, 1, nq, hd, t), lambda bi, h, i: (bi, h, 0, 0, 0)),
                  pl.BlockSpec((hd, LANES), lambda bi, h, i: (0, 0))],
        out_specs=pl.BlockSpec((1, 1, t, hd), lambda bi, h, i: (bi, h, i, 0)),
        out_shape=jax.ShapeDtypeStruct((b, nh, nq * t, hd), BF16),
        scratch_shapes=_attend_scratch(2, hd, t),
        compiler_params=_params("parallel", "parallel", "arbitrary"),
    )(lam, qt, k, vt, g_rep)


def _fox_kernel(tot_ref, qt_ref, k_ref, kext_ref, vt_ref, o_ref, *scratch):
    bi, h, i = pl.program_id(0), pl.program_id(1), pl.program_id(2)
    nq = pl.num_programs(2)
    t = qt_ref.shape[-1]
    qt = qt_ref[0, 0, 0]
    row = lax.broadcasted_iota(jnp.int32, qt.shape, 0)
    q_cat = jnp.concatenate([qt, jnp.where(row < BIAS_TERMS, 1.0, 0.0).astype(BF16)], axis=0)
    tot_base = (bi * N_HEADS + h) * nq

    def scores(j):
        k_cat = jnp.concatenate([k_ref[0, 0, j], kext_ref[0, 0, j]], axis=1)
        return (jnp.dot(k_cat, q_cat, preferred_element_type=F32),)

    ki = lax.broadcasted_iota(jnp.int32, (t, t), 0)
    qi = lax.broadcasted_iota(jnp.int32, (t, t), 1)

    def finish(results):
        ((o, l),) = results
        o_ref[0, 0] = (o / l).T.astype(BF16)

    _attend(i, 1, scores, vt_ref, ki <= qi, lambda j: tot_ref[tot_base + j], finish, *scratch)


def _fox_call(tot, qt, k, kext, vt):
    b, nh, nq, hd, t = qt.shape
    return pl.pallas_call(
        _fox_kernel,
        grid=(b, nh, nq),
        in_specs=[pl.BlockSpec(memory_space=pltpu.SMEM),
                  pl.BlockSpec((1, 1, 1, hd, t), lambda bi, h, i: (bi, h, i, 0, 0)),
                  pl.BlockSpec((1, 1, nq, t, hd), lambda bi, h, i: (bi, h, 0, 0, 0)),
                  pl.BlockSpec((1, 1, nq, t, LANES), lambda bi, h, i: (bi, h, 0, 0, 0)),
                  pl.BlockSpec((1, 1, nq, hd, t), lambda bi, h, i: (bi, h, 0, 0, 0))],
        out_specs=pl.BlockSpec((1, 1, t, hd), lambda bi, h, i: (bi, h, i, 0)),
        out_shape=jax.ShapeDtypeStruct((b, nh, nq * t, hd), BF16),
        scratch_shapes=_attend_scratch(1, hd, t),
        compiler_params=_params("parallel", "parallel", "arbitrary"),
    )(tot, qt, k, kext, vt)


def _out_kernel(oa_ref, za_ref, ob_ref, zb_ref, ma_ref, mb_ref, x_ref, gate_ref,
                wa_ref, wb_ref, wo_ref, gf_ref, o_ref):
    def gated(o_ref_, z_ref_):
        o = jnp.concatenate([o_ref_[0, hh] for hh in range(N_HEADS)], axis=1)
        return (o.astype(F32) * z_ref_[0].astype(F32)).astype(BF16)

    ya = jnp.dot(gated(oa_ref, za_ref), wa_ref[...], preferred_element_type=F32)
    yb = jnp.dot(gated(ob_ref, zb_ref), wb_ref[...], preferred_element_type=F32)
    merged = ma_ref[0].astype(F32) * ya + mb_ref[0].astype(F32) * yb
    out = jnp.dot(merged.astype(BF16), wo_ref[...], preferred_element_type=F32)
    xn = x_ref[0] + gate_ref[0] * out
    ms = jnp.mean(xn * xn, axis=-1, keepdims=True)
    o_ref[0] = xn * lax.rsqrt(ms + NORM_EPS) * gf_ref[...]


def _out_call(oa, za, ob, zb, ma, mb, x, gate, wa, wb, wo, gf, bm):
    b, s, d = x.shape
    heads = pl.BlockSpec((1, N_HEADS, bm, HEAD_DIM), lambda bi, i: (bi, 0, i, 0))
    row = pl.BlockSpec((1, bm, d), lambda bi, i: (bi, i, 0))
    wspec = pl.BlockSpec((d, d), lambda bi, i: (0, 0))
    return pl.pallas_call(
        _out_kernel,
        grid=(b, s // bm),
        in_specs=[heads, row, heads, row, row, row, row,
                  pl.BlockSpec((1, 1, d), lambda bi, i: (bi, 0, 0)),
                  wspec, wspec, wspec,
                  pl.BlockSpec((1, d), lambda bi, i: (0, 0))],
        out_specs=row,
        out_shape=jax.ShapeDtypeStruct((b, s, d), F32),
        compiler_params=_params("parallel", "parallel"),
    )(oa, za, ob, zb, ma, mb, x, gate, wa, wb, wo, gf)


def kernel(x, c, positions, w_ada, b_ada, g_norm, w_in, b_forget, lambda_q1, lambda_k1, lambda_q2,
           lambda_k2, g_subln, w_branch_a, w_branch_b, w_out, g_final):
    b, s, d = x.shape
    assert d == D_MODEL and s % ATT_TILE == 0 and b <= SUBLANES
    assert w_ada.shape[0] == 1, "single-layer block"
    t = ATT_TILE
    nq = s // t
    n_main = 8 * D_MODEL

    c_pad = jnp.zeros((SUBLANES, d), F32).at[:b].set(c)
    w_cat = jnp.concatenate([w_in[0][:, :n_main], w_in[0][:, n_main + N_HEADS:]], axis=1).astype(BF16)
    wf_pad = jnp.zeros((d, LANES), F32).at[:, :N_HEADS].set(w_in[0][:, n_main:n_main + N_HEADS]).astype(BF16)
    bf_pad = jnp.zeros((1, LANES), F32).at[0, :N_HEADS].set(b_forget[0])
    half = DA_QK_DIM // 2
    inv_freq = ROPE_THETA ** (-jnp.arange(half, dtype=F32) / half)
    invf = jnp.tile(inv_freq, LANES // half)[None, :]
    sgn = jnp.tile(jnp.concatenate([-jnp.ones((half,), F32), jnp.ones((half,), F32)]),
                   LANES // DA_QK_DIM)[None, :]
    pos_b = jnp.broadcast_to(positions.astype(F32)[:, :, None], (b, s, LANES))
    g_rep = jnp.broadcast_to((g_subln[0] * (1.0 - LAMBDA_INIT))[:, None], (HEAD_DIM, LANES))

    mod, lam_tile = _mod_call(c_pad, w_ada[0], b_ada[0][None, :], lambda_q1[0][None, :],
                              lambda_k1[0][None, :], lambda_q2[0][None, :], lambda_k2[0][None, :])
    shift = mod[:b, None, 0:d]
    scale = mod[:b, None, d:2 * d]
    gate = mod[:b, None, 2 * d:3 * d]
    lam = lam_tile[0, :1]

    h = _hnorm_call(x, g_norm[0][None, :], scale, shift, t)
    tables = _rope_table_call(pos_b, invf, sgn, t)

    proj = functools.partial(_proj_call, h, w_cat)
    qa_t = proj(0, tables, t, rope=True, scale=LOG2E * DA_QK_DIM ** -0.5, layout="heads_t")
    ka = proj(1, tables, t, rope=True, layout="heads")
    va_t = proj(2, None, t, layout="heads_t")
    za = proj(3, None, t, act="silu")
    qb_t = proj(4, None, t, scale=LOG2E * HEAD_DIM ** -0.5, layout="heads_t")
    kb = proj(5, None, t, layout="heads")
    vb_t = proj(6, None, t, layout="heads_t")
    zb = proj(7, None, t, act="silu")
    ma = proj(8, None, t, act="sigmoid")
    mb = proj(9, None, t, act="sigmoid")

    kext, tot_tile = _cum_call(h, wf_pad, bf_pad, t)
    tot = tot_tile[:, :, 0, :N_HEADS].transpose(0, 2, 1).reshape(-1)

    def tiles(a):
        return a.reshape(b, N_HEADS, nq, t, a.shape[-1])

    oa = _diff_call(lam, qa_t, tiles(ka), va_t, g_rep)
    ob = _fox_call(tot, qb_t, tiles(kb), tiles(kext), vb_t)

    return _out_call(oa, za, ob, zb, ma, mb, x, gate,
                     w_branch_a[0].astype(BF16), w_branch_b[0].astype(BF16), w_out[0].astype(BF16),
                     g_final[None, :], t)
```

```python
import functools
import math

import jax
import jax.numpy as jnp
from jax import lax
from jax.experimental import pallas as pl
from jax.experimental.pallas import tpu as pltpu

F32 = jnp.float32
BF16 = jnp.bfloat16

D_MODEL = 1024
N_HEADS = 8
HEAD_DIM = 128
DA_QK_DIM = 64
CHUNK = 64
ROPE_THETA = 10000.0
NORM_EPS = 1e-6
SUBLN_EPS = 1e-5
LAMBDA_INIT = 0.8 - 0.6 * math.exp(-0.3 * 0)
LOG2E = math.log2(math.e)

LANES = 128
SUBLANES = 8
ATT_TILE = 512
BIAS_TERMS = 3
VMEM_LIMIT = 48 * 1024 * 1024
NEG_BIG = -1e30


def _params(*sem):
    return pltpu.CompilerParams(dimension_semantics=sem, vmem_limit_bytes=VMEM_LIMIT)


def _mod_kernel(c_ref, w_ref, b_ref, lq1_ref, lk1_ref, lq2_ref, lk2_ref, mod_ref, lam_ref):
    c = c_ref[...]
    c_act = c * jax.nn.sigmoid(c)
    mod_ref[...] = jnp.dot(c_act, w_ref[...], precision=lax.Precision.HIGHEST,
                           preferred_element_type=F32) + b_ref[...]
    s1 = jnp.sum(lq1_ref[...] * lk1_ref[...], axis=-1, keepdims=True)
    s2 = jnp.sum(lq2_ref[...] * lk2_ref[...], axis=-1, keepdims=True)
    lam = jnp.exp(s1) - jnp.exp(s2) + LAMBDA_INIT
    lam_ref[...] = jnp.broadcast_to(lam, lam_ref.shape)


def _mod_call(c_pad, w_ada, b_ada, lq1, lk1, lq2, lk2):
    d = D_MODEL
    lam_spec = pl.BlockSpec((1, DA_QK_DIM), lambda j: (0, 0))
    return pl.pallas_call(
        _mod_kernel,
        grid=(3,),
        in_specs=[pl.BlockSpec((SUBLANES, d), lambda j: (0, 0)),
                  pl.BlockSpec((d, d), lambda j: (0, j)),
                  pl.BlockSpec((1, d), lambda j: (0, j)),
                  lam_spec, lam_spec, lam_spec, lam_spec],
        out_specs=[pl.BlockSpec((SUBLANES, d), lambda j: (0, j)),
                   pl.BlockSpec((SUBLANES, LANES), lambda j: (0, 0))],
        out_shape=[jax.ShapeDtypeStruct((SUBLANES, 3 * d), F32),
                   jax.ShapeDtypeStruct((SUBLANES, LANES), F32)],
        compiler_params=_params("arbitrary"),
    )(c_pad, w_ada, b_ada, lq1, lk1, lq2, lk2)


def _hnorm_kernel(x_ref, g_ref, scale_ref, shift_ref, h_ref):
    x = x_ref[0]
    ms = jnp.mean(x * x, axis=-1, keepdims=True)
    y = x * lax.rsqrt(ms + NORM_EPS) * g_ref[...]
    h_ref[0] = (y * (1.0 + scale_ref[0]) + shift_ref[0]).astype(BF16)


def _hnorm_call(x, g, scale, shift, bm):
    b, s, d = x.shape
    row = pl.BlockSpec((1, bm, d), lambda bi, i: (bi, i, 0))
    vec = pl.BlockSpec((1, 1, d), lambda bi, i: (bi, 0, 0))
    return pl.pallas_call(
        _hnorm_kernel,
        grid=(b, s // bm),
        in_specs=[row, pl.BlockSpec((1, d), lambda bi, i: (0, 0)), vec, vec],
        out_specs=row,
        out_shape=jax.ShapeDtypeStruct((b, s, d), BF16),
        compiler_params=_params("parallel", "parallel"),
    )(x, g, scale, shift)


def _rope_table_kernel(pos_ref, invf_ref, sgn_ref, cos_ref, sin_ref):
    ang = pos_ref[0] * invf_ref[...]
    cos_ref[0] = jnp.cos(ang)
    sin_ref[0] = jnp.sin(ang) * sgn_ref[...]


def _rope_table_call(pos_b, invf, sgn, bm):
    b, s, _ = pos_b.shape
    row = pl.BlockSpec((1, bm, LANES), lambda bi, i: (bi, i, 0))
    vec = pl.BlockSpec((1, LANES), lambda bi, i: (0, 0))
    shape = jax.ShapeDtypeStruct((b, s, LANES), F32)
    return pl.pallas_call(
        _rope_table_kernel,
        grid=(b, s // bm),
        in_specs=[row, vec, vec],
        out_specs=[row, row],
        out_shape=[shape, shape],
        compiler_params=_params("parallel", "parallel"),
    )(pos_b, invf, sgn)


def _proj_kernel(*refs, rope, scale, act, layout):
    if rope:
        h_ref, w_ref, cos_ref, sin_ref, o_ref = refs
    else:
        h_ref, w_ref, o_ref = refs
    acc = jnp.dot(h_ref[0], w_ref[...], preferred_element_type=F32)
    if act == "silu":
        acc = acc * jax.nn.sigmoid(acc)
    elif act == "sigmoid":
        acc = jax.nn.sigmoid(acc)
    if layout == "nat":
        o_ref[0] = acc.astype(BF16)
        return
    if rope:
        cos = cos_ref[0]
        sin = sin_ref[0]
        lane = lax.broadcasted_iota(jnp.int32, cos.shape, 1)
        first_half = (lane % DA_QK_DIM) < (DA_QK_DIM // 2)
    for hh in range(N_HEADS):
        t = acc[:, hh * HEAD_DIM:(hh + 1) * HEAD_DIM]
        if rope:
            partner = jnp.where(first_half,
                                pltpu.roll(t, HEAD_DIM - DA_QK_DIM // 2, 1),
                                pltpu.roll(t, DA_QK_DIM // 2, 1))
            t = t * cos + partner * sin
        if scale != 1.0:
            t = t * scale
        if layout == "heads":
            o_ref[0, hh] = t.astype(BF16)
        else:
            o_ref[0, hh, 0] = t.T.astype(BF16)


def _proj_call(h, w_all, col_block, tables, bm, *, rope=False, scale=1.0, act=None, layout="nat"):
    b, s, d = h.shape
    n = D_MODEL
    in_specs = [pl.BlockSpec((1, bm, d), lambda bi, i: (bi, i, 0)),
                pl.BlockSpec((d, n), lambda bi, i: (0, col_block))]
    args = [h, w_all]
    if rope:
        tab = pl.BlockSpec((1, bm, LANES), lambda bi, i: (bi, i, 0))
        in_specs += [tab, tab]
        args += list(tables)
    if layout == "nat":
        out_spec = pl.BlockSpec((1, bm, n), lambda bi, i: (bi, i, 0))
        out_shape = jax.ShapeDtypeStruct((b, s, n), BF16)
    elif layout == "heads":
        out_spec = pl.BlockSpec((1, N_HEADS, bm, HEAD_DIM), lambda bi, i: (bi, 0, i, 0))
        out_shape = jax.ShapeDtypeStruct((b, N_HEADS, s, HEAD_DIM), BF16)
    else:
        out_spec = pl.BlockSpec((1, N_HEADS, 1, HEAD_DIM, bm), lambda bi, i: (bi, 0, i, 0, 0))
        out_shape = jax.ShapeDtypeStruct((b, N_HEADS, s // bm, HEAD_DIM, bm), BF16)
    return pl.pallas_call(
        functools.partial(_proj_kernel, rope=rope, scale=scale, act=act, layout=layout),
        grid=(b, s // bm),
        in_specs=in_specs,
        out_specs=out_spec,
        out_shape=out_shape,
        compiler_params=_params("parallel", "parallel"),
    )(*args)


def _split3(v):
    hi = v.astype(BF16).astype(F32)
    r1 = v - hi
    mid = r1.astype(BF16).astype(F32)
    return hi, mid, r1 - mid


def _cum_kernel(h_ref, wf_ref, bf_ref, kext_ref, tot_ref):
    bm = h_ref.shape[1]
    logit = jnp.dot(h_ref[0], wf_ref[...], preferred_element_type=F32) + bf_ref[...]
    log_f = jnp.minimum(logit, 0.0) - jnp.log(1.0 + jnp.exp(-jnp.abs(logit)))
    r = lax.broadcasted_iota(jnp.int32, (bm, bm), 0)
    c = lax.broadcasted_iota(jnp.int32, (bm, bm), 1)
    tri = jnp.where(c <= r, 1.0, 0.0).astype(BF16)
    cum = sum(jnp.dot(tri, part.astype(BF16), preferred_element_type=F32) for part in _split3(log_f))
    bias = cum * (-LOG2E)
    tot_ref[0, 0] = jnp.broadcast_to(-bias[bm - 1:bm, :], tot_ref.shape[2:])
    lane = lax.broadcasted_iota(jnp.int32, (bm, LANES), 1)
    for hh in range(N_HEADS):
        parts = _split3(jnp.broadcast_to(bias[:, hh:hh + 1], (bm, LANES)))
        ext = jnp.zeros((bm, LANES), F32)
        for n, part in enumerate(parts):
            ext = jnp.where(lane == n, part, ext)
        kext_ref[0, hh] = ext.astype(BF16)


def _cum_call(h, wf_pad, bf_pad, bm):
    b, s, d = h.shape
    return pl.pallas_call(
        _cum_kernel,
        grid=(b, s // bm),
        in_specs=[pl.BlockSpec((1, bm, d), lambda bi, i: (bi, i, 0)),
                  pl.BlockSpec((d, LANES), lambda bi, i: (0, 0)),
                  pl.BlockSpec((1, LANES), lambda bi, i: (0, 0))],
        out_specs=[pl.BlockSpec((1, N_HEADS, bm, LANES), lambda bi, i: (bi, 0, i, 0)),
                   pl.BlockSpec((1, 1, SUBLANES, LANES), lambda bi, i: (bi, i, 0, 0))],
        out_shape=[jax.ShapeDtypeStruct((b, N_HEADS, s, LANES), BF16),
                   jax.ShapeDtypeStruct((b, s // bm, SUBLANES, LANES), F32)],
        compiler_params=_params("parallel", "parallel"),
    )(h, wf_pad, bf_pad)


PIPE_SLOTS = 3
DENOM_ROWS = 16


def _attend(i, n_maps, scores, vt_ref, visible, frame_shift, finish, m, acc, *bufs):
    s_buf, c_buf, p_buf, a_buf = (bufs[k * PIPE_SLOTS:(k + 1) * PIPE_SLOTS] for k in range(4))
    hd = acc.shape[1] - DENOM_ROWS
    ones = jnp.ones((DENOM_ROWS, vt_ref.shape[-1]), BF16)
    m[...] = jnp.full(m.shape, NEG_BIG, F32)
    acc[...] = jnp.zeros(acc.shape, F32)
    for slot in (PIPE_SLOTS - 2, PIPE_SLOTS - 1):
        p_buf[slot][...] = jnp.zeros(p_buf[slot].shape, BF16)
        a_buf[slot][...] = jnp.ones(a_buf[slot].shape, F32)

    def put_scores(slot, j):
        for n, s in enumerate(scores(j)):
            s_buf[slot][n] = s
            c_buf[slot][n] = jnp.max(s, axis=0, keepdims=True)

    def softmax(s, s_max, n):
        m_prev = m[n]
        m_new = jnp.maximum(m_prev, s_max)
        alpha = jnp.exp2(m_prev - m_new)
        p = jnp.exp2(s - m_new).astype(BF16)
        return m_new, alpha, p

    def value_product(prev, alpha, j, p):
        vt_ones = jnp.concatenate([vt_ref[0, 0, j], ones], axis=0)
        return alpha * prev + jnp.dot(vt_ones, p, preferred_element_type=F32)

    def step(j, slot, prefetch=True):
        if prefetch:
            put_scores((slot + 2) % PIPE_SLOTS, jnp.minimum(j + 2, i))
        old = (slot - 2) % PIPE_SLOTS
        for n in range(n_maps):
            acc[n] = value_product(acc[n], a_buf[old][n], jnp.maximum(j - 2, 0), p_buf[old][n])
            m_new, alpha, p = softmax(s_buf[slot][n], c_buf[slot][n], n)
            p_buf[slot][n] = p
            a_buf[slot][n] = alpha
            m[n] = m_new + frame_shift(j)

    def last(slot):
        results = []
        for n in range(n_maps):
            out = acc[n]
            for back in (2, 1):
                old = (slot - back) % PIPE_SLOTS
                out = value_product(out, a_buf[old][n], jnp.maximum(i - back, 0), p_buf[old][n])
            s = jnp.where(visible, s_buf[slot][n], -jnp.inf)
            _, alpha, p = softmax(s, jnp.max(s, axis=0, keepdims=True), n)
            out = value_product(out, alpha, i, p)
            results.append((out[:hd], jnp.mean(out[hd:], axis=0, keepdims=True)))
        finish(results)

    put_scores(0, 0)
    put_scores(1, jnp.minimum(1, i))

    def rotation(jj, carry):
        for slot in range(PIPE_SLOTS):
            step(PIPE_SLOTS * jj + slot, slot)
        return carry

    lax.fori_loop(0, i // PIPE_SLOTS, rotation, 0)

    for rem in range(PIPE_SLOTS):
        @pl.when(i % PIPE_SLOTS == rem)
        def _(rem=rem):
            for slot in range(rem):
                step(i - rem + slot, slot, prefetch=slot + 2 <= rem)
            last(rem)


def _attend_scratch(n_maps, hd, tk, tq):
    return ([pltpu.VMEM((n_maps, 1, tq), F32),
             pltpu.VMEM((n_maps, hd + DENOM_ROWS, tq), F32)]
            + [pltpu.VMEM((n_maps, tk, tq), F32)] * PIPE_SLOTS
            + [pltpu.VMEM((n_maps, 1, tq), F32)] * PIPE_SLOTS
            + [pltpu.VMEM((n_maps, tk, tq), BF16)] * PIPE_SLOTS
            + [pltpu.VMEM((n_maps, 1, tq), F32)] * PIPE_SLOTS)


def _diff_kernel(lam_ref, qt_ref, k_ref, vt_ref, g_ref, o_ref, *scratch):
    i = pl.program_id(2)
    t = qt_ref.shape[-1]
    qt = qt_ref[0, 0, 0]
    row = lax.broadcasted_iota(jnp.int32, qt.shape, 0)
    zero = jnp.zeros_like(qt)
    q12 = jnp.concatenate([jnp.where(row < DA_QK_DIM, qt, zero),
                           jnp.where(row >= DA_QK_DIM, qt, zero)], axis=1)

    def scores(j):
        return (jnp.dot(k_ref[0, 0, j], q12, preferred_element_type=F32),)

    kc = lax.broadcasted_iota(jnp.int32, (t, 2 * t), 0) // CHUNK
    qc = (lax.broadcasted_iota(jnp.int32, (t, 2 * t), 1) % t) // CHUNK

    def finish(results):
        ((out, l),) = results
        o = out / l
        o = o[:, :t] - lam_ref[0] * o[:, t:]
        ms = jnp.mean(o * o, axis=0, keepdims=True)
        g = jnp.concatenate([g_ref[...]] * (t // LANES), axis=1)
        y = o * lax.rsqrt(ms + SUBLN_EPS) * g
        o_ref[0, 0] = y.T.astype(BF16)

    _attend(i, 1, scores, vt_ref, kc <= qc, lambda j: 0.0, finish, *scratch)


def _diff_call(lam, qt, k, vt, g_rep):
    b, nh, nq, hd, t = qt.shape
    return pl.pallas_call(
        _diff_kernel,
        grid=(b, nh, nq),
        in_specs=[pl.BlockSpec(memory_space=pltpu.SMEM),
                  pl.BlockSpec((1, 1, 1, hd, t), lambda bi, h, i: (bi, h, i, 0, 0)),
                  pl.BlockSpec((1, 1, nq, t, hd), lambda bi, h, i: (bi, h, 0, 0, 0)),
                  pl.BlockSpec((1, 1, nq, hd, t), lambda bi, h, i: (bi, h, 0, 0, 0)),
                  pl.BlockSpec((hd, LANES), lambda bi, h, i: (0, 0))],
        out_specs=pl.BlockSpec((1, 1, t, hd), lambda bi, h, i: (bi, h, i, 0)),
        out_shape=jax.ShapeDtypeStruct((b, nh, nq * t, hd), BF16),
        scratch_shapes=_attend_scratch(1, hd, t, 2 * t),
        compiler_params=_params("parallel", "parallel", "arbitrary"),
    )(lam, qt, k, vt, g_rep)


def _fox_kernel(tot_ref, qt_ref, k_ref, kext_ref, vt_ref, o_ref, *scratch):
    bi, h, i = pl.program_id(0), pl.program_id(1), pl.program_id(2)
    nq = pl.num_programs(2)
    t = qt_ref.shape[-1]
    qt = qt_ref[0, 0, 0]
    row = lax.broadcasted_iota(jnp.int32, qt.shape, 0)
    q_cat = jnp.concatenate([qt, jnp.where(row < BIAS_TERMS, 1.0, 0.0).astype(BF16)], axis=0)
    tot_base = (bi * N_HEADS + h) * nq

    def scores(j):
        k_cat = jnp.concatenate([k_ref[0, 0, j], kext_ref[0, 0, j]], axis=1)
        return (jnp.dot(k_cat, q_cat, preferred_element_type=F32),)

    ki = lax.broadcasted_iota(jnp.int32, (t, t), 0)
    qi = lax.broadcasted_iota(jnp.int32, (t, t), 1)

    def finish(results):
        ((o, l),) = results
        o_ref[0, 0] = (o / l).T.astype(BF16)

    _attend(i, 1, scores, vt_ref, ki <= qi, lambda j: tot_ref[tot_base + j], finish, *scratch)


def _fox_call(tot, qt, k, kext, vt):
    b, nh, nq, hd, t = qt.shape
    return pl.pallas_call(
        _fox_kernel,
        grid=(b, nh, nq),
        in_specs=[pl.BlockSpec(memory_space=pltpu.SMEM),
                  pl.BlockSpec((1, 1, 1, hd, t), lambda bi, h, i: (bi, h, i, 0, 0)),
                  pl.BlockSpec((1, 1, nq, t, hd), lambda bi, h, i: (bi, h, 0, 0, 0)),
                  pl.BlockSpec((1, 1, nq, t, LANES), lambda bi, h, i: (bi, h, 0, 0, 0)),
                  pl.BlockSpec((1, 1, nq, hd, t), lambda bi, h, i: (bi, h, 0, 0, 0))],
        out_specs=pl.BlockSpec((1, 1, t, hd), lambda bi, h, i: (bi, h, i, 0)),
        out_shape=jax.ShapeDtypeStruct((b, nh, nq * t, hd), BF16),
        scratch_shapes=_attend_scratch(1, hd, t, t),
        compiler_params=_params("parallel", "parallel", "arbitrary"),
    )(tot, qt, k, kext, vt)


def _out_kernel(oa_ref, za_ref, ob_ref, zb_ref, ma_ref, mb_ref, x_ref, gate_ref,
                wa_ref, wb_ref, wo_ref, gf_ref, o_ref):
    def gated(o_ref_, z_ref_):
        o = jnp.concatenate([o_ref_[0, hh] for hh in range(N_HEADS)], axis=1)
        return (o.astype(F32) * z_ref_[0].astype(F32)).astype(BF16)

    ya = jnp.dot(gated(oa_ref, za_ref), wa_ref[...], preferred_element_type=F32)
    yb = jnp.dot(gated(ob_ref, zb_ref), wb_ref[...], preferred_element_type=F32)
    merged = ma_ref[0].astype(F32) * ya + mb_ref[0].astype(F32) * yb
    out = jnp.dot(merged.astype(BF16), wo_ref[...], preferred_element_type=F32)
    xn = x_ref[0] + gate_ref[0] * out
    ms = jnp.mean(xn * xn, axis=-1, keepdims=True)
    o_ref[0] = xn * lax.rsqrt(ms + NORM_EPS) * gf_ref[...]


def _out_call(oa, za, ob, zb, ma, mb, x, gate, wa, wb, wo, gf, bm):
    b, s, d = x.shape
    heads = pl.BlockSpec((1, N_HEADS, bm, HEAD_DIM), lambda bi, i: (bi, 0, i, 0))
    row = pl.BlockSpec((1, bm, d), lambda bi, i: (bi, i, 0))
    wspec = pl.BlockSpec((d, d), lambda bi, i: (0, 0))
    return pl.pallas_call(
        _out_kernel,
        grid=(b, s // bm),
        in_specs=[heads, row, heads, row, row, row, row,
                  pl.BlockSpec((1, 1, d), lambda bi, i: (bi, 0, 0)),
                  wspec, wspec, wspec,
                  pl.BlockSpec((1, d), lambda bi, i: (0, 0))],
        out_specs=row,
        out_shape=jax.ShapeDtypeStruct((b, s, d), F32),
        compiler_params=_params("parallel", "parallel"),
    )(oa, za, ob, zb, ma, mb, x, gate, wa, wb, wo, gf)


def kernel(x, c, positions, w_ada, b_ada, g_norm, w_in, b_forget, lambda_q1, lambda_k1, lambda_q2,
           lambda_k2, g_subln, w_branch_a, w_branch_b, w_out, g_final):
    b, s, d = x.shape
    assert d == D_MODEL and s % ATT_TILE == 0 and b <= SUBLANES
    assert w_ada.shape[0] == 1, "single-layer block"
    t = ATT_TILE
    nq = s // t
    n_main = 8 * D_MODEL

    c_pad = jnp.zeros((SUBLANES, d), F32).at[:b].set(c)
    w_cat = jnp.concatenate([w_in[0][:, :n_main], w_in[0][:, n_main + N_HEADS:]], axis=1).astype(BF16)
    wf_pad = jnp.zeros((d, LANES), F32).at[:, :N_HEADS].set(w_in[0][:, n_main:n_main + N_HEADS]).astype(BF16)
    bf_pad = jnp.zeros((1, LANES), F32).at[0, :N_HEADS].set(b_forget[0])
    half = DA_QK_DIM // 2
    inv_freq = ROPE_THETA ** (-jnp.arange(half, dtype=F32) / half)
    invf = jnp.tile(inv_freq, LANES // half)[None, :]
    sgn = jnp.tile(jnp.concatenate([-jnp.ones((half,), F32), jnp.ones((half,), F32)]),
                   LANES // DA_QK_DIM)[None, :]
    pos_b = jnp.broadcast_to(positions.astype(F32)[:, :, None], (b, s, LANES))
    g_rep = jnp.broadcast_to((g_subln[0] * (1.0 - LAMBDA_INIT))[:, None], (HEAD_DIM, LANES))

    mod, lam_tile = _mod_call(c_pad, w_ada[0], b_ada[0][None, :], lambda_q1[0][None, :],
                              lambda_k1[0][None, :], lambda_q2[0][None, :], lambda_k2[0][None, :])
    shift = mod[:b, None, 0:d]
    scale = mod[:b, None, d:2 * d]
    gate = mod[:b, None, 2 * d:3 * d]
    lam = lam_tile[0, :1]

    h = _hnorm_call(x, g_norm[0][None, :], scale, shift, t)
    tables = _rope_table_call(pos_b, invf, sgn, t)

    proj = functools.partial(_proj_call, h, w_cat)
    qa_t = proj(0, tables, t, rope=True, scale=LOG2E * DA_QK_DIM ** -0.5, layout="heads_t")
    ka = proj(1, tables, t, rope=True, layout="heads")
    va_t = proj(2, None, t, layout="heads_t")
    za = proj(3, None, t, act="silu")
    qb_t = proj(4, None, t, scale=LOG2E * HEAD_DIM ** -0.5, layout="heads_t")
    kb = proj(5, None, t, layout="heads")
    vb_t = proj(6, None, t, layout="heads_t")
    zb = proj(7, None, t, act="silu")
    ma = proj(8, None, t, act="sigmoid")
    mb = proj(9, None, t, act="sigmoid")

    kext, tot_tile = _cum_call(h, wf_pad, bf_pad, t)
    tot = tot_tile[:, :, 0, :N_HEADS].transpose(0, 2, 1).reshape(-1)

    def tiles(a):
        return a.reshape(b, N_HEADS, nq, t, a.shape[-1])

    oa = _diff_call(lam, qa_t, tiles(ka), va_t, g_rep)
    ob = _fox_call(tot, qb_t, tiles(kb), tiles(kext), vb_t)

    return _out_call(oa, za, ob, zb, ma, mb, x, gate,
                     w_branch_a[0].astype(BF16), w_branch_b[0].astype(BF16), w_out[0].astype(BF16),
                     g_final[None, :], t)
```

```python
import functools
import math

import jax
import jax.numpy as jnp
from jax import lax
from jax.experimental import pallas as pl
from jax.experimental.pallas import tpu as pltpu

F32 = jnp.float32
BF16 = jnp.bfloat16

D_MODEL = 1024
N_HEADS = 8
HEAD_DIM = 128
DA_QK_DIM = 64
ROPE_HALF = DA_QK_DIM // 2
CHUNK = 64
ROPE_THETA = 10000.0
NORM_EPS = 1e-6
SUBLN_EPS = 1e-5
LAMBDA_INIT = 0.8 - 0.6 * math.exp(-0.3 * 0)
LOG2E = math.log2(math.e)

LANES = 128
SUBLANES = 8
ATT_TILE = 512
PROJ_ROWS = 1024
BIAS_TERMS = 3
VMEM_LIMIT = 48 * 1024 * 1024
NEG_BIG = -1e30


def _params(*sem):
    return pltpu.CompilerParams(dimension_semantics=sem, vmem_limit_bytes=VMEM_LIMIT)


def _mod_kernel(c_ref, w_ref, b_ref, lq1_ref, lk1_ref, lq2_ref, lk2_ref, mod_ref, lam_ref):
    c = c_ref[...]
    c_act = c * jax.nn.sigmoid(c)
    mod_ref[...] = jnp.dot(c_act, w_ref[...], precision=lax.Precision.HIGHEST,
                           preferred_element_type=F32) + b_ref[...]
    s1 = jnp.sum(lq1_ref[...] * lk1_ref[...], axis=-1, keepdims=True)
    s2 = jnp.sum(lq2_ref[...] * lk2_ref[...], axis=-1, keepdims=True)
    lam = jnp.exp(s1) - jnp.exp(s2) + LAMBDA_INIT
    lam_ref[...] = jnp.broadcast_to(lam, lam_ref.shape)


def _mod_call(c_pad, w_ada, b_ada, lq1, lk1, lq2, lk2):
    d = D_MODEL
    lam_spec = pl.BlockSpec((1, DA_QK_DIM), lambda j: (0, 0))
    return pl.pallas_call(
        _mod_kernel,
        grid=(3,),
        in_specs=[pl.BlockSpec((SUBLANES, d), lambda j: (0, 0)),
                  pl.BlockSpec((d, d), lambda j: (0, j)),
                  pl.BlockSpec((1, d), lambda j: (0, j)),
                  lam_spec, lam_spec, lam_spec, lam_spec],
        out_specs=[pl.BlockSpec((SUBLANES, d), lambda j: (0, j)),
                   pl.BlockSpec((SUBLANES, LANES), lambda j: (0, 0))],
        out_shape=[jax.ShapeDtypeStruct((SUBLANES, 3 * d), F32),
                   jax.ShapeDtypeStruct((SUBLANES, LANES), F32)],
        compiler_params=_params("arbitrary"),
    )(c_pad, w_ada, b_ada, lq1, lk1, lq2, lk2)


def _hnorm_kernel(x_ref, g_ref, scale_ref, shift_ref, h_ref):
    x = x_ref[0]
    ms = jnp.mean(x * x, axis=-1, keepdims=True)
    y = x * lax.rsqrt(ms + NORM_EPS) * g_ref[...]
    h_ref[0] = (y * (1.0 + scale_ref[0]) + shift_ref[0]).astype(BF16)


def _hnorm_call(x, g, scale, shift, bm):
    b, s, d = x.shape
    row = pl.BlockSpec((1, bm, d), lambda bi, i: (bi, i, 0))
    vec = pl.BlockSpec((1, 1, d), lambda bi, i: (bi, 0, 0))
    return pl.pallas_call(
        _hnorm_kernel,
        grid=(b, s // bm),
        in_specs=[row, pl.BlockSpec((1, d), lambda bi, i: (0, 0)), vec, vec],
        out_specs=row,
        out_shape=jax.ShapeDtypeStruct((b, s, d), BF16),
        compiler_params=_params("parallel", "parallel"),
    )(x, g, scale, shift)


def _rope_table_kernel(pos_ref, invf_ref, sgn_ref, cos_ref, sin_ref):
    ang = pos_ref[0] * invf_ref[...]
    cos_ref[0] = jnp.cos(ang)
    sin_ref[0] = jnp.sin(ang) * sgn_ref[...]


def _rope_table_call(pos_b, invf, sgn, bm):
    b, s, _ = pos_b.shape
    row = pl.BlockSpec((1, bm, LANES), lambda bi, i: (bi, i, 0))
    vec = pl.BlockSpec((1, LANES), lambda bi, i: (0, 0))
    shape = jax.ShapeDtypeStruct((b, s, LANES), F32)
    return pl.pallas_call(
        _rope_table_kernel,
        grid=(b, s // bm),
        in_specs=[row, vec, vec],
        out_specs=[row, row],
        out_shape=[shape, shape],
        compiler_params=_params("parallel", "parallel"),
    )(pos_b, invf, sgn)


def _proj_kernel(*refs, rope, scale, act, layout):
    if rope:
        h_ref, w_ref, cos_ref, sin_ref, o_ref = refs
    else:
        h_ref, w_ref, o_ref = refs
    acc = jnp.dot(h_ref[0], w_ref[...], preferred_element_type=F32)
    if act == "silu":
        acc = acc * jax.nn.sigmoid(acc)
    elif act == "sigmoid":
        acc = jax.nn.sigmoid(acc)
    if layout == "nat":
        o_ref[0] = acc.astype(BF16)
        return
    if rope:
        cos = cos_ref[0]
        sin = sin_ref[0]
    for hh in range(N_HEADS):
        t = acc[:, hh * HEAD_DIM:(hh + 1) * HEAD_DIM]
        if rope:
            t = t * cos + pltpu.roll(t, HEAD_DIM // 2, 1) * sin
        if scale != 1.0:
            t = t * scale
        if layout == "heads":
            o_ref[0, hh] = t.astype(BF16)
        else:
            tile = o_ref.shape[-1]
            for part in range(o_ref.shape[2]):
                o_ref[0, hh, part] = t[part * tile:(part + 1) * tile].T.astype(BF16)


def _proj_call(h, w_all, col_block, tables, bm, *, rope=False, scale=1.0, act=None, layout="nat"):
    b, s, d = h.shape
    n = D_MODEL
    in_specs = [pl.BlockSpec((1, bm, d), lambda bi, i: (bi, i, 0)),
                pl.BlockSpec((d, n), lambda bi, i: (0, col_block))]
    args = [h, w_all]
    if rope:
        tab = pl.BlockSpec((1, bm, LANES), lambda bi, i: (bi, i, 0))
        in_specs += [tab, tab]
        args += list(tables)
    if layout == "nat":
        out_spec = pl.BlockSpec((1, bm, n), lambda bi, i: (bi, i, 0))
        out_shape = jax.ShapeDtypeStruct((b, s, n), BF16)
    elif layout == "heads":
        out_spec = pl.BlockSpec((1, N_HEADS, bm, HEAD_DIM), lambda bi, i: (bi, 0, i, 0))
        out_shape = jax.ShapeDtypeStruct((b, N_HEADS, s, HEAD_DIM), BF16)
    else:
        parts = bm // ATT_TILE
        out_spec = pl.BlockSpec((1, N_HEADS, parts, HEAD_DIM, ATT_TILE), lambda bi, i: (bi, 0, i, 0, 0))
        out_shape = jax.ShapeDtypeStruct((b, N_HEADS, s // ATT_TILE, HEAD_DIM, ATT_TILE), BF16)
    return pl.pallas_call(
        functools.partial(_proj_kernel, rope=rope, scale=scale, act=act, layout=layout),
        grid=(b, s // bm),
        in_specs=in_specs,
        out_specs=out_spec,
        out_shape=out_shape,
        compiler_params=_params("parallel", "parallel"),
    )(*args)


def _split3(v):
    hi = v.astype(BF16).astype(F32)
    r1 = v - hi
    mid = r1.astype(BF16).astype(F32)
    return hi, mid, r1 - mid


def _cum_kernel(h_ref, wf_ref, bf_ref, kext_ref, tot_ref):
    bm = h_ref.shape[1]
    logit = jnp.dot(h_ref[0], wf_ref[...], preferred_element_type=F32) + bf_ref[...]
    log_f = jnp.minimum(logit, 0.0) - jnp.log(1.0 + jnp.exp(-jnp.abs(logit)))
    r = lax.broadcasted_iota(jnp.int32, (bm, bm), 0)
    c = lax.broadcasted_iota(jnp.int32, (bm, bm), 1)
    tri = jnp.where(c <= r, 1.0, 0.0).astype(BF16)
    cum = sum(jnp.dot(tri, part.astype(BF16), preferred_element_type=F32) for part in _split3(log_f))
    bias = cum * (-LOG2E)
    tot_ref[0, 0] = jnp.broadcast_to(-bias[bm - 1:bm, :], tot_ref.shape[2:])
    lane = lax.broadcasted_iota(jnp.int32, (bm, LANES), 1)
    for hh in range(N_HEADS):
        parts = _split3(jnp.broadcast_to(bias[:, hh:hh + 1], (bm, LANES)))
        ext = jnp.zeros((bm, LANES), F32)
        for n, part in enumerate(parts):
            ext = jnp.where(lane == n, part, ext)
        kext_ref[0, hh] = ext.astype(BF16)


def _cum_call(h, wf_pad, bf_pad, bm):
    b, s, d = h.shape
    return pl.pallas_call(
        _cum_kernel,
        grid=(b, s // bm),
        in_specs=[pl.BlockSpec((1, bm, d), lambda bi, i: (bi, i, 0)),
                  pl.BlockSpec((d, LANES), lambda bi, i: (0, 0)),
                  pl.BlockSpec((1, LANES), lambda bi, i: (0, 0))],
        out_specs=[pl.BlockSpec((1, N_HEADS, bm, LANES), lambda bi, i: (bi, 0, i, 0)),
                   pl.BlockSpec((1, 1, SUBLANES, LANES), lambda bi, i: (bi, i, 0, 0))],
        out_shape=[jax.ShapeDtypeStruct((b, N_HEADS, s, LANES), BF16),
                   jax.ShapeDtypeStruct((b, s // bm, SUBLANES, LANES), F32)],
        compiler_params=_params("parallel", "parallel"),
    )(h, wf_pad, bf_pad)


PIPE_SLOTS = 3
DENOM_ROWS = 16


def _attend(i, n_maps, scores, next_scores, vt_ref, visible, frame_shift, finish, m, acc, *bufs):
    s_buf, c_buf, p_buf, a_buf = (bufs[k * PIPE_SLOTS:(k + 1) * PIPE_SLOTS] for k in range(4))
    hd = acc.shape[1] - DENOM_ROWS
    ones = jnp.ones((DENOM_ROWS, vt_ref.shape[-1]), BF16)
    m[...] = jnp.full(m.shape, NEG_BIG, F32)
    acc[...] = jnp.zeros(acc.shape, F32)
    for slot in (PIPE_SLOTS - 2, PIPE_SLOTS - 1):
        p_buf[slot][...] = jnp.zeros(p_buf[slot].shape, BF16)
        a_buf[slot][...] = jnp.ones(a_buf[slot].shape, F32)

    def put_scores(slot, tiles):
        for n, s in enumerate(tiles):
            s_buf[slot][n] = s
            c_buf[slot][n] = jnp.max(s, axis=0, keepdims=True)

    def softmax(s, s_max, n):
        m_prev = m[n]
        m_new = jnp.maximum(m_prev, s_max)
        alpha = jnp.exp2(m_prev - m_new)
        p = jnp.exp2(s - m_new).astype(BF16)
        return m_new, alpha, p

    def value_product(prev, alpha, j, p):
        vt_ones = jnp.concatenate([vt_ref[0, 0, j], ones], axis=0)
        return alpha * prev + jnp.dot(vt_ones, p, preferred_element_type=F32)

    def step(j, slot, prefetch=True):
        if prefetch:
            put_scores((slot + 2) % PIPE_SLOTS, scores(jnp.minimum(j + 2, i)))
        old = (slot - 2) % PIPE_SLOTS
        for n in range(n_maps):
            acc[n] = value_product(acc[n], a_buf[old][n], jnp.maximum(j - 2, 0), p_buf[old][n])
            m_new, alpha, p = softmax(s_buf[slot][n], c_buf[slot][n], n)
            p_buf[slot][n] = p
            a_buf[slot][n] = alpha
            m[n] = m_new + frame_shift(j)

    def last(slot):
        results = []
        for n in range(n_maps):
            out = acc[n]
            for back in (2, 1):
                old = (slot - back) % PIPE_SLOTS
                out = value_product(out, a_buf[old][n], jnp.maximum(i - back, 0), p_buf[old][n])
            s = jnp.where(visible, s_buf[slot][n], -jnp.inf)
            _, alpha, p = softmax(s, jnp.max(s, axis=0, keepdims=True), n)
            out = value_product(out, alpha, i, p)
            results.append((out[:hd], jnp.mean(out[hd:], axis=0, keepdims=True)))
        put_scores(0, next_scores(0))
        put_scores(1, next_scores(1))
        finish(results)

    @pl.when(i == 0)
    def _():
        put_scores(0, scores(0))

    def rotation(jj, carry):
        for slot in range(PIPE_SLOTS):
            step(PIPE_SLOTS * jj + slot, slot)
        return carry

    lax.fori_loop(0, i // PIPE_SLOTS, rotation, 0)

    for rem in range(PIPE_SLOTS):
        @pl.when(i % PIPE_SLOTS == rem)
        def _(rem=rem):
            for slot in range(rem):
                step(i - rem + slot, slot, prefetch=slot + 2 <= rem)
            last(rem)


def _attend_scratch(n_maps, hd, tk, tq):
    return ([pltpu.VMEM((n_maps, 1, tq), F32),
             pltpu.VMEM((n_maps, hd + DENOM_ROWS, tq), F32)]
            + [pltpu.VMEM((n_maps, tk, tq), F32)] * PIPE_SLOTS
            + [pltpu.VMEM((n_maps, 1, tq), F32)] * PIPE_SLOTS
            + [pltpu.VMEM((n_maps, tk, tq), BF16)] * PIPE_SLOTS
            + [pltpu.VMEM((n_maps, 1, tq), F32)] * PIPE_SLOTS)


def _diff_kernel(lam_ref, qt_ref, k_ref, vt_ref, g_ref, o_ref, *scratch):
    i = pl.program_id(2)
    nq, t = qt_ref.shape[2], qt_ref.shape[-1]

    def both_maps(block):
        qt = qt_ref[0, 0, block]
        row = lax.broadcasted_iota(jnp.int32, qt.shape, 0)
        map1 = (row // ROPE_HALF) % 2 == 0
        zero = jnp.zeros_like(qt)
        return jnp.concatenate([jnp.where(map1, qt, zero), jnp.where(map1, zero, qt)], axis=1)

    q12 = both_maps(i)

    def scores(j):
        return (jnp.dot(k_ref[0, 0, j], q12, preferred_element_type=F32),)

    def next_scores(j):
        return (jnp.dot(k_ref[0, 0, j], both_maps(jnp.minimum(i + 1, nq - 1)), preferred_element_type=F32),)

    kc = lax.broadcasted_iota(jnp.int32, (t, 2 * t), 0) // CHUNK
    qc = (lax.broadcasted_iota(jnp.int32, (t, 2 * t), 1) % t) // CHUNK

    def finish(results):
        ((out, l),) = results
        o = out / l
        o = o[:, :t] - lam_ref[0] * o[:, t:]
        ms = jnp.mean(o * o, axis=0, keepdims=True)
        g = jnp.concatenate([g_ref[...]] * (t // LANES), axis=1)
        y = o * lax.rsqrt(ms + SUBLN_EPS) * g
        o_ref[0, 0] = y.T.astype(BF16)

    _attend(i, 1, scores, next_scores, vt_ref, kc <= qc, lambda j: 0.0, finish, *scratch)


def _diff_call(lam, qt, k, vt, g_rep):
    b, nh, nq, hd, t = qt.shape
    return pl.pallas_call(
        _diff_kernel,
        grid=(b, nh, nq),
        in_specs=[pl.BlockSpec(memory_space=pltpu.SMEM),
                  pl.BlockSpec((1, 1, nq, hd, t), lambda bi, h, i: (bi, h, 0, 0, 0)),
                  pl.BlockSpec((1, 1, nq, t, hd), lambda bi, h, i: (bi, h, 0, 0, 0)),
                  pl.BlockSpec((1, 1, nq, hd, t), lambda bi, h, i: (bi, h, 0, 0, 0)),
                  pl.BlockSpec((hd, LANES), lambda bi, h, i: (0, 0))],
        out_specs=pl.BlockSpec((1, 1, t, hd), lambda bi, h, i: (bi, h, i, 0)),
        out_shape=jax.ShapeDtypeStruct((b, nh, nq * t, hd), BF16),
        scratch_shapes=_attend_scratch(1, hd, t, 2 * t),
        compiler_params=_params("parallel", "parallel", "arbitrary"),
    )(lam, qt, k, vt, g_rep)


def _fox_kernel(tot_ref, qt_ref, k_ref, kext_ref, vt_ref, o_ref, *scratch):
    bi, h, i = pl.program_id(0), pl.program_id(1), pl.program_id(2)
    nq, t = qt_ref.shape[2], qt_ref.shape[-1]
    tot_base = (bi * N_HEADS + h) * nq

    def with_ones(block):
        qt = qt_ref[0, 0, block]
        row = lax.broadcasted_iota(jnp.int32, qt.shape, 0)
        return jnp.concatenate([qt, jnp.where(row < BIAS_TERMS, 1.0, 0.0).astype(BF16)], axis=0)

    q_cat = with_ones(i)

    def scores_of(q, j):
        k_cat = jnp.concatenate([k_ref[0, 0, j], kext_ref[0, 0, j]], axis=1)
        return (jnp.dot(k_cat, q, preferred_element_type=F32),)

    def scores(j):
        return scores_of(q_cat, j)

    def next_scores(j):
        return scores_of(with_ones(jnp.minimum(i + 1, nq - 1)), j)

    ki = lax.broadcasted_iota(jnp.int32, (t, t), 0)
    qi = lax.broadcasted_iota(jnp.int32, (t, t), 1)

    def finish(results):
        ((o, l),) = results
        o_ref[0, 0] = (o / l).T.astype(BF16)

    _attend(i, 1, scores, next_scores, vt_ref, ki <= qi, lambda j: tot_ref[tot_base + j], finish, *scratch)


def _fox_call(tot, qt, k, kext, vt):
    b, nh, nq, hd, t = qt.shape
    return pl.pallas_call(
        _fox_kernel,
        grid=(b, nh, nq),
        in_specs=[pl.BlockSpec(memory_space=pltpu.SMEM),
                  pl.BlockSpec((1, 1, nq, hd, t), lambda bi, h, i: (bi, h, 0, 0, 0)),
                  pl.BlockSpec((1, 1, nq, t, hd), lambda bi, h, i: (bi, h, 0, 0, 0)),
                  pl.BlockSpec((1, 1, nq, t, LANES), lambda bi, h, i: (bi, h, 0, 0, 0)),
                  pl.BlockSpec((1, 1, nq, hd, t), lambda bi, h, i: (bi, h, 0, 0, 0))],
        out_specs=pl.BlockSpec((1, 1, t, hd), lambda bi, h, i: (bi, h, i, 0)),
        out_shape=jax.ShapeDtypeStruct((b, nh, nq * t, hd), BF16),
        scratch_shapes=_attend_scratch(1, hd, t, t),
        compiler_params=_params("parallel", "parallel", "arbitrary"),
    )(tot, qt, k, kext, vt)


def _out_kernel(oa_ref, za_ref, ob_ref, zb_ref, ma_ref, mb_ref, x_ref, gate_ref,
                wa_ref, wb_ref, wo_ref, gf_ref, o_ref):
    def gated(o_ref_, z_ref_):
        o = jnp.concatenate([o_ref_[0, hh] for hh in range(N_HEADS)], axis=1)
        return (o.astype(F32) * z_ref_[0].astype(F32)).astype(BF16)

    ya = jnp.dot(gated(oa_ref, za_ref), wa_ref[...], preferred_element_type=F32)
    yb = jnp.dot(gated(ob_ref, zb_ref), wb_ref[...], preferred_element_type=F32)
    merged = ma_ref[0].astype(F32) * ya + mb_ref[0].astype(F32) * yb
    out = jnp.dot(merged.astype(BF16), wo_ref[...], preferred_element_type=F32)
    xn = x_ref[0] + gate_ref[0] * out
    ms = jnp.mean(xn * xn, axis=-1, keepdims=True)
    o_ref[0] = xn * lax.rsqrt(ms + NORM_EPS) * gf_ref[...]


def _out_call(oa, za, ob, zb, ma, mb, x, gate, wa, wb, wo, gf, bm):
    b, s, d = x.shape
    heads = pl.BlockSpec((1, N_HEADS, bm, HEAD_DIM), lambda bi, i: (bi, 0, i, 0))
    row = pl.BlockSpec((1, bm, d), lambda bi, i: (bi, i, 0))
    wspec = pl.BlockSpec((d, d), lambda bi, i: (0, 0))
    return pl.pallas_call(
        _out_kernel,
        grid=(b, s // bm),
        in_specs=[heads, row, heads, row, row, row, row,
                  pl.BlockSpec((1, 1, d), lambda bi, i: (bi, 0, 0)),
                  wspec, wspec, wspec,
                  pl.BlockSpec((1, d), lambda bi, i: (0, 0))],
        out_specs=row,
        out_shape=jax.ShapeDtypeStruct((b, s, d), F32),
        compiler_params=_params("parallel", "parallel"),
    )(oa, za, ob, zb, ma, mb, x, gate, wa, wb, wo, gf)


def kernel(x, c, positions, w_ada, b_ada, g_norm, w_in, b_forget, lambda_q1, lambda_k1, lambda_q2,
           lambda_k2, g_subln, w_branch_a, w_branch_b, w_out, g_final):
    b, s, d = x.shape
    assert d == D_MODEL and s % ATT_TILE == 0 and b <= SUBLANES
    assert w_ada.shape[0] == 1, "single-layer block"
    t = ATT_TILE
    nq = s // t
    n_main = 8 * D_MODEL

    c_pad = jnp.zeros((SUBLANES, d), F32).at[:b].set(c)
    w_qk = w_in[0][:, :2 * D_MODEL].reshape(d, 2 * N_HEADS, 2, 2, ROPE_HALF).transpose(0, 1, 3, 2, 4)
    w_cat = jnp.concatenate([w_qk.reshape(d, 2 * D_MODEL), w_in[0][:, 2 * D_MODEL:n_main],
                             w_in[0][:, n_main + N_HEADS:]], axis=1).astype(BF16)
    wf_pad = jnp.zeros((d, LANES), F32).at[:, :N_HEADS].set(w_in[0][:, n_main:n_main + N_HEADS]).astype(BF16)
    bf_pad = jnp.zeros((1, LANES), F32).at[0, :N_HEADS].set(b_forget[0])
    inv_freq = ROPE_THETA ** (-jnp.arange(ROPE_HALF, dtype=F32) / ROPE_HALF)
    invf = jnp.tile(inv_freq, LANES // ROPE_HALF)[None, :]
    sgn = jnp.concatenate([-jnp.ones((HEAD_DIM // 2,), F32), jnp.ones((HEAD_DIM // 2,), F32)])[None, :]
    pos_b = jnp.broadcast_to(positions.astype(F32)[:, :, None], (b, s, LANES))
    g_rep = jnp.broadcast_to((g_subln[0] * (1.0 - LAMBDA_INIT))[:, None], (HEAD_DIM, LANES))

    mod, lam_tile = _mod_call(c_pad, w_ada[0], b_ada[0][None, :], lambda_q1[0][None, :],
                              lambda_k1[0][None, :], lambda_q2[0][None, :], lambda_k2[0][None, :])
    shift = mod[:b, None, 0:d]
    scale = mod[:b, None, d:2 * d]
    gate = mod[:b, None, 2 * d:3 * d]
    lam = lam_tile[0, :1]

    h = _hnorm_call(x, g_norm[0][None, :], scale, shift, t)
    tables = _rope_table_call(pos_b, invf, sgn, t)

    pm = PROJ_ROWS if s % PROJ_ROWS == 0 else t
    proj = functools.partial(_proj_call, h, w_cat)
    qa_t = proj(0, tables, pm, rope=True, scale=LOG2E * DA_QK_DIM ** -0.5, layout="heads_t")
    ka = proj(1, tables, pm, rope=True, layout="heads")
    va_t = proj(2, None, pm, layout="heads_t")
    za = proj(3, None, pm, act="silu")
    qb_t = proj(4, None, pm, scale=LOG2E * HEAD_DIM ** -0.5, layout="heads_t")
    kb = proj(5, None, pm, layout="heads")
    vb_t = proj(6, None, pm, layout="heads_t")
    zb = proj(7, None, pm, act="silu")
    ma = proj(8, None, pm, act="sigmoid")
    mb = proj(9, None, pm, act="sigmoid")

    kext, tot_tile = _cum_call(h, wf_pad, bf_pad, t)
    tot = tot_tile[:, :, 0, :N_HEADS].transpose(0, 2, 1).reshape(-1)

    def tiles(a):
        return a.reshape(b, N_HEADS, nq, t, a.shape[-1])

    oa = _diff_call(lam, qa_t, tiles(ka), va_t, g_rep)
    ob = _fox_call(tot, qb_t, tiles(kb), tiles(kext), vb_t)

    return _out_call(oa, za, ob, zb, ma, mb, x, gate,
                     w_branch_a[0].astype(BF16), w_branch_b[0].astype(BF16), w_out[0].astype(BF16),
                     g_final[None, :], t)
```

```python
import functools
import math

import jax
import jax.numpy as jnp
import numpy as np
from jax import lax
from jax.experimental import pallas as pl
from jax.experimental.pallas import tpu as pltpu

F32 = jnp.float32
BF16 = jnp.bfloat16

D_MODEL = 1024
N_HEADS = 8
HEAD_DIM = 128
DA_QK_DIM = 64
ROPE_HALF = DA_QK_DIM // 2
CHUNK = 64
ROPE_THETA = 10000.0
NORM_EPS = 1e-6
SUBLN_EPS = 1e-5
LAMBDA_INIT = 0.8 - 0.6 * math.exp(-0.3 * 0)
LOG2E = math.log2(math.e)

LANES = 128
SUBLANES = 8
ATT_TILE = 512
PROJ_ROWS = 1024
BIAS_TERMS = 3
VMEM_LIMIT = 48 * 1024 * 1024
NEG_BIG = -1e30


def _params(*sem):
    return pltpu.CompilerParams(dimension_semantics=sem, vmem_limit_bytes=VMEM_LIMIT)


def _mod_kernel(c_ref, w_ref, b_ref, lq1_ref, lk1_ref, lq2_ref, lk2_ref, mod_ref, lam_ref):
    c = c_ref[...]
    c_act = c * jax.nn.sigmoid(c)
    mod_ref[...] = jnp.dot(c_act, w_ref[...], precision=lax.Precision.HIGHEST,
                           preferred_element_type=F32) + b_ref[...]
    s1 = jnp.sum(lq1_ref[...] * lk1_ref[...], axis=-1, keepdims=True)
    s2 = jnp.sum(lq2_ref[...] * lk2_ref[...], axis=-1, keepdims=True)
    lam = jnp.exp(s1) - jnp.exp(s2) + LAMBDA_INIT
    lam_ref[...] = jnp.broadcast_to(lam, lam_ref.shape)


def _mod_call(c_pad, w_ada, b_ada, lq1, lk1, lq2, lk2):
    d = D_MODEL
    lam_spec = pl.BlockSpec((1, DA_QK_DIM), lambda j: (0, 0))
    return pl.pallas_call(
        _mod_kernel,
        grid=(3,),
        in_specs=[pl.BlockSpec((SUBLANES, d), lambda j: (0, 0)),
                  pl.BlockSpec((d, d), lambda j: (0, j)),
                  pl.BlockSpec((1, d), lambda j: (0, j)),
                  lam_spec, lam_spec, lam_spec, lam_spec],
        out_specs=[pl.BlockSpec((SUBLANES, d), lambda j: (0, j)),
                   pl.BlockSpec((SUBLANES, LANES), lambda j: (0, 0))],
        out_shape=[jax.ShapeDtypeStruct((SUBLANES, 3 * d), F32),
                   jax.ShapeDtypeStruct((SUBLANES, LANES), F32)],
        compiler_params=_params("arbitrary"),
    )(c_pad, w_ada, b_ada, lq1, lk1, lq2, lk2)


def _hnorm_kernel(x_ref, g_ref, scale_ref, shift_ref, h_ref):
    x = x_ref[0]
    ms = jnp.mean(x * x, axis=-1, keepdims=True)
    y = x * lax.rsqrt(ms + NORM_EPS) * g_ref[...]
    h_ref[0] = (y * (1.0 + scale_ref[0]) + shift_ref[0]).astype(BF16)


def _hnorm_call(x, g, scale, shift, bm):
    b, s, d = x.shape
    row = pl.BlockSpec((1, bm, d), lambda bi, i: (bi, i, 0))
    vec = pl.BlockSpec((1, 1, d), lambda bi, i: (bi, 0, 0))
    return pl.pallas_call(
        _hnorm_kernel,
        grid=(b, s // bm),
        in_specs=[row, pl.BlockSpec((1, d), lambda bi, i: (0, 0)), vec, vec],
        out_specs=row,
        out_shape=jax.ShapeDtypeStruct((b, s, d), BF16),
        compiler_params=_params("parallel", "parallel"),
    )(x, g, scale, shift)


def _rope_table_kernel(pos_ref, invf_ref, sgn_ref, cos_ref, sin_ref):
    ang = pos_ref[0] * invf_ref[...]
    cos_ref[0] = jnp.cos(ang)
    sin_ref[0] = jnp.sin(ang) * sgn_ref[...]


def _rope_table_call(pos_b, invf, sgn, bm):
    b, s, _ = pos_b.shape
    row = pl.BlockSpec((1, bm, LANES), lambda bi, i: (bi, i, 0))
    vec = pl.BlockSpec((1, LANES), lambda bi, i: (0, 0))
    shape = jax.ShapeDtypeStruct((b, s, LANES), F32)
    return pl.pallas_call(
        _rope_table_kernel,
        grid=(b, s // bm),
        in_specs=[row, vec, vec],
        out_specs=[row, row],
        out_shape=[shape, shape],
        compiler_params=_params("parallel", "parallel"),
    )(pos_b, invf, sgn)


def _proj_kernel(*refs, rope, scale, act, layout):
    if rope:
        h_ref, w_ref, cos_ref, sin_ref, o_ref = refs
    else:
        h_ref, w_ref, o_ref = refs
    acc = jnp.dot(h_ref[0], w_ref[...].astype(BF16), preferred_element_type=F32)
    if act == "silu":
        acc = acc * jax.nn.sigmoid(acc)
    elif act == "sigmoid":
        acc = jax.nn.sigmoid(acc)
    if layout == "nat":
        o_ref[0] = acc.astype(BF16)
        return
    if rope:
        cos = cos_ref[0]
        sin = sin_ref[0]
    for hh in range(N_HEADS):
        t = acc[:, hh * HEAD_DIM:(hh + 1) * HEAD_DIM]
        if rope:
            t = t * cos + pltpu.roll(t, HEAD_DIM // 2, 1) * sin
        if scale != 1.0:
            t = t * scale
        if layout == "heads":
            o_ref[0, hh] = t.astype(BF16)
        else:
            tile = o_ref.shape[-1]
            for part in range(o_ref.shape[2]):
                o_ref[0, hh, part] = t[part * tile:(part + 1) * tile].T.astype(BF16)


def _proj_call(h, w_all, col_block, tables, bm, *, rope=False, scale=1.0, act=None, layout="nat"):
    b, s, d = h.shape
    n = D_MODEL
    in_specs = [pl.BlockSpec((1, bm, d), lambda bi, i: (bi, i, 0)),
                pl.BlockSpec((d, n), lambda bi, i: (0, col_block))]
    args = [h, w_all]
    if rope:
        tab = pl.BlockSpec((1, bm, LANES), lambda bi, i: (bi, i, 0))
        in_specs += [tab, tab]
        args += list(tables)
    if layout == "nat":
        out_spec = pl.BlockSpec((1, bm, n), lambda bi, i: (bi, i, 0))
        out_shape = jax.ShapeDtypeStruct((b, s, n), BF16)
    elif layout == "heads":
        out_spec = pl.BlockSpec((1, N_HEADS, bm, HEAD_DIM), lambda bi, i: (bi, 0, i, 0))
        out_shape = jax.ShapeDtypeStruct((b, N_HEADS, s, HEAD_DIM), BF16)
    else:
        parts = bm // ATT_TILE
        out_spec = pl.BlockSpec((1, N_HEADS, parts, HEAD_DIM, ATT_TILE), lambda bi, i: (bi, 0, i, 0, 0))
        out_shape = jax.ShapeDtypeStruct((b, N_HEADS, s // ATT_TILE, HEAD_DIM, ATT_TILE), BF16)
    return pl.pallas_call(
        functools.partial(_proj_kernel, rope=rope, scale=scale, act=act, layout=layout),
        grid=(b, s // bm),
        in_specs=in_specs,
        out_specs=out_spec,
        out_shape=out_shape,
        compiler_params=_params("parallel", "parallel"),
    )(*args)


def _split3(v):
    hi = v.astype(BF16).astype(F32)
    r1 = v - hi
    mid = r1.astype(BF16).astype(F32)
    return hi, mid, r1 - mid


def _cum_kernel(h_ref, wf_ref, bf_ref, sel_ref, kext_ref, tot_ref):
    bm = h_ref.shape[1]
    logit = jnp.dot(h_ref[0], wf_ref[...], preferred_element_type=F32) + bf_ref[...]
    log_f = jnp.minimum(logit, 0.0) - jnp.log(1.0 + jnp.exp(-jnp.abs(logit)))
    r = lax.broadcasted_iota(jnp.int32, (bm, bm), 0)
    c = lax.broadcasted_iota(jnp.int32, (bm, bm), 1)
    tri = jnp.where(c <= r, 1.0, 0.0).astype(BF16)
    terms = jnp.concatenate([part.astype(BF16) for part in _split3(log_f)], axis=1)
    sums = jnp.dot(tri, terms, preferred_element_type=F32)
    cum = sum(sums[:, n * LANES:(n + 1) * LANES] for n in range(BIAS_TERMS))
    bias = cum * (-LOG2E)
    tot_ref[0, 0] = jnp.broadcast_to(-bias[bm - 1:bm, :], tot_ref.shape[2:])
    parts = jnp.concatenate([part.astype(BF16) for part in _split3(bias)], axis=1)
    ext = jnp.dot(parts, sel_ref[...], preferred_element_type=F32)
    for hh in range(N_HEADS):
        kext_ref[0, hh] = ext[:, hh * LANES:(hh + 1) * LANES].astype(BF16)


def _bias_selection():
    sel = np.zeros((BIAS_TERMS, LANES, N_HEADS, LANES), np.float32)
    for n in range(BIAS_TERMS):
        for hh in range(N_HEADS):
            sel[n, hh, hh, n] = 1.0
    return sel.reshape(BIAS_TERMS * LANES, N_HEADS * LANES)


def _cum_call(h, wf_pad, bf_pad, bm):
    b, s, d = h.shape
    sel = jnp.asarray(_bias_selection(), BF16)
    return pl.pallas_call(
        _cum_kernel,
        grid=(b, s // bm),
        in_specs=[pl.BlockSpec((1, bm, d), lambda bi, i: (bi, i, 0)),
                  pl.BlockSpec((d, LANES), lambda bi, i: (0, 0)),
                  pl.BlockSpec((1, LANES), lambda bi, i: (0, 0)),
                  pl.BlockSpec(sel.shape, lambda bi, i: (0, 0))],
        out_specs=[pl.BlockSpec((1, N_HEADS, bm, LANES), lambda bi, i: (bi, 0, i, 0)),
                   pl.BlockSpec((1, 1, SUBLANES, LANES), lambda bi, i: (bi, i, 0, 0))],
        out_shape=[jax.ShapeDtypeStruct((b, N_HEADS, s, LANES), BF16),
                   jax.ShapeDtypeStruct((b, s // bm, SUBLANES, LANES), F32)],
        compiler_params=_params("parallel", "parallel"),
    )(h, wf_pad, bf_pad, sel)


PIPE_SLOTS = 3
PV_LAG = 2
DENOM_ROWS = 16


def _attend(i, n_maps, scores, next_scores, vt_ref, visible, frame_shift, finish, m, acc, *bufs):
    s_buf, c_buf, p_buf, a_buf = (bufs[k * PIPE_SLOTS:(k + 1) * PIPE_SLOTS] for k in range(4))
    hd = acc.shape[1] - DENOM_ROWS
    ones = jnp.ones((DENOM_ROWS, vt_ref.shape[-1]), BF16)
    m[...] = jnp.full(m.shape, NEG_BIG, F32)
    acc[...] = jnp.zeros(acc.shape, F32)
    for slot in range(PIPE_SLOTS - PV_LAG, PIPE_SLOTS):
        p_buf[slot][...] = jnp.zeros(p_buf[slot].shape, BF16)
        a_buf[slot][...] = jnp.ones(a_buf[slot].shape, F32)

    def put_scores(slot, tiles):
        for n, s in enumerate(tiles):
            s_buf[slot][n] = s
            c_buf[slot][n] = jnp.max(s, axis=0, keepdims=True)

    def softmax(s, s_max, n):
        m_prev = m[n]
        m_new = jnp.maximum(m_prev, s_max)
        alpha = jnp.exp2(m_prev - m_new)
        p = jnp.exp2(s - m_new).astype(BF16)
        return m_new, alpha, p

    def value_product(prev, alpha, j, p):
        vt_ones = jnp.concatenate([vt_ref[0, 0, j], ones], axis=0)
        return alpha * prev + jnp.dot(vt_ones, p, preferred_element_type=F32)

    def step(j, slot, prefetch=True):
        if prefetch:
            put_scores((slot + 2) % PIPE_SLOTS, scores(jnp.minimum(j + 2, i)))
        old = (slot - PV_LAG) % PIPE_SLOTS
        for n in range(n_maps):
            acc[n] = value_product(acc[n], a_buf[old][n], jnp.maximum(j - PV_LAG, 0), p_buf[old][n])
            m_new, alpha, p = softmax(s_buf[slot][n], c_buf[slot][n], n)
            p_buf[slot][n] = p
            a_buf[slot][n] = alpha
            m[n] = m_new + frame_shift(j)

    def last(slot):
        results = []
        for n in range(n_maps):
            out = acc[n]
            for back in range(PV_LAG, 0, -1):
                old = (slot - back) % PIPE_SLOTS
                out = value_product(out, a_buf[old][n], jnp.maximum(i - back, 0), p_buf[old][n])
            s = jnp.where(visible, s_buf[slot][n], -jnp.inf)
            _, alpha, p = softmax(s, jnp.max(s, axis=0, keepdims=True), n)
            out = value_product(out, alpha, i, p)
            results.append((out[:hd], jnp.mean(out[hd:], axis=0, keepdims=True)))
        put_scores(0, next_scores(0))
        put_scores(1, next_scores(1))
        finish(results)

    @pl.when(i == 0)
    def _():
        put_scores(0, scores(0))

    def rotation(jj, carry):
        for slot in range(PIPE_SLOTS):
            step(PIPE_SLOTS * jj + slot, slot)
        return carry

    rotations = i // PIPE_SLOTS

    def two_rotations(u, carry):
        return rotation(2 * u + 1, rotation(2 * u, carry))

    lax.fori_loop(0, rotations // 2, two_rotations, 0)

    @pl.when(rotations % 2 == 1)
    def _():
        rotation(rotations - 1, 0)

    for rem in range(PIPE_SLOTS):
        @pl.when(i % PIPE_SLOTS == rem)
        def _(rem=rem):
            for slot in range(rem):
                step(i - rem + slot, slot, prefetch=slot + 2 <= rem)
            last(rem)


def _attend_scratch(n_maps, hd, tk, tq):
    return ([pltpu.VMEM((n_maps, 1, tq), F32),
             pltpu.VMEM((n_maps, hd + DENOM_ROWS, tq), F32)]
            + [pltpu.VMEM((n_maps, tk, tq), F32)] * PIPE_SLOTS
            + [pltpu.VMEM((n_maps, 1, tq), F32)] * PIPE_SLOTS
            + [pltpu.VMEM((n_maps, tk, tq), BF16)] * PIPE_SLOTS
            + [pltpu.VMEM((n_maps, 1, tq), F32)] * PIPE_SLOTS)


def _diff_kernel(lam_ref, qt_ref, k_ref, vt_ref, g_ref, o_ref, *scratch):
    i = pl.program_id(2)
    nq, t = qt_ref.shape[2], qt_ref.shape[-1]

    def both_maps(block):
        qt = qt_ref[0, 0, block]
        row = lax.broadcasted_iota(jnp.int32, qt.shape, 0)
        map1 = (row // ROPE_HALF) % 2 == 0
        zero = jnp.zeros_like(qt)
        return jnp.concatenate([jnp.where(map1, qt, zero), jnp.where(map1, zero, qt)], axis=1)

    q12 = both_maps(i)

    def scores(j):
        return (jnp.dot(k_ref[0, 0, j], q12, preferred_element_type=F32),)

    def next_scores(j):
        return (jnp.dot(k_ref[0, 0, j], both_maps(jnp.minimum(i + 1, nq - 1)), preferred_element_type=F32),)

    kc = lax.broadcasted_iota(jnp.int32, (t, 2 * t), 0) // CHUNK
    qc = (lax.broadcasted_iota(jnp.int32, (t, 2 * t), 1) % t) // CHUNK

    def finish(results):
        ((out, l),) = results
        o = out / l
        o = o[:, :t] - lam_ref[0] * o[:, t:]
        ms = jnp.mean(o * o, axis=0, keepdims=True)
        g = jnp.concatenate([g_ref[...]] * (t // LANES), axis=1)
        y = o * lax.rsqrt(ms + SUBLN_EPS) * g
        o_ref[0, 0] = y.T.astype(BF16)

    _attend(i, 1, scores, next_scores, vt_ref, kc <= qc, lambda j: 0.0, finish, *scratch)


def _diff_call(lam, qt, k, vt, g_rep):
    b, nh, nq, hd, t = qt.shape
    return pl.pallas_call(
        _diff_kernel,
        grid=(b, nh, nq),
        in_specs=[pl.BlockSpec(memory_space=pltpu.SMEM),
                  pl.BlockSpec((1, 1, nq, hd, t), lambda bi, h, i: (bi, h, 0, 0, 0)),
                  pl.BlockSpec((1, 1, nq, t, hd), lambda bi, h, i: (bi, h, 0, 0, 0)),
                  pl.BlockSpec((1, 1, nq, hd, t), lambda bi, h, i: (bi, h, 0, 0, 0)),
                  pl.BlockSpec((hd, LANES), lambda bi, h, i: (0, 0))],
        out_specs=pl.BlockSpec((1, 1, t, hd), lambda bi, h, i: (bi, h, i, 0)),
        out_shape=jax.ShapeDtypeStruct((b, nh, nq * t, hd), BF16),
        scratch_shapes=_attend_scratch(1, hd, t, 2 * t),
        compiler_params=_params("parallel", "parallel", "arbitrary"),
    )(lam, qt, k, vt, g_rep)


def _fox_kernel(tot_ref, qt_ref, k_ref, kext_ref, vt_ref, o_ref, *scratch):
    bi, h, i = pl.program_id(0), pl.program_id(1), pl.program_id(2)
    nq, t = qt_ref.shape[2], qt_ref.shape[-1]
    tot_base = (bi * N_HEADS + h) * nq

    def with_ones(block):
        qt = qt_ref[0, 0, block]
        row = lax.broadcasted_iota(jnp.int32, qt.shape, 0)
        return jnp.concatenate([qt, jnp.where(row < BIAS_TERMS, 1.0, 0.0).astype(BF16)], axis=0)

    q_cat = with_ones(i)

    def scores_of(q, j):
        k_cat = jnp.concatenate([k_ref[0, 0, j], kext_ref[0, 0, j]], axis=1)
        return (jnp.dot(k_cat, q, preferred_element_type=F32),)

    def scores(j):
        return scores_of(q_cat, j)

    def next_scores(j):
        return scores_of(with_ones(jnp.minimum(i + 1, nq - 1)), j)

    ki = lax.broadcasted_iota(jnp.int32, (t, t), 0)
    qi = lax.broadcasted_iota(jnp.int32, (t, t), 1)

    def finish(results):
        ((o, l),) = results
        o_ref[0, 0] = (o / l).T.astype(BF16)

    _attend(i, 1, scores, next_scores, vt_ref, ki <= qi, lambda j: tot_ref[tot_base + j], finish, *scratch)


def _fox_call(tot, qt, k, kext, vt):
    b, nh, nq, hd, t = qt.shape
    return pl.pallas_call(
        _fox_kernel,
        grid=(b, nh, nq),
        in_specs=[pl.BlockSpec(memory_space=pltpu.SMEM),
                  pl.BlockSpec((1, 1, nq, hd, t), lambda bi, h, i: (bi, h, 0, 0, 0)),
                  pl.BlockSpec((1, 1, nq, t, hd), lambda bi, h, i: (bi, h, 0, 0, 0)),
                  pl.BlockSpec((1, 1, nq, t, LANES), lambda bi, h, i: (bi, h, 0, 0, 0)),
                  pl.BlockSpec((1, 1, nq, hd, t), lambda bi, h, i: (bi, h, 0, 0, 0))],
        out_specs=pl.BlockSpec((1, 1, t, hd), lambda bi, h, i: (bi, h, i, 0)),
        out_shape=jax.ShapeDtypeStruct((b, nh, nq * t, hd), BF16),
        scratch_shapes=_attend_scratch(1, hd, t, t),
        compiler_params=_params("parallel", "parallel", "arbitrary"),
    )(tot, qt, k, kext, vt)


def _out_kernel(oa_ref, za_ref, ob_ref, zb_ref, ma_ref, mb_ref, x_ref, gate_ref,
                wa_ref, wb_ref, wo_ref, gf_ref, o_ref):
    def gated(o_ref_, z_ref_):
        o = jnp.concatenate([o_ref_[0, hh] for hh in range(N_HEADS)], axis=1)
        return (o.astype(F32) * z_ref_[0].astype(F32)).astype(BF16)

    ya = jnp.dot(gated(oa_ref, za_ref), wa_ref[...], preferred_element_type=F32)
    yb = jnp.dot(gated(ob_ref, zb_ref), wb_ref[...], preferred_element_type=F32)
    merged = ma_ref[0].astype(F32) * ya + mb_ref[0].astype(F32) * yb
    out = jnp.dot(merged.astype(BF16), wo_ref[...], preferred_element_type=F32)
    xn = x_ref[0] + gate_ref[0] * out
    ms = jnp.mean(xn * xn, axis=-1, keepdims=True)
    o_ref[0] = xn * lax.rsqrt(ms + NORM_EPS) * gf_ref[...]


def _out_call(oa, za, ob, zb, ma, mb, x, gate, wa, wb, wo, gf, bm):
    b, s, d = x.shape
    heads = pl.BlockSpec((1, N_HEADS, bm, HEAD_DIM), lambda bi, i: (bi, 0, i, 0))
    row = pl.BlockSpec((1, bm, d), lambda bi, i: (bi, i, 0))
    wspec = pl.BlockSpec((d, d), lambda bi, i: (0, 0))
    return pl.pallas_call(
        _out_kernel,
        grid=(b, s // bm),
        in_specs=[heads, row, heads, row, row, row, row,
                  pl.BlockSpec((1, 1, d), lambda bi, i: (bi, 0, 0)),
                  wspec, wspec, wspec,
                  pl.BlockSpec((1, d), lambda bi, i: (0, 0))],
        out_specs=row,
        out_shape=jax.ShapeDtypeStruct((b, s, d), F32),
        compiler_params=_params("parallel", "parallel"),
    )(oa, za, ob, zb, ma, mb, x, gate, wa, wb, wo, gf)


def kernel(x, c, positions, w_ada, b_ada, g_norm, w_in, b_forget, lambda_q1, lambda_k1, lambda_q2,
           lambda_k2, g_subln, w_branch_a, w_branch_b, w_out, g_final):
    b, s, d = x.shape
    assert d == D_MODEL and s % ATT_TILE == 0 and b <= SUBLANES
    assert w_ada.shape[0] == 1, "single-layer block"
    t = ATT_TILE
    nq = s // t
    n_main = 8 * D_MODEL

    c_pad = jnp.zeros((SUBLANES, d), F32).at[:b].set(c)
    w_qk = w_in[0][:, :2 * D_MODEL].reshape(d, 2 * N_HEADS, 2, 2, ROPE_HALF).transpose(0, 1, 3, 2, 4)
    w_qk = w_qk.reshape(d, 2 * D_MODEL)
    w_merge = w_in[0][:, n_main + N_HEADS:]
    wf_pad = jnp.zeros((d, LANES), F32).at[:, :N_HEADS].set(w_in[0][:, n_main:n_main + N_HEADS]).astype(BF16)
    bf_pad = jnp.zeros((1, LANES), F32).at[0, :N_HEADS].set(b_forget[0])
    inv_freq = ROPE_THETA ** (-jnp.arange(ROPE_HALF, dtype=F32) / ROPE_HALF)
    invf = jnp.tile(inv_freq, LANES // ROPE_HALF)[None, :]
    sgn = jnp.concatenate([-jnp.ones((HEAD_DIM // 2,), F32), jnp.ones((HEAD_DIM // 2,), F32)])[None, :]
    pos_b = jnp.broadcast_to(positions.astype(F32)[:, :, None], (b, s, LANES))
    g_rep = jnp.broadcast_to((g_subln[0] * (1.0 - LAMBDA_INIT))[:, None], (HEAD_DIM, LANES))

    mod, lam_tile = _mod_call(c_pad, w_ada[0], b_ada[0][None, :], lambda_q1[0][None, :],
                              lambda_k1[0][None, :], lambda_q2[0][None, :], lambda_k2[0][None, :])
    shift = mod[:b, None, 0:d]
    scale = mod[:b, None, d:2 * d]
    gate = mod[:b, None, 2 * d:3 * d]
    lam = lam_tile[0, :1]

    h = _hnorm_call(x, g_norm[0][None, :], scale, shift, t)
    tables = _rope_table_call(pos_b, invf, sgn, t)

    pm = PROJ_ROWS if s % PROJ_ROWS == 0 else t
    proj = functools.partial(_proj_call, h)
    qa_t = proj(w_qk, 0, tables, pm, rope=True, scale=LOG2E * DA_QK_DIM ** -0.5, layout="heads_t")
    ka = proj(w_qk, 1, tables, pm, rope=True, layout="heads")
    va_t = proj(w_in[0], 2, None, pm, layout="heads_t")
    za = proj(w_in[0], 3, None, pm, act="silu")
    qb_t = proj(w_in[0], 4, None, pm, scale=LOG2E * HEAD_DIM ** -0.5, layout="heads_t")
    kb = proj(w_in[0], 5, None, pm, layout="heads")
    vb_t = proj(w_in[0], 6, None, pm, layout="heads_t")
    zb = proj(w_in[0], 7, None, pm, act="silu")
    ma = proj(w_merge, 0, None, pm, act="sigmoid")
    mb = proj(w_merge, 1, None, pm, act="sigmoid")

    kext, tot_tile = _cum_call(h, wf_pad, bf_pad, t)
    tot = tot_tile[:, :, 0, :N_HEADS].transpose(0, 2, 1).reshape(-1)

    def tiles(a):
        return a.reshape(b, N_HEADS, nq, t, a.shape[-1])

    oa = _diff_call(lam, qa_t, tiles(ka), va_t, g_rep)
    ob = _fox_call(tot, qb_t, tiles(kb), tiles(kext), vb_t)

    return _out_call(oa, za, ob, zb, ma, mb, x, gate,
                     w_branch_a[0].astype(BF16), w_branch_b[0].astype(BF16), w_out[0].astype(BF16),
                     g_final[None, :], t)
```

```python
import functools
import math

import jax
import jax.numpy as jnp
import numpy as np
from jax import lax
from jax.experimental import pallas as pl
from jax.experimental.pallas import tpu as pltpu

F32 = jnp.float32
BF16 = jnp.bfloat16

D_MODEL = 1024
N_HEADS = 8
HEAD_DIM = 128
DA_QK_DIM = 64
ROPE_HALF = DA_QK_DIM // 2
CHUNK = 64
ROPE_THETA = 10000.0
NORM_EPS = 1e-6
SUBLN_EPS = 1e-5
LAMBDA_INIT = 0.8 - 0.6 * math.exp(-0.3 * 0)
LOG2E = math.log2(math.e)

LANES = 128
SUBLANES = 8
ATT_TILE = 512
PROJ_ROWS = 1024
BIAS_TERMS = 3
VMEM_LIMIT = 48 * 1024 * 1024
NEG_BIG = -1e30


def _params(*sem):
    return pltpu.CompilerParams(dimension_semantics=sem, vmem_limit_bytes=VMEM_LIMIT)


def _mod_kernel(c_ref, w_ref, b_ref, lq1_ref, lk1_ref, lq2_ref, lk2_ref, mod_ref, lam_ref):
    c = c_ref[...]
    c_act = c * jax.nn.sigmoid(c)
    mod_ref[...] = jnp.dot(c_act, w_ref[0], precision=lax.Precision.HIGHEST,
                           preferred_element_type=F32) + b_ref[...]
    s1 = jnp.sum(lq1_ref[...] * lk1_ref[...], axis=-1, keepdims=True)
    s2 = jnp.sum(lq2_ref[...] * lk2_ref[...], axis=-1, keepdims=True)
    lam = jnp.exp(s1) - jnp.exp(s2) + LAMBDA_INIT
    lam_ref[...] = jnp.broadcast_to(lam, lam_ref.shape)


def _mod_call(c_pad, w_ada, b_ada, lq1, lk1, lq2, lk2):
    d = D_MODEL
    lam_spec = pl.BlockSpec((1, DA_QK_DIM), lambda j: (0, 0))
    return pl.pallas_call(
        _mod_kernel,
        grid=(3,),
        in_specs=[pl.BlockSpec((SUBLANES, d), lambda j: (0, 0)),
                  pl.BlockSpec((1, d, d), lambda j: (0, 0, j)),
                  pl.BlockSpec((1, d), lambda j: (0, j)),
                  lam_spec, lam_spec, lam_spec, lam_spec],
        out_specs=[pl.BlockSpec((SUBLANES, d), lambda j: (0, j)),
                   pl.BlockSpec((SUBLANES, LANES), lambda j: (0, 0))],
        out_shape=[jax.ShapeDtypeStruct((SUBLANES, 3 * d), F32),
                   jax.ShapeDtypeStruct((SUBLANES, LANES), F32)],
        compiler_params=_params("arbitrary"),
    )(c_pad, w_ada, b_ada, lq1, lk1, lq2, lk2)


def _rope_table_kernel(pos_ref, invf_ref, sgn_ref, cos_ref, sin_ref):
    ang = pos_ref[0] * invf_ref[...]
    cos_ref[0] = jnp.cos(ang)
    sin_ref[0] = jnp.sin(ang) * sgn_ref[...]


def _rope_table_call(pos_b, invf, sgn, bm):
    b, s, _ = pos_b.shape
    row = pl.BlockSpec((1, bm, LANES), lambda bi, i: (bi, i, 0))
    vec = pl.BlockSpec((1, LANES), lambda bi, i: (0, 0))
    shape = jax.ShapeDtypeStruct((b, s, LANES), F32)
    return pl.pallas_call(
        _rope_table_kernel,
        grid=(b, s // bm),
        in_specs=[row, vec, vec],
        out_specs=[row, row],
        out_shape=[shape, shape],
        compiler_params=_params("parallel", "parallel"),
    )(pos_b, invf, sgn)


def _proj_kernel(*refs, rope, scale, act, layout):
    if rope:
        h_ref, w_ref, cos_ref, sin_ref, o_ref = refs
    else:
        h_ref, w_ref, o_ref = refs
    acc = jnp.dot(h_ref[0], w_ref[0].astype(BF16), preferred_element_type=F32)
    if act == "silu":
        acc = acc * jax.nn.sigmoid(acc)
    elif act == "sigmoid":
        acc = jax.nn.sigmoid(acc)
    if layout == "nat":
        o_ref[0] = acc.astype(BF16)
        return
    if rope:
        cos = cos_ref[0]
        sin = sin_ref[0]
    for hh in range(N_HEADS):
        t = acc[:, hh * HEAD_DIM:(hh + 1) * HEAD_DIM]
        if rope:
            t = t * cos + pltpu.roll(t, HEAD_DIM // 2, 1) * sin
        if scale != 1.0:
            t = t * scale
        if layout == "heads":
            o_ref[0, hh] = t.astype(BF16)
        else:
            tile = o_ref.shape[-1]
            for part in range(o_ref.shape[2]):
                o_ref[0, hh, part] = t[part * tile:(part + 1) * tile].T.astype(BF16)


def _proj_call(h, w_all, col_block, tables, bm, *, rope=False, scale=1.0, act=None, layout="nat"):
    b, s, d = h.shape
    n = D_MODEL
    in_specs = [pl.BlockSpec((1, bm, d), lambda bi, i: (bi, i, 0)),
                pl.BlockSpec((1, d, n), lambda bi, i: (0, 0, col_block))]
    args = [h, w_all]
    if rope:
        tab = pl.BlockSpec((1, bm, LANES), lambda bi, i: (bi, i, 0))
        in_specs += [tab, tab]
        args += list(tables)
    if layout == "nat":
        out_spec = pl.BlockSpec((1, bm, n), lambda bi, i: (bi, i, 0))
        out_shape = jax.ShapeDtypeStruct((b, s, n), BF16)
    elif layout == "heads":
        out_spec = pl.BlockSpec((1, N_HEADS, bm, HEAD_DIM), lambda bi, i: (bi, 0, i, 0))
        out_shape = jax.ShapeDtypeStruct((b, N_HEADS, s, HEAD_DIM), BF16)
    else:
        parts = bm // ATT_TILE
        out_spec = pl.BlockSpec((1, N_HEADS, parts, HEAD_DIM, ATT_TILE), lambda bi, i: (bi, 0, i, 0, 0))
        out_shape = jax.ShapeDtypeStruct((b, N_HEADS, s // ATT_TILE, HEAD_DIM, ATT_TILE), BF16)
    return pl.pallas_call(
        functools.partial(_proj_kernel, rope=rope, scale=scale, act=act, layout=layout),
        grid=(b, s // bm),
        in_specs=in_specs,
        out_specs=out_spec,
        out_shape=out_shape,
        compiler_params=_params("parallel", "parallel"),
    )(*args)


def _split3(v):
    hi = v.astype(BF16).astype(F32)
    r1 = v - hi
    mid = r1.astype(BF16).astype(F32)
    return hi, mid, r1 - mid


def _prenorm_cum_kernel(x_ref, g_ref, scale_ref, shift_ref, wf_ref, bf_ref, sel_ref, h_ref, kext_ref, tot_ref):
    bm = x_ref.shape[1]
    x = x_ref[0]
    ms = jnp.mean(x * x, axis=-1, keepdims=True)
    y = x * lax.rsqrt(ms + NORM_EPS) * g_ref[...]
    h = (y * (1.0 + scale_ref[0]) + shift_ref[0]).astype(BF16)
    h_ref[0] = h
    logit = jnp.dot(h, wf_ref[...], preferred_element_type=F32) + bf_ref[...]
    log_f = jnp.minimum(logit, 0.0) - jnp.log(1.0 + jnp.exp(-jnp.abs(logit)))
    r = lax.broadcasted_iota(jnp.int32, (bm, bm), 0)
    c = lax.broadcasted_iota(jnp.int32, (bm, bm), 1)
    tri = jnp.where(c <= r, 1.0, 0.0).astype(BF16)
    terms = jnp.concatenate([part.astype(BF16) for part in _split3(log_f)], axis=1)
    sums = jnp.dot(tri, terms, preferred_element_type=F32)
    cum = sum(sums[:, n * LANES:(n + 1) * LANES] for n in range(BIAS_TERMS))
    bias = cum * (-LOG2E)
    tot_ref[0, 0] = jnp.broadcast_to(-bias[bm - 1:bm, :], tot_ref.shape[2:])
    parts = jnp.concatenate([part.astype(BF16) for part in _split3(bias)], axis=1)
    ext = jnp.dot(parts, sel_ref[...], preferred_element_type=F32)
    for hh in range(N_HEADS):
        kext_ref[0, hh] = ext[:, hh * LANES:(hh + 1) * LANES].astype(BF16)


def _bias_selection():
    sel = np.zeros((BIAS_TERMS, LANES, N_HEADS, LANES), np.float32)
    for n in range(BIAS_TERMS):
        for hh in range(N_HEADS):
            sel[n, hh, hh, n] = 1.0
    return sel.reshape(BIAS_TERMS * LANES, N_HEADS * LANES)


def _prenorm_cum_call(x, g, scale, shift, wf_pad, bf_pad, bm):
    b, s, d = x.shape
    sel = jnp.asarray(_bias_selection(), BF16)
    row = pl.BlockSpec((1, bm, d), lambda bi, i: (bi, i, 0))
    vec = pl.BlockSpec((1, 1, d), lambda bi, i: (bi, 0, 0))
    return pl.pallas_call(
        _prenorm_cum_kernel,
        grid=(b, s // bm),
        in_specs=[row, pl.BlockSpec((1, d), lambda bi, i: (0, 0)), vec, vec,
                  pl.BlockSpec((d, LANES), lambda bi, i: (0, 0)),
                  pl.BlockSpec((1, LANES), lambda bi, i: (0, 0)),
                  pl.BlockSpec(sel.shape, lambda bi, i: (0, 0))],
        out_specs=[row,
                   pl.BlockSpec((1, N_HEADS, bm, LANES), lambda bi, i: (bi, 0, i, 0)),
                   pl.BlockSpec((1, 1, SUBLANES, LANES), lambda bi, i: (bi, i, 0, 0))],
        out_shape=[jax.ShapeDtypeStruct((b, s, d), BF16),
                   jax.ShapeDtypeStruct((b, N_HEADS, s, LANES), BF16),
                   jax.ShapeDtypeStruct((b, s // bm, SUBLANES, LANES), F32)],
        compiler_params=_params("parallel", "parallel"),
    )(x, g, scale, shift, wf_pad, bf_pad, sel)


PIPE_SLOTS = 3
PV_LAG = 2
DENOM_ROWS = 16


def _attend(i, n_maps, scores, next_scores, vt_ref, visible, frame_shift, finish, m, acc, *bufs):
    s_buf, c_buf, p_buf, a_buf = (bufs[k * PIPE_SLOTS:(k + 1) * PIPE_SLOTS] for k in range(4))
    hd = acc.shape[1] - DENOM_ROWS
    ones = jnp.ones((DENOM_ROWS, vt_ref.shape[-1]), BF16)
    m[...] = jnp.full(m.shape, NEG_BIG, F32)
    acc[...] = jnp.zeros(acc.shape, F32)
    for slot in range(PIPE_SLOTS - PV_LAG, PIPE_SLOTS):
        p_buf[slot][...] = jnp.zeros(p_buf[slot].shape, BF16)
        a_buf[slot][...] = jnp.ones(a_buf[slot].shape, F32)

    def put_scores(slot, tiles):
        for n, s in enumerate(tiles):
            s_buf[slot][n] = s
            c_buf[slot][n] = jnp.max(s, axis=0, keepdims=True)

    def softmax(s, s_max, n):
        m_prev = m[n]
        m_new = jnp.maximum(m_prev, s_max)
        alpha = jnp.exp2(m_prev - m_new)
        p = jnp.exp2(s - m_new).astype(BF16)
        return m_new, alpha, p

    def value_product(prev, alpha, j, p):
        vt_ones = jnp.concatenate([vt_ref[0, 0, j], ones], axis=0)
        return alpha * prev + jnp.dot(vt_ones, p, preferred_element_type=F32)

    def step(j, slot, prefetch=True):
        if prefetch:
            put_scores((slot + 2) % PIPE_SLOTS, scores(jnp.minimum(j + 2, i)))
        old = (slot - PV_LAG) % PIPE_SLOTS
        for n in range(n_maps):
            acc[n] = value_product(acc[n], a_buf[old][n], jnp.maximum(j - PV_LAG, 0), p_buf[old][n])
            m_new, alpha, p = softmax(s_buf[slot][n], c_buf[slot][n], n)
            p_buf[slot][n] = p
            a_buf[slot][n] = alpha
            m[n] = m_new + frame_shift(j)

    def last(slot):
        results = []
        for n in range(n_maps):
            out = acc[n]
            for back in range(PV_LAG, 0, -1):
                old = (slot - back) % PIPE_SLOTS
                out = value_product(out, a_buf[old][n], jnp.maximum(i - back, 0), p_buf[old][n])
            s = jnp.where(visible, s_buf[slot][n], -jnp.inf)
            _, alpha, p = softmax(s, jnp.max(s, axis=0, keepdims=True), n)
            out = value_product(out, alpha, i, p)
            results.append((out[:hd], jnp.mean(out[hd:], axis=0, keepdims=True)))
        put_scores(0, next_scores(0))
        put_scores(1, next_scores(1))
        finish(results)

    @pl.when(i == 0)
    def _():
        put_scores(0, scores(0))

    def rotation(jj, carry):
        for slot in range(PIPE_SLOTS):
            step(PIPE_SLOTS * jj + slot, slot)
        return carry

    rotations = i // PIPE_SLOTS

    def two_rotations(u, carry):
        return rotation(2 * u + 1, rotation(2 * u, carry))

    lax.fori_loop(0, rotations // 2, two_rotations, 0)

    @pl.when(rotations % 2 == 1)
    def _():
        rotation(rotations - 1, 0)

    for rem in range(PIPE_SLOTS):
        @pl.when(i % PIPE_SLOTS == rem)
        def _(rem=rem):
            for slot in range(rem):
                step(i - rem + slot, slot, prefetch=slot + 2 <= rem)
            last(rem)


def _attend_scratch(n_maps, hd, tk, tq):
    return ([pltpu.VMEM((n_maps, 1, tq), F32),
             pltpu.VMEM((n_maps, hd + DENOM_ROWS, tq), F32)]
            + [pltpu.VMEM((n_maps, tk, tq), F32)] * PIPE_SLOTS
            + [pltpu.VMEM((n_maps, 1, tq), F32)] * PIPE_SLOTS
            + [pltpu.VMEM((n_maps, tk, tq), BF16)] * PIPE_SLOTS
            + [pltpu.VMEM((n_maps, 1, tq), F32)] * PIPE_SLOTS)


def _diff_kernel(lam_ref, qt_ref, k_ref, vt_ref, g_ref, o_ref, *scratch):
    i = pl.program_id(2)
    nq, t = qt_ref.shape[2], qt_ref.shape[-1]

    def both_maps(block):
        qt = qt_ref[0, 0, block]
        row = lax.broadcasted_iota(jnp.int32, qt.shape, 0)
        map1 = (row // ROPE_HALF) % 2 == 0
        zero = jnp.zeros_like(qt)
        return jnp.concatenate([jnp.where(map1, qt, zero), jnp.where(map1, zero, qt)], axis=1)

    q12 = both_maps(i)

    def scores(j):
        return (jnp.dot(k_ref[0, 0, j], q12, preferred_element_type=F32),)

    def next_scores(j):
        return (jnp.dot(k_ref[0, 0, j], both_maps(jnp.minimum(i + 1, nq - 1)), preferred_element_type=F32),)

    kc = lax.broadcasted_iota(jnp.int32, (t, 2 * t), 0) // CHUNK
    qc = (lax.broadcasted_iota(jnp.int32, (t, 2 * t), 1) % t) // CHUNK

    def finish(results):
        ((out, l),) = results
        o = out / l
        o = o[:, :t] - lam_ref[0] * o[:, t:]
        ms = jnp.mean(o * o, axis=0, keepdims=True)
        g = jnp.concatenate([g_ref[...]] * (t // LANES), axis=1)
        y = o * lax.rsqrt(ms + SUBLN_EPS) * g
        o_ref[0, 0] = y.T.astype(BF16)

    _attend(i, 1, scores, next_scores, vt_ref, kc <= qc, lambda j: 0.0, finish, *scratch)


def _diff_call(lam, qt, k, vt, g_rep):
    b, nh, nq, hd, t = qt.shape
    return pl.pallas_call(
        _diff_kernel,
        grid=(b, nh, nq),
        in_specs=[pl.BlockSpec(memory_space=pltpu.SMEM),
                  pl.BlockSpec((1, 1, nq, hd, t), lambda bi, h, i: (bi, h, 0, 0, 0)),
                  pl.BlockSpec((1, 1, nq, t, hd), lambda bi, h, i: (bi, h, 0, 0, 0)),
                  pl.BlockSpec((1, 1, nq, hd, t), lambda bi, h, i: (bi, h, 0, 0, 0)),
                  pl.BlockSpec((hd, LANES), lambda bi, h, i: (0, 0))],
        out_specs=pl.BlockSpec((1, 1, t, hd), lambda bi, h, i: (bi, h, i, 0)),
        out_shape=jax.ShapeDtypeStruct((b, nh, nq * t, hd), BF16),
        scratch_shapes=_attend_scratch(1, hd, t, 2 * t),
        compiler_params=_params("parallel", "parallel", "arbitrary"),
    )(lam, qt, k, vt, g_rep)


def _fox_kernel(tot_ref, qt_ref, k_ref, kext_ref, vt_ref, o_ref, *scratch):
    bi, h, i = pl.program_id(0), pl.program_id(1), pl.program_id(2)
    nq, t = qt_ref.shape[2], qt_ref.shape[-1]
    tot_base = (bi * N_HEADS + h) * nq

    def with_ones(block):
        qt = qt_ref[0, 0, block]
        row = lax.broadcasted_iota(jnp.int32, qt.shape, 0)
        return jnp.concatenate([qt, jnp.where(row < BIAS_TERMS, 1.0, 0.0).astype(BF16)], axis=0)

    q_cat = with_ones(i)

    def scores_of(q, j):
        k_cat = jnp.concatenate([k_ref[0, 0, j], kext_ref[0, 0, j]], axis=1)
        return (jnp.dot(k_cat, q, preferred_element_type=F32),)

    def scores(j):
        return scores_of(q_cat, j)

    def next_scores(j):
        return scores_of(with_ones(jnp.minimum(i + 1, nq - 1)), j)

    ki = lax.broadcasted_iota(jnp.int32, (t, t), 0)
    qi = lax.broadcasted_iota(jnp.int32, (t, t), 1)

    def finish(results):
        ((o, l),) = results
        o_ref[0, 0] = (o / l).T.astype(BF16)

    _attend(i, 1, scores, next_scores, vt_ref, ki <= qi, lambda j: tot_ref[tot_base + j], finish, *scratch)


def _fox_call(tot, qt, k, kext, vt):
    b, nh, nq, hd, t = qt.shape
    return pl.pallas_call(
        _fox_kernel,
        grid=(b, nh, nq),
        in_specs=[pl.BlockSpec(memory_space=pltpu.SMEM),
                  pl.BlockSpec((1, 1, nq, hd, t), lambda bi, h, i: (bi, h, 0, 0, 0)),
                  pl.BlockSpec((1, 1, nq, t, hd), lambda bi, h, i: (bi, h, 0, 0, 0)),
                  pl.BlockSpec((1, 1, nq, t, LANES), lambda bi, h, i: (bi, h, 0, 0, 0)),
                  pl.BlockSpec((1, 1, nq, hd, t), lambda bi, h, i: (bi, h, 0, 0, 0))],
        out_specs=pl.BlockSpec((1, 1, t, hd), lambda bi, h, i: (bi, h, i, 0)),
        out_shape=jax.ShapeDtypeStruct((b, nh, nq * t, hd), BF16),
        scratch_shapes=_attend_scratch(1, hd, t, t),
        compiler_params=_params("parallel", "parallel", "arbitrary"),
    )(tot, qt, k, kext, vt)


def _out_kernel(oa_ref, za_ref, ob_ref, zb_ref, ma_ref, mb_ref, x_ref, gate_ref,
                wa_ref, wb_ref, wo_ref, gf_ref, o_ref):
    def gated(o_ref_, z_ref_):
        o = jnp.concatenate([o_ref_[0, hh] for hh in range(N_HEADS)], axis=1)
        return (o.astype(F32) * z_ref_[0].astype(F32)).astype(BF16)

    ya = jnp.dot(gated(oa_ref, za_ref), wa_ref[...], preferred_element_type=F32)
    yb = jnp.dot(gated(ob_ref, zb_ref), wb_ref[...], preferred_element_type=F32)
    merged = ma_ref[0].astype(F32) * ya + mb_ref[0].astype(F32) * yb
    out = jnp.dot(merged.astype(BF16), wo_ref[...], preferred_element_type=F32)
    xn = x_ref[0] + gate_ref[0] * out
    ms = jnp.mean(xn * xn, axis=-1, keepdims=True)
    o_ref[0] = xn * lax.rsqrt(ms + NORM_EPS) * gf_ref[...]


def _out_call(oa, za, ob, zb, ma, mb, x, gate, wa, wb, wo, gf, bm):
    b, s, d = x.shape
    heads = pl.BlockSpec((1, N_HEADS, bm, HEAD_DIM), lambda bi, i: (bi, 0, i, 0))
    row = pl.BlockSpec((1, bm, d), lambda bi, i: (bi, i, 0))
    wspec = pl.BlockSpec((d, d), lambda bi, i: (0, 0))
    return pl.pallas_call(
        _out_kernel,
        grid=(b, s // bm),
        in_specs=[heads, row, heads, row, row, row, row,
                  pl.BlockSpec((1, 1, d), lambda bi, i: (bi, 0, 0)),
                  wspec, wspec, wspec,
                  pl.BlockSpec((1, d), lambda bi, i: (0, 0))],
        out_specs=row,
        out_shape=jax.ShapeDtypeStruct((b, s, d), F32),
        compiler_params=_params("parallel", "parallel"),
    )(oa, za, ob, zb, ma, mb, x, gate, wa, wb, wo, gf)


def kernel(x, c, positions, w_ada, b_ada, g_norm, w_in, b_forget, lambda_q1, lambda_k1, lambda_q2,
           lambda_k2, g_subln, w_branch_a, w_branch_b, w_out, g_final):
    b, s, d = x.shape
    assert d == D_MODEL and s % ATT_TILE == 0 and b <= SUBLANES
    assert w_ada.shape[0] == 1, "single-layer block"
    t = ATT_TILE
    nq = s // t
    n_main = 8 * D_MODEL

    c_pad = jnp.zeros((SUBLANES, d), F32).at[:b].set(c)
    w_qk = w_in[0][:, :2 * D_MODEL].reshape(d, 2 * N_HEADS, 2, 2, ROPE_HALF).transpose(0, 1, 3, 2, 4)
    w_qk = w_qk.reshape(1, d, 2 * D_MODEL)
    w_merge = w_in[:, :, n_main + N_HEADS:]
    wf_pad = jnp.zeros((d, LANES), F32).at[:, :N_HEADS].set(w_in[0][:, n_main:n_main + N_HEADS]).astype(BF16)
    bf_pad = jnp.zeros((1, LANES), F32).at[0, :N_HEADS].set(b_forget[0])
    inv_freq = ROPE_THETA ** (-jnp.arange(ROPE_HALF, dtype=F32) / ROPE_HALF)
    invf = jnp.tile(inv_freq, LANES // ROPE_HALF)[None, :]
    sgn = jnp.concatenate([-jnp.ones((HEAD_DIM // 2,), F32), jnp.ones((HEAD_DIM // 2,), F32)])[None, :]
    pos_b = jnp.broadcast_to(positions.astype(F32)[:, :, None], (b, s, LANES))
    g_rep = jnp.broadcast_to((g_subln[0] * (1.0 - LAMBDA_INIT))[:, None], (HEAD_DIM, LANES))

    mod, lam_tile = _mod_call(c_pad, w_ada, b_ada, lambda_q1, lambda_k1, lambda_q2, lambda_k2)
    shift = mod[:b, None, 0:d]
    scale = mod[:b, None, d:2 * d]
    gate = mod[:b, None, 2 * d:3 * d]
    lam = lam_tile[0, :1]

    h, kext, tot_tile = _prenorm_cum_call(x, g_norm, scale, shift, wf_pad, bf_pad, t)
    tables = _rope_table_call(pos_b, invf, sgn, t)

    pm = PROJ_ROWS if s % PROJ_ROWS == 0 else t
    proj = functools.partial(_proj_call, h)
    qa_t = proj(w_qk, 0, tables, pm, rope=True, scale=LOG2E * DA_QK_DIM ** -0.5, layout="heads_t")
    ka = proj(w_qk, 1, tables, pm, rope=True, layout="heads")
    va_t = proj(w_in, 2, None, pm, layout="heads_t")
    za = proj(w_in, 3, None, pm, act="silu")
    qb_t = proj(w_in, 4, None, pm, scale=LOG2E * HEAD_DIM ** -0.5, layout="heads_t")
    kb = proj(w_in, 5, None, pm, layout="heads")
    vb_t = proj(w_in, 6, None, pm, layout="heads_t")
    zb = proj(w_in, 7, None, pm, act="silu")
    ma = proj(w_merge, 0, None, pm, act="sigmoid")
    mb = proj(w_merge, 1, None, pm, act="sigmoid")

    tot = tot_tile[:, :, 0, :N_HEADS].transpose(0, 2, 1).reshape(-1)

    def tiles(a):
        return a.reshape(b, N_HEADS, nq, t, a.shape[-1])

    oa = _diff_call(lam, qa_t, tiles(ka), va_t, g_rep)
    ob = _fox_call(tot, qb_t, tiles(kb), tiles(kext), vb_t)

    return _out_call(oa, za, ob, zb, ma, mb, x, gate,
                     w_branch_a[0].astype(BF16), w_branch_b[0].astype(BF16), w_out[0].astype(BF16),
                     g_final[None, :], t)
```

```python
import functools
import math

import jax
import jax.numpy as jnp
import numpy as np
from jax import lax
from jax.experimental import pallas as pl
from jax.experimental.pallas import tpu as pltpu

F32 = jnp.float32
BF16 = jnp.bfloat16

D_MODEL = 1024
N_HEADS = 8
HEAD_DIM = 128
DA_QK_DIM = 64
ROPE_HALF = DA_QK_DIM // 2
CHUNK = 64
ROPE_THETA = 10000.0
NORM_EPS = 1e-6
SUBLN_EPS = 1e-5
LAMBDA_INIT = 0.8 - 0.6 * math.exp(-0.3 * 0)
LOG2E = math.log2(math.e)

LANES = 128
SUBLANES = 8
ATT_TILE = 512
PROJ_ROWS = 1024
BIAS_TERMS = 3
VMEM_LIMIT = 48 * 1024 * 1024
NEG_BIG = -1e30


def _params(*sem):
    return pltpu.CompilerParams(dimension_semantics=sem, vmem_limit_bytes=VMEM_LIMIT)


def _mod_kernel(c_ref, w_ref, b_ref, lq1_ref, lk1_ref, lq2_ref, lk2_ref, mod_ref, lam_ref):
    c = c_ref[...]
    c_act = c * jax.nn.sigmoid(c)
    mod_ref[...] = jnp.dot(c_act, w_ref[0], precision=lax.Precision.HIGHEST,
                           preferred_element_type=F32) + b_ref[...]
    s1 = jnp.sum(lq1_ref[...] * lk1_ref[...], axis=-1, keepdims=True)
    s2 = jnp.sum(lq2_ref[...] * lk2_ref[...], axis=-1, keepdims=True)
    lam = jnp.exp(s1) - jnp.exp(s2) + LAMBDA_INIT
    lam_ref[...] = jnp.broadcast_to(lam, lam_ref.shape)


def _mod_call(c_pad, w_ada, b_ada, lq1, lk1, lq2, lk2):
    d = D_MODEL
    lam_spec = pl.BlockSpec((1, DA_QK_DIM), lambda j: (0, 0))
    return pl.pallas_call(
        _mod_kernel,
        grid=(3,),
        in_specs=[pl.BlockSpec((SUBLANES, d), lambda j: (0, 0)),
                  pl.BlockSpec((1, d, d), lambda j: (0, 0, j)),
                  pl.BlockSpec((1, d), lambda j: (0, j)),
                  lam_spec, lam_spec, lam_spec, lam_spec],
        out_specs=[pl.BlockSpec((SUBLANES, d), lambda j: (0, j)),
                   pl.BlockSpec((SUBLANES, LANES), lambda j: (0, 0))],
        out_shape=[jax.ShapeDtypeStruct((SUBLANES, 3 * d), F32),
                   jax.ShapeDtypeStruct((SUBLANES, LANES), F32)],
        compiler_params=_params("arbitrary"),
    )(c_pad, w_ada, b_ada, lq1, lk1, lq2, lk2)


def _rope_table_kernel(pos_ref, invf_ref, sgn_ref, cos_ref, sin_ref):
    ang = pos_ref[0] * invf_ref[...]
    cos_ref[0] = jnp.cos(ang)
    sin_ref[0] = jnp.sin(ang) * sgn_ref[...]


def _rope_table_call(pos_b, invf, sgn, bm):
    b, s, _ = pos_b.shape
    row = pl.BlockSpec((1, bm, LANES), lambda bi, i: (bi, i, 0))
    vec = pl.BlockSpec((1, LANES), lambda bi, i: (0, 0))
    shape = jax.ShapeDtypeStruct((b, s, LANES), F32)
    return pl.pallas_call(
        _rope_table_kernel,
        grid=(b, s // bm),
        in_specs=[row, vec, vec],
        out_specs=[row, row],
        out_shape=[shape, shape],
        compiler_params=_params("parallel", "parallel"),
    )(pos_b, invf, sgn)


def _proj_kernel(*refs, rope, scale, act, layout):
    if rope:
        h_ref, w_ref, cos_ref, sin_ref, o_ref = refs
    else:
        h_ref, w_ref, o_ref = refs
    acc = lax.dot_general(h_ref[0], w_ref[0].astype(BF16), (((1,), (1,)), ((), ())),
                          preferred_element_type=F32)
    if act == "silu":
        acc = acc * jax.nn.sigmoid(acc)
    elif act == "sigmoid":
        acc = jax.nn.sigmoid(acc)
    if layout == "nat":
        o_ref[0] = acc.astype(BF16)
        return
    if rope:
        cos = cos_ref[0]
        sin = sin_ref[0]
    for hh in range(N_HEADS):
        t = acc[:, hh * HEAD_DIM:(hh + 1) * HEAD_DIM]
        if rope:
            t = t * cos + pltpu.roll(t, HEAD_DIM // 2, 1) * sin
        if scale != 1.0:
            t = t * scale
        if layout == "heads":
            o_ref[0, hh] = t.astype(BF16)
        else:
            tile = o_ref.shape[-1]
            for part in range(o_ref.shape[2]):
                o_ref[0, hh, part] = t[part * tile:(part + 1) * tile].T.astype(BF16)


def _proj_call(h, w_all, col_block, tables, bm, *, rope=False, scale=1.0, act=None, layout="nat"):
    b, s, d = h.shape
    n = D_MODEL
    in_specs = [pl.BlockSpec((1, bm, d), lambda bi, i: (bi, i, 0)),
                pl.BlockSpec((1, n, d), lambda bi, i: (0, col_block, 0))]
    args = [h, w_all]
    if rope:
        tab = pl.BlockSpec((1, bm, LANES), lambda bi, i: (bi, i, 0))
        in_specs += [tab, tab]
        args += list(tables)
    if layout == "nat":
        out_spec = pl.BlockSpec((1, bm, n), lambda bi, i: (bi, i, 0))
        out_shape = jax.ShapeDtypeStruct((b, s, n), BF16)
    elif layout == "heads":
        out_spec = pl.BlockSpec((1, N_HEADS, bm, HEAD_DIM), lambda bi, i: (bi, 0, i, 0))
        out_shape = jax.ShapeDtypeStruct((b, N_HEADS, s, HEAD_DIM), BF16)
    else:
        parts = bm // ATT_TILE
        out_spec = pl.BlockSpec((1, N_HEADS, parts, HEAD_DIM, ATT_TILE), lambda bi, i: (bi, 0, i, 0, 0))
        out_shape = jax.ShapeDtypeStruct((b, N_HEADS, s // ATT_TILE, HEAD_DIM, ATT_TILE), BF16)
    return pl.pallas_call(
        functools.partial(_proj_kernel, rope=rope, scale=scale, act=act, layout=layout),
        grid=(b, s // bm),
        in_specs=in_specs,
        out_specs=out_spec,
        out_shape=out_shape,
        compiler_params=_params("parallel", "parallel"),
    )(*args)


def _split3(v):
    hi = v.astype(BF16).astype(F32)
    r1 = v - hi
    mid = r1.astype(BF16).astype(F32)
    return hi, mid, r1 - mid


def _prenorm_cum_kernel(x_ref, g_ref, scale_ref, shift_ref, wf_ref, bf_ref, sel_ref, h_ref, kext_ref, tot_ref):
    bm = x_ref.shape[1]
    x = x_ref[0]
    ms = jnp.mean(x * x, axis=-1, keepdims=True)
    y = x * lax.rsqrt(ms + NORM_EPS) * g_ref[...]
    h = (y * (1.0 + scale_ref[0]) + shift_ref[0]).astype(BF16)
    h_ref[0] = h
    logit = jnp.dot(h, wf_ref[...], preferred_element_type=F32) + bf_ref[...]
    log_f = jnp.minimum(logit, 0.0) - jnp.log(1.0 + jnp.exp(-jnp.abs(logit)))
    r = lax.broadcasted_iota(jnp.int32, (bm, bm), 0)
    c = lax.broadcasted_iota(jnp.int32, (bm, bm), 1)
    tri = jnp.where(c <= r, 1.0, 0.0).astype(BF16)
    terms = jnp.concatenate([part.astype(BF16) for part in _split3(log_f)], axis=1)
    sums = jnp.dot(tri, terms, preferred_element_type=F32)
    cum = sum(sums[:, n * LANES:(n + 1) * LANES] for n in range(BIAS_TERMS))
    bias = cum * (-LOG2E)
    tot_ref[0, 0] = jnp.broadcast_to(-bias[bm - 1:bm, :], tot_ref.shape[2:])
    parts = jnp.concatenate([part.astype(BF16) for part in _split3(bias)], axis=1)
    ext = jnp.dot(parts, sel_ref[...], preferred_element_type=F32)
    for hh in range(N_HEADS):
        kext_ref[0, hh] = ext[:, hh * LANES:(hh + 1) * LANES].astype(BF16)


def _bias_selection():
    sel = np.zeros((BIAS_TERMS, LANES, N_HEADS, LANES), np.float32)
    for n in range(BIAS_TERMS):
        for hh in range(N_HEADS):
            sel[n, hh, hh, n] = 1.0
    return sel.reshape(BIAS_TERMS * LANES, N_HEADS * LANES)


def _prenorm_cum_call(x, g, scale, shift, wf_pad, bf_pad, bm):
    b, s, d = x.shape
    sel = jnp.asarray(_bias_selection(), BF16)
    row = pl.BlockSpec((1, bm, d), lambda bi, i: (bi, i, 0))
    vec = pl.BlockSpec((1, 1, d), lambda bi, i: (bi, 0, 0))
    return pl.pallas_call(
        _prenorm_cum_kernel,
        grid=(b, s // bm),
        in_specs=[row, pl.BlockSpec((1, d), lambda bi, i: (0, 0)), vec, vec,
                  pl.BlockSpec((d, LANES), lambda bi, i: (0, 0)),
                  pl.BlockSpec((1, LANES), lambda bi, i: (0, 0)),
                  pl.BlockSpec(sel.shape, lambda bi, i: (0, 0))],
        out_specs=[row,
                   pl.BlockSpec((1, N_HEADS, bm, LANES), lambda bi, i: (bi, 0, i, 0)),
                   pl.BlockSpec((1, 1, SUBLANES, LANES), lambda bi, i: (bi, i, 0, 0))],
        out_shape=[jax.ShapeDtypeStruct((b, s, d), BF16),
                   jax.ShapeDtypeStruct((b, N_HEADS, s, LANES), BF16),
                   jax.ShapeDtypeStruct((b, s // bm, SUBLANES, LANES), F32)],
        compiler_params=_params("parallel", "parallel"),
    )(x, g, scale, shift, wf_pad, bf_pad, sel)


PIPE_SLOTS = 3
PV_LAG = 2
DENOM_ROWS = 16


def _attend(i, n_maps, scores, next_scores, vt_ref, visible, frame_shift, finish, m, acc, *bufs):
    s_buf, c_buf, p_buf, a_buf = (bufs[k * PIPE_SLOTS:(k + 1) * PIPE_SLOTS] for k in range(4))
    hd = acc.shape[1] - DENOM_ROWS
    ones = jnp.ones((DENOM_ROWS, vt_ref.shape[-1]), BF16)
    m[...] = jnp.full(m.shape, NEG_BIG, F32)
    acc[...] = jnp.zeros(acc.shape, F32)
    for slot in range(PIPE_SLOTS - PV_LAG, PIPE_SLOTS):
        p_buf[slot][...] = jnp.zeros(p_buf[slot].shape, BF16)
        a_buf[slot][...] = jnp.ones(a_buf[slot].shape, F32)

    def put_scores(slot, tiles):
        for n, s in enumerate(tiles):
            s_buf[slot][n] = s
            c_buf[slot][n] = jnp.max(s, axis=0, keepdims=True)

    def softmax(s, s_max, n):
        m_prev = m[n]
        m_new = jnp.maximum(m_prev, s_max)
        alpha = jnp.exp2(m_prev - m_new)
        p = jnp.exp2(s - m_new).astype(BF16)
        return m_new, alpha, p

    def value_product(prev, alpha, j, p):
        vt_ones = jnp.concatenate([vt_ref[0, 0, j], ones], axis=0)
        return alpha * prev + jnp.dot(vt_ones, p, preferred_element_type=F32)

    def step(j, slot, prefetch=True):
        if prefetch:
            put_scores((slot + 2) % PIPE_SLOTS, scores(jnp.minimum(j + 2, i)))
        old = (slot - PV_LAG) % PIPE_SLOTS
        for n in range(n_maps):
            acc[n] = value_product(acc[n], a_buf[old][n], jnp.maximum(j - PV_LAG, 0), p_buf[old][n])
            m_new, alpha, p = softmax(s_buf[slot][n], c_buf[slot][n], n)
            p_buf[slot][n] = p
            a_buf[slot][n] = alpha
            m[n] = m_new + frame_shift(j)

    def last(slot):
        results = []
        for n in range(n_maps):
            out = acc[n]
            for back in range(PV_LAG, 0, -1):
                old = (slot - back) % PIPE_SLOTS
                out = value_product(out, a_buf[old][n], jnp.maximum(i - back, 0), p_buf[old][n])
            s = jnp.where(visible, s_buf[slot][n], -jnp.inf)
            _, alpha, p = softmax(s, jnp.max(s, axis=0, keepdims=True), n)
            out = value_product(out, alpha, i, p)
            results.append((out[:hd], jnp.mean(out[hd:], axis=0, keepdims=True)))
        put_scores(0, next_scores(0))
        put_scores(1, next_scores(1))
        finish(results)

    @pl.when(i == 0)
    def _():
        put_scores(0, scores(0))

    def rotation(jj, carry):
        for slot in range(PIPE_SLOTS):
            step(PIPE_SLOTS * jj + slot, slot)
        return carry

    rotations = i // PIPE_SLOTS

    def two_rotations(u, carry):
        return rotation(2 * u + 1, rotation(2 * u, carry))

    lax.fori_loop(0, rotations // 2, two_rotations, 0)

    @pl.when(rotations % 2 == 1)
    def _():
        rotation(rotations - 1, 0)

    for rem in range(PIPE_SLOTS):
        @pl.when(i % PIPE_SLOTS == rem)
        def _(rem=rem):
            for slot in range(rem):
                step(i - rem + slot, slot, prefetch=slot + 2 <= rem)
            last(rem)


def _attend_scratch(n_maps, hd, tk, tq):
    return ([pltpu.VMEM((n_maps, 1, tq), F32),
             pltpu.VMEM((n_maps, hd + DENOM_ROWS, tq), F32)]
            + [pltpu.VMEM((n_maps, tk, tq), F32)] * PIPE_SLOTS
            + [pltpu.VMEM((n_maps, 1, tq), F32)] * PIPE_SLOTS
            + [pltpu.VMEM((n_maps, tk, tq), BF16)] * PIPE_SLOTS
            + [pltpu.VMEM((n_maps, 1, tq), F32)] * PIPE_SLOTS)


def _diff_kernel(lam_ref, qt_ref, k_ref, vt_ref, g_ref, o_ref, *scratch):
    i = pl.program_id(2)
    nq, t = qt_ref.shape[2], qt_ref.shape[-1]

    def both_maps(block):
        qt = qt_ref[0, 0, block]
        row = lax.broadcasted_iota(jnp.int32, qt.shape, 0)
        map1 = (row // ROPE_HALF) % 2 == 0
        zero = jnp.zeros_like(qt)
        return jnp.concatenate([jnp.where(map1, qt, zero), jnp.where(map1, zero, qt)], axis=1)

    q12 = both_maps(i)

    def scores(j):
        return (jnp.dot(k_ref[0, 0, j], q12, preferred_element_type=F32),)

    def next_scores(j):
        return (jnp.dot(k_ref[0, 0, j], both_maps(jnp.minimum(i + 1, nq - 1)), preferred_element_type=F32),)

    kc = lax.broadcasted_iota(jnp.int32, (t, 2 * t), 0) // CHUNK
    qc = (lax.broadcasted_iota(jnp.int32, (t, 2 * t), 1) % t) // CHUNK

    def finish(results):
        ((out, l),) = results
        o = out / l
        o = o[:, :t] - lam_ref[0] * o[:, t:]
        ms = jnp.mean(o * o, axis=0, keepdims=True)
        g = jnp.concatenate([g_ref[...]] * (t // LANES), axis=1)
        y = o * lax.rsqrt(ms + SUBLN_EPS) * g
        o_ref[0, 0] = y.T.astype(BF16)

    _attend(i, 1, scores, next_scores, vt_ref, kc <= qc, lambda j: 0.0, finish, *scratch)


def _diff_call(lam, qt, k, vt, g_rep):
    b, nh, nq, hd, t = qt.shape
    return pl.pallas_call(
        _diff_kernel,
        grid=(b, nh, nq),
        in_specs=[pl.BlockSpec(memory_space=pltpu.SMEM),
                  pl.BlockSpec((1, 1, nq, hd, t), lambda bi, h, i: (bi, h, 0, 0, 0)),
                  pl.BlockSpec((1, 1, nq, t, hd), lambda bi, h, i: (bi, h, 0, 0, 0)),
                  pl.BlockSpec((1, 1, nq, hd, t), lambda bi, h, i: (bi, h, 0, 0, 0)),
                  pl.BlockSpec((hd, LANES), lambda bi, h, i: (0, 0))],
        out_specs=pl.BlockSpec((1, 1, t, hd), lambda bi, h, i: (bi, h, i, 0)),
        out_shape=jax.ShapeDtypeStruct((b, nh, nq * t, hd), BF16),
        scratch_shapes=_attend_scratch(1, hd, t, 2 * t),
        compiler_params=_params("parallel", "parallel", "arbitrary"),
    )(lam, qt, k, vt, g_rep)


def _fox_kernel(tot_ref, qt_ref, k_ref, kext_ref, vt_ref, o_ref, *scratch):
    bi, h, i = pl.program_id(0), pl.program_id(1), pl.program_id(2)
    nq, t = qt_ref.shape[2], qt_ref.shape[-1]
    tot_base = (bi * N_HEADS + h) * nq

    def with_ones(block):
        qt = qt_ref[0, 0, block]
        row = lax.broadcasted_iota(jnp.int32, qt.shape, 0)
        return jnp.concatenate([qt, jnp.where(row < BIAS_TERMS, 1.0, 0.0).astype(BF16)], axis=0)

    q_cat = with_ones(i)

    def scores_of(q, j):
        k_cat = jnp.concatenate([k_ref[0, 0, j], kext_ref[0, 0, j]], axis=1)
        return (jnp.dot(k_cat, q, preferred_element_type=F32),)

    def scores(j):
        return scores_of(q_cat, j)

    def next_scores(j):
        return scores_of(with_ones(jnp.minimum(i + 1, nq - 1)), j)

    ki = lax.broadcasted_iota(jnp.int32, (t, t), 0)
    qi = lax.broadcasted_iota(jnp.int32, (t, t), 1)

    def finish(results):
        ((o, l),) = results
        o_ref[0, 0] = (o / l).T.astype(BF16)

    _attend(i, 1, scores, next_scores, vt_ref, ki <= qi, lambda j: tot_ref[tot_base + j], finish, *scratch)


def _fox_call(tot, qt, k, kext, vt):
    b, nh, nq, hd, t = qt.shape
    return pl.pallas_call(
        _fox_kernel,
        grid=(b, nh, nq),
        in_specs=[pl.BlockSpec(memory_space=pltpu.SMEM),
                  pl.BlockSpec((1, 1, nq, hd, t), lambda bi, h, i: (bi, h, 0, 0, 0)),
                  pl.BlockSpec((1, 1, nq, t, hd), lambda bi, h, i: (bi, h, 0, 0, 0)),
                  pl.BlockSpec((1, 1, nq, t, LANES), lambda bi, h, i: (bi, h, 0, 0, 0)),
                  pl.BlockSpec((1, 1, nq, hd, t), lambda bi, h, i: (bi, h, 0, 0, 0))],
        out_specs=pl.BlockSpec((1, 1, t, hd), lambda bi, h, i: (bi, h, i, 0)),
        out_shape=jax.ShapeDtypeStruct((b, nh, nq * t, hd), BF16),
        scratch_shapes=_attend_scratch(1, hd, t, t),
        compiler_params=_params("parallel", "parallel", "arbitrary"),
    )(tot, qt, k, kext, vt)


def _out_kernel(oa_ref, za_ref, ob_ref, zb_ref, ma_ref, mb_ref, x_ref, gate_ref,
                wa_ref, wb_ref, wo_ref, gf_ref, o_ref):
    def gated(o_ref_, z_ref_):
        o = jnp.concatenate([o_ref_[0, hh] for hh in range(N_HEADS)], axis=1)
        return (o.astype(F32) * z_ref_[0].astype(F32)).astype(BF16)

    ya = jnp.dot(gated(oa_ref, za_ref), wa_ref[...], preferred_element_type=F32)
    yb = jnp.dot(gated(ob_ref, zb_ref), wb_ref[...], preferred_element_type=F32)
    merged = ma_ref[0].astype(F32) * ya + mb_ref[0].astype(F32) * yb
    out = jnp.dot(merged.astype(BF16), wo_ref[...], preferred_element_type=F32)
    xn = x_ref[0] + gate_ref[0] * out
    ms = jnp.mean(xn * xn, axis=-1, keepdims=True)
    o_ref[0] = xn * lax.rsqrt(ms + NORM_EPS) * gf_ref[...]


def _out_call(oa, za, ob, zb, ma, mb, x, gate, wa, wb, wo, gf, bm):
    b, s, d = x.shape
    heads = pl.BlockSpec((1, N_HEADS, bm, HEAD_DIM), lambda bi, i: (bi, 0, i, 0))
    row = pl.BlockSpec((1, bm, d), lambda bi, i: (bi, i, 0))
    wspec = pl.BlockSpec((d, d), lambda bi, i: (0, 0))
    return pl.pallas_call(
        _out_kernel,
        grid=(b, s // bm),
        in_specs=[heads, row, heads, row, row, row, row,
                  pl.BlockSpec((1, 1, d), lambda bi, i: (bi, 0, 0)),
                  wspec, wspec, wspec,
                  pl.BlockSpec((1, d), lambda bi, i: (0, 0))],
        out_specs=row,
        out_shape=jax.ShapeDtypeStruct((b, s, d), F32),
        compiler_params=_params("parallel", "parallel"),
    )(oa, za, ob, zb, ma, mb, x, gate, wa, wb, wo, gf)


def kernel(x, c, positions, w_ada, b_ada, g_norm, w_in, b_forget, lambda_q1, lambda_k1, lambda_q2,
           lambda_k2, g_subln, w_branch_a, w_branch_b, w_out, g_final):
    b, s, d = x.shape
    assert d == D_MODEL and s % ATT_TILE == 0 and b <= SUBLANES
    assert w_ada.shape[0] == 1, "single-layer block"
    t = ATT_TILE
    nq = s // t
    n_main = 8 * D_MODEL

    c_pad = jnp.zeros((SUBLANES, d), F32).at[:b].set(c)
    w_t = jnp.swapaxes(w_in, 1, 2)
    w_qk = w_t[:, :2 * D_MODEL].reshape(2 * N_HEADS, 2, 2, ROPE_HALF, d).transpose(0, 2, 1, 3, 4)
    w_qk = w_qk.reshape(1, 2 * D_MODEL, d)
    w_merge = w_t[:, n_main + N_HEADS:]
    wf_pad = jnp.zeros((d, LANES), F32).at[:, :N_HEADS].set(w_in[0][:, n_main:n_main + N_HEADS]).astype(BF16)
    bf_pad = jnp.zeros((1, LANES), F32).at[0, :N_HEADS].set(b_forget[0])
    inv_freq = ROPE_THETA ** (-jnp.arange(ROPE_HALF, dtype=F32) / ROPE_HALF)
    invf = jnp.tile(inv_freq, LANES // ROPE_HALF)[None, :]
    sgn = jnp.concatenate([-jnp.ones((HEAD_DIM // 2,), F32), jnp.ones((HEAD_DIM // 2,), F32)])[None, :]
    pos_b = jnp.broadcast_to(positions.astype(F32)[:, :, None], (b, s, LANES))
    g_rep = jnp.broadcast_to((g_subln[0] * (1.0 - LAMBDA_INIT))[:, None], (HEAD_DIM, LANES))

    mod, lam_tile = _mod_call(c_pad, w_ada, b_ada, lambda_q1, lambda_k1, lambda_q2, lambda_k2)
    shift = mod[:b, None, 0:d]
    scale = mod[:b, None, d:2 * d]
    gate = mod[:b, None, 2 * d:3 * d]
    lam = lam_tile[0, :1]

    h, kext, tot_tile = _prenorm_cum_call(x, g_norm, scale, shift, wf_pad, bf_pad, t)
    tables = _rope_table_call(pos_b, invf, sgn, t)

    pm = PROJ_ROWS if s % PROJ_ROWS == 0 else t
    proj = functools.partial(_proj_call, h)
    qa_t = proj(w_qk, 0, tables, pm, rope=True, scale=LOG2E * DA_QK_DIM ** -0.5, layout="heads_t")
    ka = proj(w_qk, 1, tables, pm, rope=True, layout="heads")
    va_t = proj(w_t, 2, None, pm, layout="heads_t")
    za = proj(w_t, 3, None, pm, act="silu")
    qb_t = proj(w_t, 4, None, pm, scale=LOG2E * HEAD_DIM ** -0.5, layout="heads_t")
    kb = proj(w_t, 5, None, pm, layout="heads")
    vb_t = proj(w_t, 6, None, pm, layout="heads_t")
    zb = proj(w_t, 7, None, pm, act="silu")
    ma = proj(w_merge, 0, None, pm, act="sigmoid")
    mb = proj(w_merge, 1, None, pm, act="sigmoid")

    tot = tot_tile[:, :, 0, :N_HEADS].transpose(0, 2, 1).reshape(-1)

    def tiles(a):
        return a.reshape(b, N_HEADS, nq, t, a.shape[-1])

    oa = _diff_call(lam, qa_t, tiles(ka), va_t, g_rep)
    ob = _fox_call(tot, qb_t, tiles(kb), tiles(kext), vb_t)

    return _out_call(oa, za, ob, zb, ma, mb, x, gate,
                     w_branch_a[0].astype(BF16), w_branch_b[0].astype(BF16), w_out[0].astype(BF16),
                     g_final[None, :], t)
```

```python
import functools
import math

import jax
import jax.numpy as jnp
import numpy as np
from jax import lax
from jax.experimental import pallas as pl
from jax.experimental.pallas import tpu as pltpu

F32 = jnp.float32
BF16 = jnp.bfloat16

D_MODEL = 1024
N_HEADS = 8
HEAD_DIM = 128
DA_QK_DIM = 64
ROPE_HALF = DA_QK_DIM // 2
CHUNK = 64
ROPE_THETA = 10000.0
NORM_EPS = 1e-6
SUBLN_EPS = 1e-5
LAMBDA_INIT = 0.8 - 0.6 * math.exp(-0.3 * 0)
LOG2E = math.log2(math.e)

LANES = 128
SUBLANES = 8
ATT_TILE = 512
PROJ_ROWS = 1024
BIAS_TERMS = 3
VMEM_LIMIT = 48 * 1024 * 1024
NEG_BIG = -1e30


def _params(*sem):
    return pltpu.CompilerParams(dimension_semantics=sem, vmem_limit_bytes=VMEM_LIMIT)


def _mod_kernel(c_ref, w_ref, b_ref, lq1_ref, lk1_ref, lq2_ref, lk2_ref, mod_ref, lam_ref):
    c = c_ref[...]
    c_act = c * jax.nn.sigmoid(c)
    mod_ref[...] = jnp.dot(c_act, w_ref[0], precision=lax.Precision.HIGHEST,
                           preferred_element_type=F32) + b_ref[...]
    s1 = jnp.sum(lq1_ref[...] * lk1_ref[...], axis=-1, keepdims=True)
    s2 = jnp.sum(lq2_ref[...] * lk2_ref[...], axis=-1, keepdims=True)
    lam = jnp.exp(s1) - jnp.exp(s2) + LAMBDA_INIT
    lam_ref[...] = jnp.broadcast_to(lam, lam_ref.shape)


def _mod_call(c_pad, w_ada, b_ada, lq1, lk1, lq2, lk2):
    d = D_MODEL
    lam_spec = pl.BlockSpec((1, DA_QK_DIM), lambda j: (0, 0))
    return pl.pallas_call(
        _mod_kernel,
        grid=(3,),
        in_specs=[pl.BlockSpec((SUBLANES, d), lambda j: (0, 0)),
                  pl.BlockSpec((1, d, d), lambda j: (0, 0, j)),
                  pl.BlockSpec((1, d), lambda j: (0, j)),
                  lam_spec, lam_spec, lam_spec, lam_spec],
        out_specs=[pl.BlockSpec((SUBLANES, d), lambda j: (0, j)),
                   pl.BlockSpec((SUBLANES, LANES), lambda j: (0, 0))],
        out_shape=[jax.ShapeDtypeStruct((SUBLANES, 3 * d), F32),
                   jax.ShapeDtypeStruct((SUBLANES, LANES), F32)],
        compiler_params=_params("arbitrary"),
    )(c_pad, w_ada, b_ada, lq1, lk1, lq2, lk2)


def _rope_table_kernel(pos_ref, invf_ref, sgn_ref, cos_ref, sin_ref):
    ang = pos_ref[0] * invf_ref[...]
    cos_ref[0] = jnp.cos(ang)
    sin_ref[0] = jnp.sin(ang) * sgn_ref[...]


def _rope_table_call(pos_b, invf, sgn, bm):
    b, s, _ = pos_b.shape
    row = pl.BlockSpec((1, bm, LANES), lambda bi, i: (bi, i, 0))
    vec = pl.BlockSpec((1, LANES), lambda bi, i: (0, 0))
    shape = jax.ShapeDtypeStruct((b, s, LANES), F32)
    return pl.pallas_call(
        _rope_table_kernel,
        grid=(b, s // bm),
        in_specs=[row, vec, vec],
        out_specs=[row, row],
        out_shape=[shape, shape],
        compiler_params=_params("parallel", "parallel"),
    )(pos_b, invf, sgn)


def _proj_kernel(*refs, rope, scale, act, layout):
    if rope:
        h_ref, w_ref, cos_ref, sin_ref, o_ref = refs
    else:
        h_ref, w_ref, o_ref = refs
    acc = lax.dot_general(h_ref[0], w_ref[0].astype(BF16), (((1,), (1,)), ((), ())),
                          preferred_element_type=F32)
    if act == "silu":
        acc = acc * jax.nn.sigmoid(acc)
    elif act == "sigmoid":
        acc = jax.nn.sigmoid(acc)
    if layout == "nat":
        o_ref[0] = acc.astype(BF16)
        return
    if rope:
        cos = cos_ref[0]
        sin = sin_ref[0]
    for hh in range(N_HEADS):
        t = acc[:, hh * HEAD_DIM:(hh + 1) * HEAD_DIM]
        if rope:
            t = t * cos + pltpu.roll(t, HEAD_DIM // 2, 1) * sin
        if scale != 1.0:
            t = t * scale
        if layout == "heads":
            o_ref[0, hh] = t.astype(BF16)
        else:
            tile = o_ref.shape[-1]
            for part in range(o_ref.shape[2]):
                o_ref[0, hh, part] = t[part * tile:(part + 1) * tile].T.astype(BF16)


def _proj_call(h, w_all, col_block, tables, bm, *, rope=False, scale=1.0, act=None, layout="nat"):
    b, s, d = h.shape
    n = D_MODEL
    in_specs = [pl.BlockSpec((1, bm, d), lambda bi, i: (bi, i, 0)),
                pl.BlockSpec((1, n, d), lambda bi, i: (0, col_block, 0))]
    args = [h, w_all]
    if rope:
        tab = pl.BlockSpec((1, bm, LANES), lambda bi, i: (bi, i, 0))
        in_specs += [tab, tab]
        args += list(tables)
    if layout == "nat":
        out_spec = pl.BlockSpec((1, bm, n), lambda bi, i: (bi, i, 0))
        out_shape = jax.ShapeDtypeStruct((b, s, n), BF16)
    elif layout == "heads":
        out_spec = pl.BlockSpec((1, N_HEADS, bm, HEAD_DIM), lambda bi, i: (bi, 0, i, 0))
        out_shape = jax.ShapeDtypeStruct((b, N_HEADS, s, HEAD_DIM), BF16)
    else:
        parts = bm // ATT_TILE
        out_spec = pl.BlockSpec((1, N_HEADS, parts, HEAD_DIM, ATT_TILE), lambda bi, i: (bi, 0, i, 0, 0))
        out_shape = jax.ShapeDtypeStruct((b, N_HEADS, s // ATT_TILE, HEAD_DIM, ATT_TILE), BF16)
    return pl.pallas_call(
        functools.partial(_proj_kernel, rope=rope, scale=scale, act=act, layout=layout),
        grid=(b, s // bm),
        in_specs=in_specs,
        out_specs=out_spec,
        out_shape=out_shape,
        compiler_params=_params("parallel", "parallel"),
    )(*args)


def _split3(v):
    hi = v.astype(BF16).astype(F32)
    r1 = v - hi
    mid = r1.astype(BF16).astype(F32)
    return hi, mid, r1 - mid


def _prenorm_cum_kernel(x_ref, g_ref, scale_ref, shift_ref, wf_ref, bf_ref, sel_ref, h_ref, kext_ref, tot_ref):
    bm = x_ref.shape[1]
    x = x_ref[0]
    ms = jnp.mean(x * x, axis=-1, keepdims=True)
    y = x * lax.rsqrt(ms + NORM_EPS) * g_ref[...]
    h = (y * (1.0 + scale_ref[0]) + shift_ref[0]).astype(BF16)
    h_ref[0] = h
    logit = lax.dot_general(h, wf_ref[...], (((1,), (1,)), ((), ())),
                            preferred_element_type=F32) + bf_ref[...]
    log_f = jnp.minimum(logit, 0.0) - jnp.log(1.0 + jnp.exp(-jnp.abs(logit)))
    r = lax.broadcasted_iota(jnp.int32, (bm, bm), 0)
    c = lax.broadcasted_iota(jnp.int32, (bm, bm), 1)
    tri = jnp.where(c <= r, 1.0, 0.0).astype(BF16)
    terms = jnp.concatenate([part.astype(BF16) for part in _split3(log_f)], axis=1)
    sums = jnp.dot(tri, terms, preferred_element_type=F32)
    cum = sum(sums[:, n * LANES:(n + 1) * LANES] for n in range(BIAS_TERMS))
    bias = cum * (-LOG2E)
    tot_ref[0, 0] = jnp.broadcast_to(-bias[bm - 1:bm, :], tot_ref.shape[2:])
    parts = jnp.concatenate([part.astype(BF16) for part in _split3(bias)], axis=1)
    ext = jnp.dot(parts, sel_ref[...], preferred_element_type=F32)
    for hh in range(N_HEADS):
        kext_ref[0, hh] = ext[:, hh * LANES:(hh + 1) * LANES].astype(BF16)


def _bias_selection():
    sel = np.zeros((BIAS_TERMS, LANES, N_HEADS, LANES), np.float32)
    for n in range(BIAS_TERMS):
        for hh in range(N_HEADS):
            sel[n, hh, hh, n] = 1.0
    return sel.reshape(BIAS_TERMS * LANES, N_HEADS * LANES)


def _prenorm_cum_call(x, g, scale, shift, wf_pad, bf_pad, bm):
    b, s, d = x.shape
    sel = jnp.asarray(_bias_selection(), BF16)
    row = pl.BlockSpec((1, bm, d), lambda bi, i: (bi, i, 0))
    vec = pl.BlockSpec((1, 1, d), lambda bi, i: (bi, 0, 0))
    return pl.pallas_call(
        _prenorm_cum_kernel,
        grid=(b, s // bm),
        in_specs=[row, pl.BlockSpec((1, d), lambda bi, i: (0, 0)), vec, vec,
                  pl.BlockSpec((LANES, d), lambda bi, i: (0, 0)),
                  pl.BlockSpec((1, LANES), lambda bi, i: (0, 0)),
                  pl.BlockSpec(sel.shape, lambda bi, i: (0, 0))],
        out_specs=[row,
                   pl.BlockSpec((1, N_HEADS, bm, LANES), lambda bi, i: (bi, 0, i, 0)),
                   pl.BlockSpec((1, 1, SUBLANES, LANES), lambda bi, i: (bi, i, 0, 0))],
        out_shape=[jax.ShapeDtypeStruct((b, s, d), BF16),
                   jax.ShapeDtypeStruct((b, N_HEADS, s, LANES), BF16),
                   jax.ShapeDtypeStruct((b, s // bm, SUBLANES, LANES), F32)],
        compiler_params=_params("parallel", "parallel"),
    )(x, g, scale, shift, wf_pad, bf_pad, sel)


PIPE_SLOTS = 3
PV_LAG = 2
DENOM_ROWS = 16


def _attend(i, n_maps, scores, next_scores, vt_ref, visible, frame_shift, finish, m, acc, *bufs):
    s_buf, c_buf, p_buf, a_buf = (bufs[k * PIPE_SLOTS:(k + 1) * PIPE_SLOTS] for k in range(4))
    hd = acc.shape[1] - DENOM_ROWS
    ones = jnp.ones((DENOM_ROWS, vt_ref.shape[-1]), BF16)
    m[...] = jnp.full(m.shape, NEG_BIG, F32)
    acc[...] = jnp.zeros(acc.shape, F32)
    for slot in range(PIPE_SLOTS - PV_LAG, PIPE_SLOTS):
        p_buf[slot][...] = jnp.zeros(p_buf[slot].shape, BF16)
        a_buf[slot][...] = jnp.ones(a_buf[slot].shape, F32)

    def put_scores(slot, tiles):
        for n, s in enumerate(tiles):
            s_buf[slot][n] = s
            c_buf[slot][n] = jnp.max(s, axis=0, keepdims=True)

    def softmax(s, s_max, n):
        m_prev = m[n]
        m_new = jnp.maximum(m_prev, s_max)
        alpha = jnp.exp2(m_prev - m_new)
        p = jnp.exp2(s - m_new).astype(BF16)
        return m_new, alpha, p

    def value_product(prev, alpha, j, p):
        vt_ones = jnp.concatenate([vt_ref[0, 0, j], ones], axis=0)
        return alpha * prev + jnp.dot(vt_ones, p, preferred_element_type=F32)

    def step(j, slot, prefetch=True):
        if prefetch:
            put_scores((slot + 2) % PIPE_SLOTS, scores(jnp.minimum(j + 2, i)))
        old = (slot - PV_LAG) % PIPE_SLOTS
        for n in range(n_maps):
            acc[n] = value_product(acc[n], a_buf[old][n], jnp.maximum(j - PV_LAG, 0), p_buf[old][n])
            m_new, alpha, p = softmax(s_buf[slot][n], c_buf[slot][n], n)
            p_buf[slot][n] = p
            a_buf[slot][n] = alpha
            m[n] = m_new + frame_shift(j)

    def last(slot):
        results = []
        for n in range(n_maps):
            out = acc[n]
            for back in range(PV_LAG, 0, -1):
                old = (slot - back) % PIPE_SLOTS
                out = value_product(out, a_buf[old][n], jnp.maximum(i - back, 0), p_buf[old][n])
            s = jnp.where(visible, s_buf[slot][n], -jnp.inf)
            _, alpha, p = softmax(s, jnp.max(s, axis=0, keepdims=True), n)
            out = value_product(out, alpha, i, p)
            results.append((out[:hd], jnp.mean(out[hd:], axis=0, keepdims=True)))
        put_scores(0, next_scores(0))
        put_scores(1, next_scores(1))
        finish(results)

    @pl.when(i == 0)
    def _():
        put_scores(0, scores(0))

    def rotation(jj, carry):
        for slot in range(PIPE_SLOTS):
            step(PIPE_SLOTS * jj + slot, slot)
        return carry

    rotations = i // PIPE_SLOTS

    def two_rotations(u, carry):
        return rotation(2 * u + 1, rotation(2 * u, carry))

    lax.fori_loop(0, rotations // 2, two_rotations, 0)

    @pl.when(rotations % 2 == 1)
    def _():
        rotation(rotations - 1, 0)

    for rem in range(PIPE_SLOTS):
        @pl.when(i % PIPE_SLOTS == rem)
        def _(rem=rem):
            for slot in range(rem):
                step(i - rem + slot, slot, prefetch=slot + 2 <= rem)
            last(rem)


def _attend_scratch(n_maps, hd, tk, tq):
    return ([pltpu.VMEM((n_maps, 1, tq), F32),
             pltpu.VMEM((n_maps, hd + DENOM_ROWS, tq), F32)]
            + [pltpu.VMEM((n_maps, tk, tq), F32)] * PIPE_SLOTS
            + [pltpu.VMEM((n_maps, 1, tq), F32)] * PIPE_SLOTS
            + [pltpu.VMEM((n_maps, tk, tq), BF16)] * PIPE_SLOTS
            + [pltpu.VMEM((n_maps, 1, tq), F32)] * PIPE_SLOTS)


def _diff_kernel(lam_ref, qt_ref, k_ref, vt_ref, g_ref, o_ref, *scratch):
    i = pl.program_id(2)
    nq, t = qt_ref.shape[2], qt_ref.shape[-1]

    def both_maps(block):
        qt = qt_ref[0, 0, block]
        row = lax.broadcasted_iota(jnp.int32, qt.shape, 0)
        map1 = (row // ROPE_HALF) % 2 == 0
        zero = jnp.zeros_like(qt)
        return jnp.concatenate([jnp.where(map1, qt, zero), jnp.where(map1, zero, qt)], axis=1)

    q12 = both_maps(i)

    def scores(j):
        return (jnp.dot(k_ref[0, 0, j], q12, preferred_element_type=F32),)

    def next_scores(j):
        return (jnp.dot(k_ref[0, 0, j], both_maps(jnp.minimum(i + 1, nq - 1)), preferred_element_type=F32),)

    kc = lax.broadcasted_iota(jnp.int32, (t, 2 * t), 0) // CHUNK
    qc = (lax.broadcasted_iota(jnp.int32, (t, 2 * t), 1) % t) // CHUNK

    def finish(results):
        ((out, l),) = results
        o = out / l
        o = o[:, :t] - lam_ref[0] * o[:, t:]
        ms = jnp.mean(o * o, axis=0, keepdims=True)
        g = jnp.concatenate([g_ref[...]] * (t // LANES), axis=1)
        y = o * lax.rsqrt(ms + SUBLN_EPS) * g
        o_ref[0, 0] = y.T.astype(BF16)

    _attend(i, 1, scores, next_scores, vt_ref, kc <= qc, lambda j: 0.0, finish, *scratch)


def _diff_call(lam, qt, k, vt, g_rep):
    b, nh, nq, hd, t = qt.shape
    return pl.pallas_call(
        _diff_kernel,
        grid=(b, nh, nq),
        in_specs=[pl.BlockSpec(memory_space=pltpu.SMEM),
                  pl.BlockSpec((1, 1, nq, hd, t), lambda bi, h, i: (bi, h, 0, 0, 0)),
                  pl.BlockSpec((1, 1, nq, t, hd), lambda bi, h, i: (bi, h, 0, 0, 0)),
                  pl.BlockSpec((1, 1, nq, hd, t), lambda bi, h, i: (bi, h, 0, 0, 0)),
                  pl.BlockSpec((hd, LANES), lambda bi, h, i: (0, 0))],
        out_specs=pl.BlockSpec((1, 1, t, hd), lambda bi, h, i: (bi, h, i, 0)),
        out_shape=jax.ShapeDtypeStruct((b, nh, nq * t, hd), BF16),
        scratch_shapes=_attend_scratch(1, hd, t, 2 * t),
        compiler_params=_params("parallel", "parallel", "arbitrary"),
    )(lam, qt, k, vt, g_rep)


def _fox_kernel(tot_ref, qt_ref, k_ref, kext_ref, vt_ref, o_ref, *scratch):
    bi, h, i = pl.program_id(0), pl.program_id(1), pl.program_id(2)
    nq, t = qt_ref.shape[2], qt_ref.shape[-1]
    tot_base = (bi * N_HEADS + h) * nq

    def with_ones(block):
        qt = qt_ref[0, 0, block]
        row = lax.broadcasted_iota(jnp.int32, qt.shape, 0)
        return jnp.concatenate([qt, jnp.where(row < BIAS_TERMS, 1.0, 0.0).astype(BF16)], axis=0)

    q_cat = with_ones(i)

    def scores_of(q, j):
        k_cat = jnp.concatenate([k_ref[0, 0, j], kext_ref[0, 0, j]], axis=1)
        return (jnp.dot(k_cat, q, preferred_element_type=F32),)

    def scores(j):
        return scores_of(q_cat, j)

    def next_scores(j):
        return scores_of(with_ones(jnp.minimum(i + 1, nq - 1)), j)

    ki = lax.broadcasted_iota(jnp.int32, (t, t), 0)
    qi = lax.broadcasted_iota(jnp.int32, (t, t), 1)

    def finish(results):
        ((o, l),) = results
        o_ref[0, 0] = (o / l).T.astype(BF16)

    _attend(i, 1, scores, next_scores, vt_ref, ki <= qi, lambda j: tot_ref[tot_base + j], finish, *scratch)


def _fox_call(tot, qt, k, kext, vt):
    b, nh, nq, hd, t = qt.shape
    return pl.pallas_call(
        _fox_kernel,
        grid=(b, nh, nq),
        in_specs=[pl.BlockSpec(memory_space=pltpu.SMEM),
                  pl.BlockSpec((1, 1, nq, hd, t), lambda bi, h, i: (bi, h, 0, 0, 0)),
                  pl.BlockSpec((1, 1, nq, t, hd), lambda bi, h, i: (bi, h, 0, 0, 0)),
                  pl.BlockSpec((1, 1, nq, t, LANES), lambda bi, h, i: (bi, h, 0, 0, 0)),
                  pl.BlockSpec((1, 1, nq, hd, t), lambda bi, h, i: (bi, h, 0, 0, 0))],
        out_specs=pl.BlockSpec((1, 1, t, hd), lambda bi, h, i: (bi, h, i, 0)),
        out_shape=jax.ShapeDtypeStruct((b, nh, nq * t, hd), BF16),
        scratch_shapes=_attend_scratch(1, hd, t, t),
        compiler_params=_params("parallel", "parallel", "arbitrary"),
    )(tot, qt, k, kext, vt)


def _out_kernel(oa_ref, za_ref, ob_ref, zb_ref, ma_ref, mb_ref, x_ref, gate_ref,
                wa_ref, wb_ref, wo_ref, gf_ref, o_ref):
    def gated(o_ref_, z_ref_):
        o = jnp.concatenate([o_ref_[0, hh] for hh in range(N_HEADS)], axis=1)
        return (o.astype(F32) * z_ref_[0].astype(F32)).astype(BF16)

    ya = jnp.dot(gated(oa_ref, za_ref), wa_ref[...], preferred_element_type=F32)
    yb = jnp.dot(gated(ob_ref, zb_ref), wb_ref[...], preferred_element_type=F32)
    merged = ma_ref[0].astype(F32) * ya + mb_ref[0].astype(F32) * yb
    out = jnp.dot(merged.astype(BF16), wo_ref[...], preferred_element_type=F32)
    xn = x_ref[0] + gate_ref[0] * out
    ms = jnp.mean(xn * xn, axis=-1, keepdims=True)
    o_ref[0] = xn * lax.rsqrt(ms + NORM_EPS) * gf_ref[...]


def _out_call(oa, za, ob, zb, ma, mb, x, gate, wa, wb, wo, gf, bm):
    b, s, d = x.shape
    heads = pl.BlockSpec((1, N_HEADS, bm, HEAD_DIM), lambda bi, i: (bi, 0, i, 0))
    row = pl.BlockSpec((1, bm, d), lambda bi, i: (bi, i, 0))
    wspec = pl.BlockSpec((d, d), lambda bi, i: (0, 0))
    return pl.pallas_call(
        _out_kernel,
        grid=(b, s // bm),
        in_specs=[heads, row, heads, row, row, row, row,
                  pl.BlockSpec((1, 1, d), lambda bi, i: (bi, 0, 0)),
                  wspec, wspec, wspec,
                  pl.BlockSpec((1, d), lambda bi, i: (0, 0))],
        out_specs=row,
        out_shape=jax.ShapeDtypeStruct((b, s, d), F32),
        compiler_params=_params("parallel", "parallel"),
    )(oa, za, ob, zb, ma, mb, x, gate, wa, wb, wo, gf)


def kernel(x, c, positions, w_ada, b_ada, g_norm, w_in, b_forget, lambda_q1, lambda_k1, lambda_q2,
           lambda_k2, g_subln, w_branch_a, w_branch_b, w_out, g_final):
    b, s, d = x.shape
    assert d == D_MODEL and s % ATT_TILE == 0 and b <= SUBLANES
    assert w_ada.shape[0] == 1, "single-layer block"
    t = ATT_TILE
    nq = s // t
    n_main = 8 * D_MODEL

    c_pad = jnp.zeros((SUBLANES, d), F32).at[:b].set(c)
    w_t = jnp.swapaxes(w_in, 1, 2)
    w_qk = w_t[:, :2 * D_MODEL].reshape(2 * N_HEADS, 2, 2, ROPE_HALF, d).transpose(0, 2, 1, 3, 4)
    w_qk = w_qk.reshape(1, 2 * D_MODEL, d)
    w_merge = w_t[:, n_main + N_HEADS:]
    wf_pad = jnp.zeros((LANES, d), F32).at[:N_HEADS].set(w_t[0, n_main:n_main + N_HEADS]).astype(BF16)
    bf_pad = jnp.zeros((1, LANES), F32).at[0, :N_HEADS].set(b_forget[0])
    inv_freq = ROPE_THETA ** (-jnp.arange(ROPE_HALF, dtype=F32) / ROPE_HALF)
    invf = jnp.tile(inv_freq, LANES // ROPE_HALF)[None, :]
    sgn = jnp.concatenate([-jnp.ones((HEAD_DIM // 2,), F32), jnp.ones((HEAD_DIM // 2,), F32)])[None, :]
    pos_b = jnp.broadcast_to(positions.astype(F32)[:, :, None], (b, s, LANES))
    g_rep = jnp.broadcast_to((g_subln[0] * (1.0 - LAMBDA_INIT))[:, None], (HEAD_DIM, LANES))

    mod, lam_tile = _mod_call(c_pad, w_ada, b_ada, lambda_q1, lambda_k1, lambda_q2, lambda_k2)
    shift = mod[:b, None, 0:d]
    scale = mod[:b, None, d:2 * d]
    gate = mod[:b, None, 2 * d:3 * d]
    lam = lam_tile[0, :1]

    h, kext, tot_tile = _prenorm_cum_call(x, g_norm, scale, shift, wf_pad, bf_pad, t)
    tables = _rope_table_call(pos_b, invf, sgn, t)

    pm = PROJ_ROWS if s % PROJ_ROWS == 0 else t
    proj = functools.partial(_proj_call, h)
    qa_t = proj(w_qk, 0, tables, pm, rope=True, scale=LOG2E * DA_QK_DIM ** -0.5, layout="heads_t")
    ka = proj(w_qk, 1, tables, pm, rope=True, layout="heads")
    va_t = proj(w_t, 2, None, pm, layout="heads_t")
    za = proj(w_t, 3, None, pm, act="silu")
    qb_t = proj(w_t, 4, None, pm, scale=LOG2E * HEAD_DIM ** -0.5, layout="heads_t")
    kb = proj(w_t, 5, None, pm, layout="heads")
    vb_t = proj(w_t, 6, None, pm, layout="heads_t")
    zb = proj(w_t, 7, None, pm, act="silu")
    ma = proj(w_merge, 0, None, pm, act="sigmoid")
    mb = proj(w_merge, 1, None, pm, act="sigmoid")

    tot = tot_tile[:, :, 0, :N_HEADS].transpose(0, 2, 1).reshape(-1)

    def tiles(a):
        return a.reshape(b, N_HEADS, nq, t, a.shape[-1])

    oa = _diff_call(lam, qa_t, tiles(ka), va_t, g_rep)
    ob = _fox_call(tot, qb_t, tiles(kb), tiles(kext), vb_t)

    return _out_call(oa, za, ob, zb, ma, mb, x, gate,
                     w_branch_a[0].astype(BF16), w_branch_b[0].astype(BF16), w_out[0].astype(BF16),
                     g_final[None, :], t)
```

```python
import functools
import math

import jax
import jax.numpy as jnp
import numpy as np
from jax import lax
from jax.experimental import pallas as pl
from jax.experimental.pallas import tpu as pltpu

F32 = jnp.float32
BF16 = jnp.bfloat16

D_MODEL = 1024
N_HEADS = 8
HEAD_DIM = 128
DA_QK_DIM = 64
ROPE_HALF = DA_QK_DIM // 2
CHUNK = 64
ROPE_THETA = 10000.0
NORM_EPS = 1e-6
SUBLN_EPS = 1e-5
LAMBDA_INIT = 0.8 - 0.6 * math.exp(-0.3 * 0)
LOG2E = math.log2(math.e)

LANES = 128
SUBLANES = 8
ATT_TILE = 512
PROJ_ROWS = 1024
BIAS_TERMS = 3
VMEM_LIMIT = 48 * 1024 * 1024
NEG_BIG = -1e30


def _params(*sem):
    return pltpu.CompilerParams(dimension_semantics=sem, vmem_limit_bytes=VMEM_LIMIT)


def _mod_kernel(c_ref, w_ref, b_ref, lq1_ref, lk1_ref, lq2_ref, lk2_ref, mod_ref, lam_ref):
    c = c_ref[...]
    c_act = c * jax.nn.sigmoid(c)
    mod_ref[...] = jnp.dot(c_act, w_ref[0], precision=lax.Precision.HIGHEST,
                           preferred_element_type=F32) + b_ref[...]
    s1 = jnp.sum(lq1_ref[...] * lk1_ref[...], axis=-1, keepdims=True)
    s2 = jnp.sum(lq2_ref[...] * lk2_ref[...], axis=-1, keepdims=True)
    lam = jnp.exp(s1) - jnp.exp(s2) + LAMBDA_INIT
    lam_ref[...] = jnp.broadcast_to(lam, lam_ref.shape)


def _mod_call(c_pad, w_ada, b_ada, lq1, lk1, lq2, lk2):
    d = D_MODEL
    lam_spec = pl.BlockSpec((1, DA_QK_DIM), lambda j: (0, 0))
    return pl.pallas_call(
        _mod_kernel,
        grid=(3,),
        in_specs=[pl.BlockSpec((SUBLANES, d), lambda j: (0, 0)),
                  pl.BlockSpec((1, d, d), lambda j: (0, 0, j)),
                  pl.BlockSpec((1, d), lambda j: (0, j)),
                  lam_spec, lam_spec, lam_spec, lam_spec],
        out_specs=[pl.BlockSpec((SUBLANES, d), lambda j: (0, j)),
                   pl.BlockSpec((SUBLANES, LANES), lambda j: (0, 0))],
        out_shape=[jax.ShapeDtypeStruct((SUBLANES, 3 * d), F32),
                   jax.ShapeDtypeStruct((SUBLANES, LANES), F32)],
        compiler_params=_params("arbitrary"),
    )(c_pad, w_ada, b_ada, lq1, lk1, lq2, lk2)


def _rope_table_kernel(pos_ref, invf_ref, sgn_ref, cos_ref, sin_ref):
    ang = pos_ref[0] * invf_ref[...]
    cos_ref[0] = jnp.cos(ang)
    sin_ref[0] = jnp.sin(ang) * sgn_ref[...]


def _rope_table_call(pos_b, invf, sgn, bm):
    b, s, _ = pos_b.shape
    row = pl.BlockSpec((1, bm, LANES), lambda bi, i: (bi, i, 0))
    vec = pl.BlockSpec((1, LANES), lambda bi, i: (0, 0))
    shape = jax.ShapeDtypeStruct((b, s, LANES), F32)
    return pl.pallas_call(
        _rope_table_kernel,
        grid=(b, s // bm),
        in_specs=[row, vec, vec],
        out_specs=[row, row],
        out_shape=[shape, shape],
        compiler_params=_params("parallel", "parallel"),
    )(pos_b, invf, sgn)


def _proj_kernel(*refs, rope, scale, act, layout):
    if rope:
        h_ref, w_ref, cos_ref, sin_ref, o_ref = refs
    else:
        h_ref, w_ref, o_ref = refs
    acc = lax.dot_general(h_ref[0], w_ref[0].astype(BF16), (((1,), (1,)), ((), ())),
                          preferred_element_type=F32)
    if act == "silu":
        acc = acc * jax.nn.sigmoid(acc)
    elif act == "sigmoid":
        acc = jax.nn.sigmoid(acc)
    if layout == "nat":
        o_ref[0] = acc.astype(BF16)
        return
    if rope:
        cos = cos_ref[0]
        sin = sin_ref[0]
    for hh in range(N_HEADS):
        t = acc[:, hh * HEAD_DIM:(hh + 1) * HEAD_DIM]
        if rope:
            t = t * cos + pltpu.roll(t, HEAD_DIM // 2, 1) * sin
        if scale != 1.0:
            t = t * scale
        if layout == "heads":
            o_ref[0, hh] = t.astype(BF16)
        else:
            tile = o_ref.shape[-1]
            for part in range(o_ref.shape[2]):
                o_ref[0, hh, part] = t[part * tile:(part + 1) * tile].T.astype(BF16)


def _proj_call(h, w_all, col_block, tables, bm, *, rope=False, scale=1.0, act=None, layout="nat"):
    b, s, d = h.shape
    n = D_MODEL
    in_specs = [pl.BlockSpec((1, bm, d), lambda bi, i: (bi, i, 0)),
                pl.BlockSpec((1, n, d), lambda bi, i: (0, col_block, 0))]
    args = [h, w_all]
    if rope:
        tab = pl.BlockSpec((1, bm, LANES), lambda bi, i: (bi, i, 0))
        in_specs += [tab, tab]
        args += list(tables)
    if layout == "nat":
        out_spec = pl.BlockSpec((1, bm, n), lambda bi, i: (bi, i, 0))
        out_shape = jax.ShapeDtypeStruct((b, s, n), BF16)
    elif layout == "heads":
        out_spec = pl.BlockSpec((1, N_HEADS, bm, HEAD_DIM), lambda bi, i: (bi, 0, i, 0))
        out_shape = jax.ShapeDtypeStruct((b, N_HEADS, s, HEAD_DIM), BF16)
    else:
        parts = bm // ATT_TILE
        out_spec = pl.BlockSpec((1, N_HEADS, parts, HEAD_DIM, ATT_TILE), lambda bi, i: (bi, 0, i, 0, 0))
        out_shape = jax.ShapeDtypeStruct((b, N_HEADS, s // ATT_TILE, HEAD_DIM, ATT_TILE), BF16)
    return pl.pallas_call(
        functools.partial(_proj_kernel, rope=rope, scale=scale, act=act, layout=layout),
        grid=(b, s // bm),
        in_specs=in_specs,
        out_specs=out_spec,
        out_shape=out_shape,
        compiler_params=_params("parallel", "parallel"),
    )(*args)


def _split3(v):
    hi = v.astype(BF16).astype(F32)
    r1 = v - hi
    mid = r1.astype(BF16).astype(F32)
    return hi, mid, r1 - mid


def _prenorm_cum_kernel(x_ref, g_ref, scale_ref, shift_ref, wf_ref, bf_ref, sel_ref, h_ref, kext_ref, tot_ref):
    bm = x_ref.shape[1]
    x = x_ref[0]
    ms = jnp.mean(x * x, axis=-1, keepdims=True)
    y = x * lax.rsqrt(ms + NORM_EPS) * g_ref[...]
    h = (y * (1.0 + scale_ref[0]) + shift_ref[0]).astype(BF16)
    h_ref[0] = h
    logit = lax.dot_general(h, wf_ref[...], (((1,), (1,)), ((), ())),
                            preferred_element_type=F32) + bf_ref[...]
    log_f = jnp.minimum(logit, 0.0) - jnp.log(1.0 + jnp.exp(-jnp.abs(logit)))
    r = lax.broadcasted_iota(jnp.int32, (bm, bm), 0)
    c = lax.broadcasted_iota(jnp.int32, (bm, bm), 1)
    tri = jnp.where(c <= r, 1.0, 0.0).astype(BF16)
    terms = jnp.concatenate([part.astype(BF16) for part in _split3(log_f)], axis=1)
    sums = jnp.dot(tri, terms, preferred_element_type=F32)
    cum = sum(sums[:, n * LANES:(n + 1) * LANES] for n in range(BIAS_TERMS))
    bias = cum * (-LOG2E)
    tot_ref[0, 0] = jnp.broadcast_to(-bias[bm - 1:bm, :], tot_ref.shape[2:])
    parts = jnp.concatenate([part.astype(BF16) for part in _split3(bias)], axis=1)
    ext = jnp.dot(parts, sel_ref[...], preferred_element_type=F32)
    for hh in range(N_HEADS):
        kext_ref[0, hh] = ext[:, hh * LANES:(hh + 1) * LANES].astype(BF16)


def _bias_selection():
    sel = np.zeros((BIAS_TERMS, LANES, N_HEADS, LANES), np.float32)
    for n in range(BIAS_TERMS):
        for hh in range(N_HEADS):
            sel[n, hh, hh, n] = 1.0
    return sel.reshape(BIAS_TERMS * LANES, N_HEADS * LANES)


def _prenorm_cum_call(x, g, scale, shift, wf_pad, bf_pad, bm):
    b, s, d = x.shape
    sel = jnp.asarray(_bias_selection(), BF16)
    row = pl.BlockSpec((1, bm, d), lambda bi, i: (bi, i, 0))
    vec = pl.BlockSpec((1, 1, d), lambda bi, i: (bi, 0, 0))
    return pl.pallas_call(
        _prenorm_cum_kernel,
        grid=(b, s // bm),
        in_specs=[row, pl.BlockSpec((1, d), lambda bi, i: (0, 0)), vec, vec,
                  pl.BlockSpec((LANES, d), lambda bi, i: (0, 0)),
                  pl.BlockSpec((1, LANES), lambda bi, i: (0, 0)),
                  pl.BlockSpec(sel.shape, lambda bi, i: (0, 0))],
        out_specs=[row,
                   pl.BlockSpec((1, N_HEADS, bm, LANES), lambda bi, i: (bi, 0, i, 0)),
                   pl.BlockSpec((1, 1, SUBLANES, LANES), lambda bi, i: (bi, i, 0, 0))],
        out_shape=[jax.ShapeDtypeStruct((b, s, d), BF16),
                   jax.ShapeDtypeStruct((b, N_HEADS, s, LANES), BF16),
                   jax.ShapeDtypeStruct((b, s // bm, SUBLANES, LANES), F32)],
        compiler_params=_params("parallel", "parallel"),
    )(x, g, scale, shift, wf_pad, bf_pad, sel)


PIPE_SLOTS = 3
PV_LAG = 2
DENOM_ROWS = 16


def _attend(i, n_maps, scores, next_scores, vt_ref, visible, frame_shift, finish, m, acc, *bufs):
    s_buf, c_buf, p_buf, a_buf = (bufs[k * PIPE_SLOTS:(k + 1) * PIPE_SLOTS] for k in range(4))
    hd = acc.shape[1] - DENOM_ROWS
    ones = jnp.ones((DENOM_ROWS, vt_ref.shape[-1]), BF16)
    m[...] = jnp.full(m.shape, NEG_BIG, F32)
    acc[...] = jnp.zeros(acc.shape, F32)
    for slot in range(PIPE_SLOTS - PV_LAG, PIPE_SLOTS):
        p_buf[slot][...] = jnp.zeros(p_buf[slot].shape, BF16)
        a_buf[slot][...] = jnp.ones(a_buf[slot].shape, F32)

    def put_scores(slot, tiles):
        for n, s in enumerate(tiles):
            s_buf[slot][n] = s
            c_buf[slot][n] = jnp.max(s, axis=0, keepdims=True)

    def softmax(s, s_max, n):
        m_prev = m[n]
        m_new = jnp.maximum(m_prev, s_max)
        alpha = jnp.exp2(m_prev - m_new)
        p = jnp.exp2(s - m_new).astype(BF16)
        return m_new, alpha, p

    def value_product(prev, alpha, j, p):
        vt_ones = jnp.concatenate([vt_ref[0, 0, j], ones], axis=0)
        return alpha * prev + jnp.dot(vt_ones, p, preferred_element_type=F32)

    def step(j, slot, prefetch=True):
        if prefetch:
            put_scores((slot + 2) % PIPE_SLOTS, scores(jnp.minimum(j + 2, i)))
        old = (slot - PV_LAG) % PIPE_SLOTS
        for n in range(n_maps):
            acc[n] = value_product(acc[n], a_buf[old][n], jnp.maximum(j - PV_LAG, 0), p_buf[old][n])
            m_new, alpha, p = softmax(s_buf[slot][n], c_buf[slot][n], n)
            p_buf[slot][n] = p
            a_buf[slot][n] = alpha
            m[n] = m_new + frame_shift(j)

    def last(slot):
        results = []
        for n in range(n_maps):
            out = acc[n]
            for back in range(PV_LAG, 0, -1):
                old = (slot - back) % PIPE_SLOTS
                out = value_product(out, a_buf[old][n], jnp.maximum(i - back, 0), p_buf[old][n])
            s = jnp.where(visible, s_buf[slot][n], -jnp.inf)
            _, alpha, p = softmax(s, jnp.max(s, axis=0, keepdims=True), n)
            out = value_product(out, alpha, i, p)
            results.append((out[:hd], jnp.mean(out[hd:], axis=0, keepdims=True)))
        put_scores(0, next_scores(0))
        put_scores(1, next_scores(1))
        finish(results)

    @pl.when(i == 0)
    def _():
        put_scores(0, scores(0))

    def rotation(jj, carry):
        for slot in range(PIPE_SLOTS):
            step(PIPE_SLOTS * jj + slot, slot)
        return carry

    trip = 2 * PIPE_SLOTS

    def two_rotations(u, carry):
        return rotation(2 * u + 1, rotation(2 * u, carry))

    lax.fori_loop(0, i // trip, two_rotations, 0)

    for rem in range(trip):
        @pl.when(i % trip == rem)
        def _(rem=rem):
            for k in range(rem):
                step(i - rem + k, k % PIPE_SLOTS, prefetch=k + 2 <= rem)
            last(rem % PIPE_SLOTS)


def _attend_scratch(n_maps, hd, tk, tq):
    return ([pltpu.VMEM((n_maps, 1, tq), F32),
             pltpu.VMEM((n_maps, hd + DENOM_ROWS, tq), F32)]
            + [pltpu.VMEM((n_maps, tk, tq), F32)] * PIPE_SLOTS
            + [pltpu.VMEM((n_maps, 1, tq), F32)] * PIPE_SLOTS
            + [pltpu.VMEM((n_maps, tk, tq), BF16)] * PIPE_SLOTS
            + [pltpu.VMEM((n_maps, 1, tq), F32)] * PIPE_SLOTS)


def _diff_kernel(lam_ref, qt_ref, k_ref, vt_ref, g_ref, o_ref, *scratch):
    i = pl.program_id(2)
    nq, t = qt_ref.shape[2], qt_ref.shape[-1]

    def both_maps(block):
        qt = qt_ref[0, 0, block]
        row = lax.broadcasted_iota(jnp.int32, qt.shape, 0)
        map1 = (row // ROPE_HALF) % 2 == 0
        zero = jnp.zeros_like(qt)
        return jnp.concatenate([jnp.where(map1, qt, zero), jnp.where(map1, zero, qt)], axis=1)

    q12 = both_maps(i)

    def scores(j):
        return (jnp.dot(k_ref[0, 0, j], q12, preferred_element_type=F32),)

    def next_scores(j):
        return (jnp.dot(k_ref[0, 0, j], both_maps(jnp.minimum(i + 1, nq - 1)), preferred_element_type=F32),)

    kc = lax.broadcasted_iota(jnp.int32, (t, 2 * t), 0) // CHUNK
    qc = (lax.broadcasted_iota(jnp.int32, (t, 2 * t), 1) % t) // CHUNK

    def finish(results):
        ((out, l),) = results
        o = out / l
        o = o[:, :t] - lam_ref[0] * o[:, t:]
        ms = jnp.mean(o * o, axis=0, keepdims=True)
        g = jnp.concatenate([g_ref[...]] * (t // LANES), axis=1)
        y = o * lax.rsqrt(ms + SUBLN_EPS) * g
        o_ref[0, 0] = y.T.astype(BF16)

    _attend(i, 1, scores, next_scores, vt_ref, kc <= qc, lambda j: 0.0, finish, *scratch)


def _diff_call(lam, qt, k, vt, g_rep):
    b, nh, nq, hd, t = qt.shape
    return pl.pallas_call(
        _diff_kernel,
        grid=(b, nh, nq),
        in_specs=[pl.BlockSpec(memory_space=pltpu.SMEM),
                  pl.BlockSpec((1, 1, nq, hd, t), lambda bi, h, i: (bi, h, 0, 0, 0)),
                  pl.BlockSpec((1, 1, nq, t, hd), lambda bi, h, i: (bi, h, 0, 0, 0)),
                  pl.BlockSpec((1, 1, nq, hd, t), lambda bi, h, i: (bi, h, 0, 0, 0)),
                  pl.BlockSpec((hd, LANES), lambda bi, h, i: (0, 0))],
        out_specs=pl.BlockSpec((1, 1, t, hd), lambda bi, h, i: (bi, h, i, 0)),
        out_shape=jax.ShapeDtypeStruct((b, nh, nq * t, hd), BF16),
        scratch_shapes=_attend_scratch(1, hd, t, 2 * t),
        compiler_params=_params("parallel", "parallel", "arbitrary"),
    )(lam, qt, k, vt, g_rep)


def _fox_kernel(tot_ref, qt_ref, k_ref, kext_ref, vt_ref, o_ref, *scratch):
    bi, h, i = pl.program_id(0), pl.program_id(1), pl.program_id(2)
    nq, t = qt_ref.shape[2], qt_ref.shape[-1]
    tot_base = (bi * N_HEADS + h) * nq

    def with_ones(block):
        qt = qt_ref[0, 0, block]
        row = lax.broadcasted_iota(jnp.int32, qt.shape, 0)
        return jnp.concatenate([qt, jnp.where(row < BIAS_TERMS, 1.0, 0.0).astype(BF16)], axis=0)

    q_cat = with_ones(i)

    def scores_of(q, j):
        k_cat = jnp.concatenate([k_ref[0, 0, j], kext_ref[0, 0, j]], axis=1)
        return (jnp.dot(k_cat, q, preferred_element_type=F32),)

    def scores(j):
        return scores_of(q_cat, j)

    def next_scores(j):
        return scores_of(with_ones(jnp.minimum(i + 1, nq - 1)), j)

    ki = lax.broadcasted_iota(jnp.int32, (t, t), 0)
    qi = lax.broadcasted_iota(jnp.int32, (t, t), 1)

    def finish(results):
        ((o, l),) = results
        o_ref[0, 0] = (o / l).T.astype(BF16)

    _attend(i, 1, scores, next_scores, vt_ref, ki <= qi, lambda j: tot_ref[tot_base + j], finish, *scratch)


def _fox_call(tot, qt, k, kext, vt):
    b, nh, nq, hd, t = qt.shape
    return pl.pallas_call(
        _fox_kernel,
        grid=(b, nh, nq),
        in_specs=[pl.BlockSpec(memory_space=pltpu.SMEM),
                  pl.BlockSpec((1, 1, nq, hd, t), lambda bi, h, i: (bi, h, 0, 0, 0)),
                  pl.BlockSpec((1, 1, nq, t, hd), lambda bi, h, i: (bi, h, 0, 0, 0)),
                  pl.BlockSpec((1, 1, nq, t, LANES), lambda bi, h, i: (bi, h, 0, 0, 0)),
                  pl.BlockSpec((1, 1, nq, hd, t), lambda bi, h, i: (bi, h, 0, 0, 0))],
        out_specs=pl.BlockSpec((1, 1, t, hd), lambda bi, h, i: (bi, h, i, 0)),
        out_shape=jax.ShapeDtypeStruct((b, nh, nq * t, hd), BF16),
        scratch_shapes=_attend_scratch(1, hd, t, t),
        compiler_params=_params("parallel", "parallel", "arbitrary"),
    )(tot, qt, k, kext, vt)


def _out_kernel(oa_ref, za_ref, ob_ref, zb_ref, ma_ref, mb_ref, x_ref, gate_ref,
                wa_ref, wb_ref, wo_ref, gf_ref, o_ref):
    def gated(o_ref_, z_ref_):
        o = jnp.concatenate([o_ref_[0, hh] for hh in range(N_HEADS)], axis=1)
        return (o.astype(F32) * z_ref_[0].astype(F32)).astype(BF16)

    ya = jnp.dot(gated(oa_ref, za_ref), wa_ref[...], preferred_element_type=F32)
    yb = jnp.dot(gated(ob_ref, zb_ref), wb_ref[...], preferred_element_type=F32)
    merged = ma_ref[0].astype(F32) * ya + mb_ref[0].astype(F32) * yb
    out = jnp.dot(merged.astype(BF16), wo_ref[...], preferred_element_type=F32)
    xn = x_ref[0] + gate_ref[0] * out
    ms = jnp.mean(xn * xn, axis=-1, keepdims=True)
    o_ref[0] = xn * lax.rsqrt(ms + NORM_EPS) * gf_ref[...]


def _out_call(oa, za, ob, zb, ma, mb, x, gate, wa, wb, wo, gf, bm):
    b, s, d = x.shape
    heads = pl.BlockSpec((1, N_HEADS, bm, HEAD_DIM), lambda bi, i: (bi, 0, i, 0))
    row = pl.BlockSpec((1, bm, d), lambda bi, i: (bi, i, 0))
    wspec = pl.BlockSpec((d, d), lambda bi, i: (0, 0))
    return pl.pallas_call(
        _out_kernel,
        grid=(b, s // bm),
        in_specs=[heads, row, heads, row, row, row, row,
                  pl.BlockSpec((1, 1, d), lambda bi, i: (bi, 0, 0)),
                  wspec, wspec, wspec,
                  pl.BlockSpec((1, d), lambda bi, i: (0, 0))],
        out_specs=row,
        out_shape=jax.ShapeDtypeStruct((b, s, d), F32),
        compiler_params=_params("parallel", "parallel"),
    )(oa, za, ob, zb, ma, mb, x, gate, wa, wb, wo, gf)


def kernel(x, c, positions, w_ada, b_ada, g_norm, w_in, b_forget, lambda_q1, lambda_k1, lambda_q2,
           lambda_k2, g_subln, w_branch_a, w_branch_b, w_out, g_final):
    b, s, d = x.shape
    assert d == D_MODEL and s % ATT_TILE == 0 and b <= SUBLANES
    assert w_ada.shape[0] == 1, "single-layer block"
    t = ATT_TILE
    nq = s // t
    n_main = 8 * D_MODEL

    c_pad = jnp.zeros((SUBLANES, d), F32).at[:b].set(c)
    w_t = jnp.swapaxes(w_in, 1, 2)
    w_qk = w_t[:, :2 * D_MODEL].reshape(2 * N_HEADS, 2, 2, ROPE_HALF, d).transpose(0, 2, 1, 3, 4)
    w_qk = w_qk.reshape(1, 2 * D_MODEL, d)
    w_merge = w_t[:, n_main + N_HEADS:]
    wf_pad = jnp.zeros((LANES, d), F32).at[:N_HEADS].set(w_t[0, n_main:n_main + N_HEADS]).astype(BF16)
    bf_pad = jnp.zeros((1, LANES), F32).at[0, :N_HEADS].set(b_forget[0])
    inv_freq = ROPE_THETA ** (-jnp.arange(ROPE_HALF, dtype=F32) / ROPE_HALF)
    invf = jnp.tile(inv_freq, LANES // ROPE_HALF)[None, :]
    sgn = jnp.concatenate([-jnp.ones((HEAD_DIM // 2,), F32), jnp.ones((HEAD_DIM // 2,), F32)])[None, :]
    pos_b = jnp.broadcast_to(positions.astype(F32)[:, :, None], (b, s, LANES))
    g_rep = jnp.broadcast_to((g_subln[0] * (1.0 - LAMBDA_INIT))[:, None], (HEAD_DIM, LANES))

    mod, lam_tile = _mod_call(c_pad, w_ada, b_ada, lambda_q1, lambda_k1, lambda_q2, lambda_k2)
    shift = mod[:b, None, 0:d]
    scale = mod[:b, None, d:2 * d]
    gate = mod[:b, None, 2 * d:3 * d]
    lam = lam_tile[0, :1]

    h, kext, tot_tile = _prenorm_cum_call(x, g_norm, scale, shift, wf_pad, bf_pad, t)
    tables = _rope_table_call(pos_b, invf, sgn, t)

    pm = PROJ_ROWS if s % PROJ_ROWS == 0 else t
    proj = functools.partial(_proj_call, h)
    qa_t = proj(w_qk, 0, tables, pm, rope=True, scale=LOG2E * DA_QK_DIM ** -0.5, layout="heads_t")
    ka = proj(w_qk, 1, tables, pm, rope=True, layout="heads")
    va_t = proj(w_t, 2, None, pm, layout="heads_t")
    za = proj(w_t, 3, None, pm, act="silu")
    qb_t = proj(w_t, 4, None, pm, scale=LOG2E * HEAD_DIM ** -0.5, layout="heads_t")
    kb = proj(w_t, 5, None, pm, layout="heads")
    vb_t = proj(w_t, 6, None, pm, layout="heads_t")
    zb = proj(w_t, 7, None, pm, act="silu")
    ma = proj(w_merge, 0, None, pm, act="sigmoid")
    mb = proj(w_merge, 1, None, pm, act="sigmoid")

    tot = tot_tile[:, :, 0, :N_HEADS].transpose(0, 2, 1).reshape(-1)

    def tiles(a):
        return a.reshape(b, N_HEADS, nq, t, a.shape[-1])

    oa = _diff_call(lam, qa_t, tiles(ka), va_t, g_rep)
    ob = _fox_call(tot, qb_t, tiles(kb), tiles(kext), vb_t)

    return _out_call(oa, za, ob, zb, ma, mb, x, gate,
                     w_branch_a[0].astype(BF16), w_branch_b[0].astype(BF16), w_out[0].astype(BF16),
                     g_final[None, :], t)
```

```python
import functools
import math

import jax
import jax.numpy as jnp
import numpy as np
from jax import lax
from jax.experimental import pallas as pl
from jax.experimental.pallas import tpu as pltpu

F32 = jnp.float32
BF16 = jnp.bfloat16

D_MODEL = 1024
N_HEADS = 8
HEAD_DIM = 128
DA_QK_DIM = 64
ROPE_HALF = DA_QK_DIM // 2
CHUNK = 64
ROPE_THETA = 10000.0
NORM_EPS = 1e-6
SUBLN_EPS = 1e-5
LAMBDA_INIT = 0.8 - 0.6 * math.exp(-0.3 * 0)
LOG2E = math.log2(math.e)

LANES = 128
SUBLANES = 8
ATT_TILE = 512
PROJ_ROWS = 1024
BIAS_TERMS = 3
VMEM_LIMIT = 48 * 1024 * 1024
NEG_BIG = -1e30


def _params(*sem):
    return pltpu.CompilerParams(dimension_semantics=sem, vmem_limit_bytes=VMEM_LIMIT)


def _mod_kernel(c_ref, w_ref, b_ref, lq1_ref, lk1_ref, lq2_ref, lk2_ref, mod_ref, lam_ref):
    c = c_ref[...]
    c_act = c * jax.nn.sigmoid(c)
    mod_ref[...] = jnp.dot(c_act, w_ref[0], precision=lax.Precision.HIGHEST,
                           preferred_element_type=F32) + b_ref[...]
    s1 = jnp.sum(lq1_ref[...] * lk1_ref[...], axis=-1, keepdims=True)
    s2 = jnp.sum(lq2_ref[...] * lk2_ref[...], axis=-1, keepdims=True)
    lam = jnp.exp(s1) - jnp.exp(s2) + LAMBDA_INIT
    lam_ref[...] = jnp.broadcast_to(lam, lam_ref.shape)


def _mod_call(c_pad, w_ada, b_ada, lq1, lk1, lq2, lk2):
    d = D_MODEL
    lam_spec = pl.BlockSpec((1, DA_QK_DIM), lambda j: (0, 0))
    return pl.pallas_call(
        _mod_kernel,
        grid=(3,),
        in_specs=[pl.BlockSpec((SUBLANES, d), lambda j: (0, 0)),
                  pl.BlockSpec((1, d, d), lambda j: (0, 0, j)),
                  pl.BlockSpec((1, d), lambda j: (0, j)),
                  lam_spec, lam_spec, lam_spec, lam_spec],
        out_specs=[pl.BlockSpec((SUBLANES, d), lambda j: (0, j)),
                   pl.BlockSpec((SUBLANES, LANES), lambda j: (0, 0))],
        out_shape=[jax.ShapeDtypeStruct((SUBLANES, 3 * d), F32),
                   jax.ShapeDtypeStruct((SUBLANES, LANES), F32)],
        compiler_params=_params("arbitrary"),
    )(c_pad, w_ada, b_ada, lq1, lk1, lq2, lk2)


def _rope_table_kernel(pos_ref, invf_ref, sgn_ref, cos_ref, sin_ref):
    ang = pos_ref[0] * invf_ref[...]
    cos_ref[0] = jnp.cos(ang)
    sin_ref[0] = jnp.sin(ang) * sgn_ref[...]


def _rope_table_call(pos_b, invf, sgn, bm):
    b, s, _ = pos_b.shape
    row = pl.BlockSpec((1, bm, LANES), lambda bi, i: (bi, i, 0))
    vec = pl.BlockSpec((1, LANES), lambda bi, i: (0, 0))
    shape = jax.ShapeDtypeStruct((b, s, LANES), F32)
    return pl.pallas_call(
        _rope_table_kernel,
        grid=(b, s // bm),
        in_specs=[row, vec, vec],
        out_specs=[row, row],
        out_shape=[shape, shape],
        compiler_params=_params("parallel", "parallel"),
    )(pos_b, invf, sgn)


def _proj_kernel(*refs, rope, scale, act, layout):
    if rope:
        h_ref, w_ref, cos_ref, sin_ref, o_ref = refs
    else:
        h_ref, w_ref, o_ref = refs
    acc = lax.dot_general(h_ref[0], w_ref[0].astype(BF16), (((1,), (1,)), ((), ())),
                          preferred_element_type=F32)
    if act == "silu":
        acc = acc * jax.nn.sigmoid(acc)
    elif act == "sigmoid":
        acc = jax.nn.sigmoid(acc)
    if layout == "nat":
        o_ref[0] = acc.astype(BF16)
        return
    if rope:
        cos = cos_ref[0]
        sin = sin_ref[0]
    for hh in range(N_HEADS):
        t = acc[:, hh * HEAD_DIM:(hh + 1) * HEAD_DIM]
        if rope:
            t = t * cos + pltpu.roll(t, HEAD_DIM // 2, 1) * sin
        if scale != 1.0:
            t = t * scale
        if layout == "heads":
            o_ref[0, hh] = t.astype(BF16)
        else:
            tile = o_ref.shape[-1]
            for part in range(o_ref.shape[2]):
                o_ref[0, hh, part] = t[part * tile:(part + 1) * tile].T.astype(BF16)


def _proj_call(h, w_all, col_block, tables, bm, *, rope=False, scale=1.0, act=None, layout="nat"):
    b, s, d = h.shape
    n = D_MODEL
    in_specs = [pl.BlockSpec((1, bm, d), lambda bi, i: (bi, i, 0)),
                pl.BlockSpec((1, n, d), lambda bi, i: (0, col_block, 0))]
    args = [h, w_all]
    if rope:
        tab = pl.BlockSpec((1, bm, LANES), lambda bi, i: (bi, i, 0))
        in_specs += [tab, tab]
        args += list(tables)
    if layout == "nat":
        out_spec = pl.BlockSpec((1, bm, n), lambda bi, i: (bi, i, 0))
        out_shape = jax.ShapeDtypeStruct((b, s, n), BF16)
    elif layout == "heads":
        out_spec = pl.BlockSpec((1, N_HEADS, bm, HEAD_DIM), lambda bi, i: (bi, 0, i, 0))
        out_shape = jax.ShapeDtypeStruct((b, N_HEADS, s, HEAD_DIM), BF16)
    else:
        parts = bm // ATT_TILE
        out_spec = pl.BlockSpec((1, N_HEADS, parts, HEAD_DIM, ATT_TILE), lambda bi, i: (bi, 0, i, 0, 0))
        out_shape = jax.ShapeDtypeStruct((b, N_HEADS, s // ATT_TILE, HEAD_DIM, ATT_TILE), BF16)
    return pl.pallas_call(
        functools.partial(_proj_kernel, rope=rope, scale=scale, act=act, layout=layout),
        grid=(b, s // bm),
        in_specs=in_specs,
        out_specs=out_spec,
        out_shape=out_shape,
        compiler_params=_params("parallel", "parallel"),
    )(*args)


def _split3(v):
    hi = v.astype(BF16).astype(F32)
    r1 = v - hi
    mid = r1.astype(BF16).astype(F32)
    return hi, mid, r1 - mid


def _prenorm_cum_kernel(x_ref, g_ref, scale_ref, shift_ref, wf_ref, bf_ref, sel_ref, h_ref, kext_ref, tot_ref):
    bm = x_ref.shape[1]
    x = x_ref[0]
    ms = jnp.mean(x * x, axis=-1, keepdims=True)
    y = x * lax.rsqrt(ms + NORM_EPS) * g_ref[...]
    h = (y * (1.0 + scale_ref[0]) + shift_ref[0]).astype(BF16)
    h_ref[0] = h
    logit = lax.dot_general(h, wf_ref[...], (((1,), (1,)), ((), ())),
                            preferred_element_type=F32) + bf_ref[...]
    log_f = jnp.minimum(logit, 0.0) - jnp.log(1.0 + jnp.exp(-jnp.abs(logit)))
    r = lax.broadcasted_iota(jnp.int32, (bm, bm), 0)
    c = lax.broadcasted_iota(jnp.int32, (bm, bm), 1)
    tri = jnp.where(c <= r, 1.0, 0.0).astype(BF16)
    terms = jnp.concatenate([part.astype(BF16) for part in _split3(log_f)], axis=1)
    sums = jnp.dot(tri, terms, preferred_element_type=F32)
    cum = sum(sums[:, n * LANES:(n + 1) * LANES] for n in range(BIAS_TERMS))
    bias = cum * (-LOG2E)
    tot_ref[0, 0] = jnp.broadcast_to(-bias[bm - 1:bm, :], tot_ref.shape[2:])
    parts = jnp.concatenate([part.astype(BF16) for part in _split3(bias)], axis=1)
    ext = jnp.dot(parts, sel_ref[...], preferred_element_type=F32)
    for hh in range(N_HEADS):
        kext_ref[0, hh] = ext[:, hh * LANES:(hh + 1) * LANES].astype(BF16)


def _bias_selection():
    sel = np.zeros((BIAS_TERMS, LANES, N_HEADS, LANES), np.float32)
    for n in range(BIAS_TERMS):
        for hh in range(N_HEADS):
            sel[n, hh, hh, n] = 1.0
    return sel.reshape(BIAS_TERMS * LANES, N_HEADS * LANES)


def _prenorm_cum_call(x, g, scale, shift, wf_pad, bf_pad, bm):
    b, s, d = x.shape
    sel = jnp.asarray(_bias_selection(), BF16)
    row = pl.BlockSpec((1, bm, d), lambda bi, i: (bi, i, 0))
    vec = pl.BlockSpec((1, 1, d), lambda bi, i: (bi, 0, 0))
    return pl.pallas_call(
        _prenorm_cum_kernel,
        grid=(b, s // bm),
        in_specs=[row, pl.BlockSpec((1, d), lambda bi, i: (0, 0)), vec, vec,
                  pl.BlockSpec((LANES, d), lambda bi, i: (0, 0)),
                  pl.BlockSpec((1, LANES), lambda bi, i: (0, 0)),
                  pl.BlockSpec(sel.shape, lambda bi, i: (0, 0))],
        out_specs=[row,
                   pl.BlockSpec((1, N_HEADS, bm, LANES), lambda bi, i: (bi, 0, i, 0)),
                   pl.BlockSpec((1, 1, SUBLANES, LANES), lambda bi, i: (bi, i, 0, 0))],
        out_shape=[jax.ShapeDtypeStruct((b, s, d), BF16),
                   jax.ShapeDtypeStruct((b, N_HEADS, s, LANES), BF16),
                   jax.ShapeDtypeStruct((b, s // bm, SUBLANES, LANES), F32)],
        compiler_params=_params("parallel", "parallel"),
    )(x, g, scale, shift, wf_pad, bf_pad, sel)


PIPE_SLOTS = 3
PV_LAG = 2
DENOM_ROWS = 16


def _attend(i, n_maps, scores, next_scores, vt_ref, visible, frame_shift, finish, m, acc, *bufs):
    s_buf, c_buf, p_buf, a_buf = (bufs[k * PIPE_SLOTS:(k + 1) * PIPE_SLOTS] for k in range(4))
    hd = acc.shape[1] - DENOM_ROWS
    ones = jnp.ones((DENOM_ROWS, vt_ref.shape[-1]), BF16)
    m[...] = jnp.full(m.shape, NEG_BIG, F32)

    def put_scores(slot, tiles):
        for n, s in enumerate(tiles):
            s_buf[slot][n] = s
            c_buf[slot][n] = jnp.max(s, axis=0, keepdims=True)

    def softmax(s, s_max, n):
        m_prev = m[n]
        m_new = jnp.maximum(m_prev, s_max)
        alpha = jnp.exp2(m_prev - m_new)
        p = jnp.exp2(s - m_new).astype(BF16)
        return m_new, alpha, p

    def value_product(prev, alpha, j, p):
        vt_ones = jnp.concatenate([vt_ref[0, 0, j], ones], axis=0)
        return alpha * prev + jnp.dot(vt_ones, p, preferred_element_type=F32)

    def step(j, slot, prefetch=True):
        if prefetch:
            put_scores((slot + 2) % PIPE_SLOTS, scores(jnp.minimum(j + 2, i)))
        old = (slot - PV_LAG) % PIPE_SLOTS
        for n in range(n_maps):
            acc[n] = value_product(acc[n], a_buf[old][n], jnp.maximum(j - PV_LAG, 0), p_buf[old][n])
            m_new, alpha, p = softmax(s_buf[slot][n], c_buf[slot][n], n)
            p_buf[slot][n] = p
            a_buf[slot][n] = alpha
            m[n] = m_new + frame_shift(j)

    def last(slot):
        results = []
        for n in range(n_maps):
            out = acc[n]
            for back in range(PV_LAG, 0, -1):
                old = (slot - back) % PIPE_SLOTS
                out = value_product(out, a_buf[old][n], jnp.maximum(i - back, 0), p_buf[old][n])
            s = jnp.where(visible, s_buf[slot][n], -jnp.inf)
            _, alpha, p = softmax(s, jnp.max(s, axis=0, keepdims=True), n)
            out = value_product(out, alpha, i, p)
            results.append((out[:hd], jnp.mean(out[hd:], axis=0, keepdims=True)))
        put_scores(0, next_scores(0))
        put_scores(1, next_scores(1))
        finish(results)

    @pl.when(i == 0)
    def _():
        acc[...] = jnp.zeros(acc.shape, F32)
        for slot in range(PIPE_SLOTS - PV_LAG, PIPE_SLOTS):
            p_buf[slot][...] = jnp.zeros(p_buf[slot].shape, BF16)
            a_buf[slot][...] = jnp.ones(a_buf[slot].shape, F32)
        put_scores(0, scores(0))

    def rotation(jj, carry):
        for slot in range(PIPE_SLOTS):
            step(PIPE_SLOTS * jj + slot, slot)
        return carry

    trip = 2 * PIPE_SLOTS

    def two_rotations(u, carry):
        return rotation(2 * u + 1, rotation(2 * u, carry))

    lax.fori_loop(0, i // trip, two_rotations, 0)

    for rem in range(trip):
        @pl.when(i % trip == rem)
        def _(rem=rem):
            for k in range(rem):
                step(i - rem + k, k % PIPE_SLOTS, prefetch=k + 2 <= rem)
            last(rem % PIPE_SLOTS)


def _attend_scratch(n_maps, hd, tk, tq):
    return ([pltpu.VMEM((n_maps, 1, tq), F32),
             pltpu.VMEM((n_maps, hd + DENOM_ROWS, tq), F32)]
            + [pltpu.VMEM((n_maps, tk, tq), F32)] * PIPE_SLOTS
            + [pltpu.VMEM((n_maps, 1, tq), F32)] * PIPE_SLOTS
            + [pltpu.VMEM((n_maps, tk, tq), BF16)] * PIPE_SLOTS
            + [pltpu.VMEM((n_maps, 1, tq), F32)] * PIPE_SLOTS)


def _diff_kernel(lam_ref, qt_ref, k_ref, vt_ref, g_ref, o_ref, *scratch):
    i = pl.program_id(2)
    nq, t = qt_ref.shape[2], qt_ref.shape[-1]

    def both_maps(block):
        qt = qt_ref[0, 0, block]
        row = lax.broadcasted_iota(jnp.int32, qt.shape, 0)
        map1 = (row // ROPE_HALF) % 2 == 0
        zero = jnp.zeros_like(qt)
        return jnp.concatenate([jnp.where(map1, qt, zero), jnp.where(map1, zero, qt)], axis=1)

    q12 = both_maps(i)

    def scores(j):
        return (jnp.dot(k_ref[0, 0, j], q12, preferred_element_type=F32),)

    def next_scores(j):
        return (jnp.dot(k_ref[0, 0, j], both_maps(jnp.minimum(i + 1, nq - 1)), preferred_element_type=F32),)

    kc = lax.broadcasted_iota(jnp.int32, (t, 2 * t), 0) // CHUNK
    qc = (lax.broadcasted_iota(jnp.int32, (t, 2 * t), 1) % t) // CHUNK

    def finish(results):
        ((out, l),) = results
        o = out / l
        o = o[:, :t] - lam_ref[0] * o[:, t:]
        ms = jnp.mean(o * o, axis=0, keepdims=True)
        g = jnp.concatenate([g_ref[...]] * (t // LANES), axis=1)
        y = o * lax.rsqrt(ms + SUBLN_EPS) * g
        o_ref[0, 0] = y.T.astype(BF16)

    _attend(i, 1, scores, next_scores, vt_ref, kc <= qc, lambda j: 0.0, finish, *scratch)


def _diff_call(lam, qt, k, vt, g_rep):
    b, nh, nq, hd, t = qt.shape
    return pl.pallas_call(
        _diff_kernel,
        grid=(b, nh, nq),
        in_specs=[pl.BlockSpec(memory_space=pltpu.SMEM),
                  pl.BlockSpec((1, 1, nq, hd, t), lambda bi, h, i: (bi, h, 0, 0, 0)),
                  pl.BlockSpec((1, 1, nq, t, hd), lambda bi, h, i: (bi, h, 0, 0, 0)),
                  pl.BlockSpec((1, 1, nq, hd, t), lambda bi, h, i: (bi, h, 0, 0, 0)),
                  pl.BlockSpec((hd, LANES), lambda bi, h, i: (0, 0))],
        out_specs=pl.BlockSpec((1, 1, t, hd), lambda bi, h, i: (bi, h, i, 0)),
        out_shape=jax.ShapeDtypeStruct((b, nh, nq * t, hd), BF16),
        scratch_shapes=_attend_scratch(1, hd, t, 2 * t),
        compiler_params=_params("parallel", "parallel", "arbitrary"),
    )(lam, qt, k, vt, g_rep)


def _fox_kernel(tot_ref, qt_ref, k_ref, kext_ref, vt_ref, o_ref, *scratch):
    bi, h, i = pl.program_id(0), pl.program_id(1), pl.program_id(2)
    nq, t = qt_ref.shape[2], qt_ref.shape[-1]
    tot_base = (bi * N_HEADS + h) * nq

    def with_ones(block):
        qt = qt_ref[0, 0, block]
        row = lax.broadcasted_iota(jnp.int32, qt.shape, 0)
        return jnp.concatenate([qt, jnp.where(row < BIAS_TERMS, 1.0, 0.0).astype(BF16)], axis=0)

    q_cat = with_ones(i)

    def scores_of(q, j):
        k_cat = jnp.concatenate([k_ref[0, 0, j], kext_ref[0, 0, j]], axis=1)
        return (jnp.dot(k_cat, q, preferred_element_type=F32),)

    def scores(j):
        return scores_of(q_cat, j)

    def next_scores(j):
        return scores_of(with_ones(jnp.minimum(i + 1, nq - 1)), j)

    ki = lax.broadcasted_iota(jnp.int32, (t, t), 0)
    qi = lax.broadcasted_iota(jnp.int32, (t, t), 1)

    def finish(results):
        ((o, l),) = results
        o_ref[0, 0] = (o / l).T.astype(BF16)

    _attend(i, 1, scores, next_scores, vt_ref, ki <= qi, lambda j: tot_ref[tot_base + j], finish, *scratch)


def _fox_call(tot, qt, k, kext, vt):
    b, nh, nq, hd, t = qt.shape
    return pl.pallas_call(
        _fox_kernel,
        grid=(b, nh, nq),
        in_specs=[pl.BlockSpec(memory_space=pltpu.SMEM),
                  pl.BlockSpec((1, 1, nq, hd, t), lambda bi, h, i: (bi, h, 0, 0, 0)),
                  pl.BlockSpec((1, 1, nq, t, hd), lambda bi, h, i: (bi, h, 0, 0, 0)),
                  pl.BlockSpec((1, 1, nq, t, LANES), lambda bi, h, i: (bi, h, 0, 0, 0)),
                  pl.BlockSpec((1, 1, nq, hd, t), lambda bi, h, i: (bi, h, 0, 0, 0))],
        out_specs=pl.BlockSpec((1, 1, t, hd), lambda bi, h, i: (bi, h, i, 0)),
        out_shape=jax.ShapeDtypeStruct((b, nh, nq * t, hd), BF16),
        scratch_shapes=_attend_scratch(1, hd, t, t),
        compiler_params=_params("parallel", "parallel", "arbitrary"),
    )(tot, qt, k, kext, vt)


def _out_kernel(oa_ref, za_ref, ob_ref, zb_ref, ma_ref, mb_ref, x_ref, gate_ref,
                wa_ref, wb_ref, wo_ref, gf_ref, o_ref):
    def gated(o_ref_, z_ref_):
        o = jnp.concatenate([o_ref_[0, hh] for hh in range(N_HEADS)], axis=1)
        return (o.astype(F32) * z_ref_[0].astype(F32)).astype(BF16)

    ya = jnp.dot(gated(oa_ref, za_ref), wa_ref[...], preferred_element_type=F32)
    yb = jnp.dot(gated(ob_ref, zb_ref), wb_ref[...], preferred_element_type=F32)
    merged = ma_ref[0].astype(F32) * ya + mb_ref[0].astype(F32) * yb
    out = jnp.dot(merged.astype(BF16), wo_ref[...], preferred_element_type=F32)
    xn = x_ref[0] + gate_ref[0] * out
    ms = jnp.mean(xn * xn, axis=-1, keepdims=True)
    o_ref[0] = xn * lax.rsqrt(ms + NORM_EPS) * gf_ref[...]


def _out_call(oa, za, ob, zb, ma, mb, x, gate, wa, wb, wo, gf, bm):
    b, s, d = x.shape
    heads = pl.BlockSpec((1, N_HEADS, bm, HEAD_DIM), lambda bi, i: (bi, 0, i, 0))
    row = pl.BlockSpec((1, bm, d), lambda bi, i: (bi, i, 0))
    wspec = pl.BlockSpec((d, d), lambda bi, i: (0, 0))
    return pl.pallas_call(
        _out_kernel,
        grid=(b, s // bm),
        in_specs=[heads, row, heads, row, row, row, row,
                  pl.BlockSpec((1, 1, d), lambda bi, i: (bi, 0, 0)),
                  wspec, wspec, wspec,
                  pl.BlockSpec((1, d), lambda bi, i: (0, 0))],
        out_specs=row,
        out_shape=jax.ShapeDtypeStruct((b, s, d), F32),
        compiler_params=_params("parallel", "parallel"),
    )(oa, za, ob, zb, ma, mb, x, gate, wa, wb, wo, gf)


def kernel(x, c, positions, w_ada, b_ada, g_norm, w_in, b_forget, lambda_q1, lambda_k1, lambda_q2,
           lambda_k2, g_subln, w_branch_a, w_branch_b, w_out, g_final):
    b, s, d = x.shape
    assert d == D_MODEL and s % ATT_TILE == 0 and b <= SUBLANES
    assert w_ada.shape[0] == 1, "single-layer block"
    t = ATT_TILE
    nq = s // t
    n_main = 8 * D_MODEL

    c_pad = jnp.zeros((SUBLANES, d), F32).at[:b].set(c)
    w_t = jnp.swapaxes(w_in, 1, 2)
    w_qk = w_t[:, :2 * D_MODEL].reshape(2 * N_HEADS, 2, 2, ROPE_HALF, d).transpose(0, 2, 1, 3, 4)
    w_qk = w_qk.reshape(1, 2 * D_MODEL, d)
    w_merge = w_t[:, n_main + N_HEADS:]
    wf_pad = jnp.zeros((LANES, d), F32).at[:N_HEADS].set(w_t[0, n_main:n_main + N_HEADS]).astype(BF16)
    bf_pad = jnp.zeros((1, LANES), F32).at[0, :N_HEADS].set(b_forget[0])
    inv_freq = ROPE_THETA ** (-jnp.arange(ROPE_HALF, dtype=F32) / ROPE_HALF)
    invf = jnp.tile(inv_freq, LANES // ROPE_HALF)[None, :]
    sgn = jnp.concatenate([-jnp.ones((HEAD_DIM // 2,), F32), jnp.ones((HEAD_DIM // 2,), F32)])[None, :]
    pos_b = jnp.broadcast_to(positions.astype(F32)[:, :, None], (b, s, LANES))
    g_rep = jnp.broadcast_to((g_subln[0] * (1.0 - LAMBDA_INIT))[:, None], (HEAD_DIM, LANES))

    mod, lam_tile = _mod_call(c_pad, w_ada, b_ada, lambda_q1, lambda_k1, lambda_q2, lambda_k2)
    shift = mod[:b, None, 0:d]
    scale = mod[:b, None, d:2 * d]
    gate = mod[:b, None, 2 * d:3 * d]
    lam = lam_tile[0, :1]

    h, kext, tot_tile = _prenorm_cum_call(x, g_norm, scale, shift, wf_pad, bf_pad, t)
    tables = _rope_table_call(pos_b, invf, sgn, t)

    pm = PROJ_ROWS if s % PROJ_ROWS == 0 else t
    proj = functools.partial(_proj_call, h)
    qa_t = proj(w_qk, 0, tables, pm, rope=True, scale=LOG2E * DA_QK_DIM ** -0.5, layout="heads_t")
    ka = proj(w_qk, 1, tables, pm, rope=True, layout="heads")
    va_t = proj(w_t, 2, None, pm, layout="heads_t")
    za = proj(w_t, 3, None, pm, act="silu")
    qb_t = proj(w_t, 4, None, pm, scale=LOG2E * HEAD_DIM ** -0.5, layout="heads_t")
    kb = proj(w_t, 5, None, pm, layout="heads")
    vb_t = proj(w_t, 6, None, pm, layout="heads_t")
    zb = proj(w_t, 7, None, pm, act="silu")
    ma = proj(w_merge, 0, None, pm, act="sigmoid")
    mb = proj(w_merge, 1, None, pm, act="sigmoid")

    tot = tot_tile[:, :, 0, :N_HEADS].transpose(0, 2, 1).reshape(-1)

    def tiles(a):
        return a.reshape(b, N_HEADS, nq, t, a.shape[-1])

    oa = _diff_call(lam, qa_t, tiles(ka), va_t, g_rep)
    ob = _fox_call(tot, qb_t, tiles(kb), tiles(kext), vb_t)

    return _out_call(oa, za, ob, zb, ma, mb, x, gate,
                     w_branch_a[0].astype(BF16), w_branch_b[0].astype(BF16), w_out[0].astype(BF16),
                     g_final[None, :], t)
```

```python
import functools
import math

import jax
import jax.numpy as jnp
import numpy as np
from jax import lax
from jax.experimental import pallas as pl
from jax.experimental.pallas import tpu as pltpu

F32 = jnp.float32
BF16 = jnp.bfloat16

D_MODEL = 1024
N_HEADS = 8
HEAD_DIM = 128
DA_QK_DIM = 64
ROPE_HALF = DA_QK_DIM // 2
CHUNK = 64
ROPE_THETA = 10000.0
NORM_EPS = 1e-6
SUBLN_EPS = 1e-5
LAMBDA_INIT = 0.8 - 0.6 * math.exp(-0.3 * 0)
LOG2E = math.log2(math.e)

LANES = 128
SUBLANES = 8
ATT_TILE = 512
PROJ_ROWS = 1024
BIAS_TERMS = 3
VMEM_LIMIT = 48 * 1024 * 1024
NEG_BIG = -1e30


def _params(*sem):
    return pltpu.CompilerParams(dimension_semantics=sem, vmem_limit_bytes=VMEM_LIMIT)


def _mod_kernel(c_ref, w_ref, b_ref, lq1_ref, lk1_ref, lq2_ref, lk2_ref, mod_ref, lam_ref):
    c = c_ref[...]
    c_act = c * jax.nn.sigmoid(c)
    mod_ref[...] = jnp.dot(c_act, w_ref[0], precision=lax.Precision.HIGHEST,
                           preferred_element_type=F32) + b_ref[...]
    s1 = jnp.sum(lq1_ref[...] * lk1_ref[...], axis=-1, keepdims=True)
    s2 = jnp.sum(lq2_ref[...] * lk2_ref[...], axis=-1, keepdims=True)
    lam = jnp.exp(s1) - jnp.exp(s2) + LAMBDA_INIT
    lam_ref[...] = jnp.broadcast_to(lam, lam_ref.shape)


def _mod_call(c_pad, w_ada, b_ada, lq1, lk1, lq2, lk2):
    d = D_MODEL
    lam_spec = pl.BlockSpec((1, DA_QK_DIM), lambda j: (0, 0))
    return pl.pallas_call(
        _mod_kernel,
        grid=(3,),
        in_specs=[pl.BlockSpec((SUBLANES, d), lambda j: (0, 0)),
                  pl.BlockSpec((1, d, d), lambda j: (0, 0, j)),
                  pl.BlockSpec((1, d), lambda j: (0, j)),
                  lam_spec, lam_spec, lam_spec, lam_spec],
        out_specs=[pl.BlockSpec((SUBLANES, d), lambda j: (0, j)),
                   pl.BlockSpec((SUBLANES, LANES), lambda j: (0, 0))],
        out_shape=[jax.ShapeDtypeStruct((SUBLANES, 3 * d), F32),
                   jax.ShapeDtypeStruct((SUBLANES, LANES), F32)],
        compiler_params=_params("arbitrary"),
    )(c_pad, w_ada, b_ada, lq1, lk1, lq2, lk2)


def _rope_table_kernel(pos_ref, invf_ref, sgn_ref, cos_ref, sin_ref):
    ang = pos_ref[0] * invf_ref[...]
    cos_ref[0] = jnp.cos(ang)
    sin_ref[0] = jnp.sin(ang) * sgn_ref[...]


def _rope_table_call(pos_b, invf, sgn, bm):
    b, s, _ = pos_b.shape
    row = pl.BlockSpec((1, bm, LANES), lambda bi, i: (bi, i, 0))
    vec = pl.BlockSpec((1, LANES), lambda bi, i: (0, 0))
    shape = jax.ShapeDtypeStruct((b, s, LANES), F32)
    return pl.pallas_call(
        _rope_table_kernel,
        grid=(b, s // bm),
        in_specs=[row, vec, vec],
        out_specs=[row, row],
        out_shape=[shape, shape],
        compiler_params=_params("parallel", "parallel"),
    )(pos_b, invf, sgn)


def _proj_kernel(*refs, rope, scale, act, layout):
    if rope:
        h_ref, w_ref, cos_ref, sin_ref, o_ref = refs
    else:
        h_ref, w_ref, o_ref = refs
    acc = lax.dot_general(h_ref[0], w_ref[0].astype(BF16), (((1,), (1,)), ((), ())),
                          preferred_element_type=F32)
    if act == "silu":
        acc = acc * jax.nn.sigmoid(acc)
    elif act == "sigmoid":
        acc = jax.nn.sigmoid(acc)
    if layout == "nat":
        o_ref[0] = acc.astype(BF16)
        return
    if rope:
        cos = cos_ref[0]
        sin = sin_ref[0]
    for hh in range(N_HEADS):
        t = acc[:, hh * HEAD_DIM:(hh + 1) * HEAD_DIM]
        if rope:
            t = t * cos + pltpu.roll(t, HEAD_DIM // 2, 1) * sin
        if scale != 1.0:
            t = t * scale
        if layout == "heads":
            o_ref[0, hh] = t.astype(BF16)
        else:
            tile = o_ref.shape[-1]
            for part in range(o_ref.shape[2]):
                o_ref[0, hh, part] = t[part * tile:(part + 1) * tile].T.astype(BF16)


def _proj_call(h, w_all, col_block, tables, bm, *, rope=False, scale=1.0, act=None, layout="nat"):
    b, s, d = h.shape
    n = D_MODEL
    in_specs = [pl.BlockSpec((1, bm, d), lambda bi, i: (bi, i, 0)),
                pl.BlockSpec((1, n, d), lambda bi, i: (0, col_block, 0))]
    args = [h, w_all]
    if rope:
        tab = pl.BlockSpec((1, bm, LANES), lambda bi, i: (bi, i, 0))
        in_specs += [tab, tab]
        args += list(tables)
    if layout == "nat":
        out_spec = pl.BlockSpec((1, bm, n), lambda bi, i: (bi, i, 0))
        out_shape = jax.ShapeDtypeStruct((b, s, n), BF16)
    elif layout == "heads":
        out_spec = pl.BlockSpec((1, N_HEADS, bm, HEAD_DIM), lambda bi, i: (bi, 0, i, 0))
        out_shape = jax.ShapeDtypeStruct((b, N_HEADS, s, HEAD_DIM), BF16)
    else:
        parts = bm // ATT_TILE
        out_spec = pl.BlockSpec((1, N_HEADS, parts, HEAD_DIM, ATT_TILE), lambda bi, i: (bi, 0, i, 0, 0))
        out_shape = jax.ShapeDtypeStruct((b, N_HEADS, s // ATT_TILE, HEAD_DIM, ATT_TILE), BF16)
    return pl.pallas_call(
        functools.partial(_proj_kernel, rope=rope, scale=scale, act=act, layout=layout),
        grid=(b, s // bm),
        in_specs=in_specs,
        out_specs=out_spec,
        out_shape=out_shape,
        compiler_params=_params("parallel", "parallel"),
    )(*args)


def _split3(v):
    hi = v.astype(BF16).astype(F32)
    r1 = v - hi
    mid = r1.astype(BF16).astype(F32)
    return hi, mid, r1 - mid


def _prenorm_cum_kernel(x_ref, g_ref, scale_ref, shift_ref, wf_ref, bf_ref, sel_ref, h_ref, kext_ref, tot_ref):
    bm = x_ref.shape[1]
    x = x_ref[0]
    ms = jnp.mean(x * x, axis=-1, keepdims=True)
    y = x * lax.rsqrt(ms + NORM_EPS) * g_ref[...]
    h = (y * (1.0 + scale_ref[0]) + shift_ref[0]).astype(BF16)
    h_ref[0] = h
    logit = lax.dot_general(h, wf_ref[...], (((1,), (1,)), ((), ())),
                            preferred_element_type=F32) + bf_ref[...]
    log_f = jnp.minimum(logit, 0.0) - jnp.log(1.0 + jnp.exp(-jnp.abs(logit)))
    r = lax.broadcasted_iota(jnp.int32, (bm, bm), 0)
    c = lax.broadcasted_iota(jnp.int32, (bm, bm), 1)
    tri = jnp.where(c <= r, 1.0, 0.0).astype(BF16)
    terms = jnp.concatenate([part.astype(BF16) for part in _split3(log_f)], axis=1)
    sums = jnp.dot(tri, terms, preferred_element_type=F32)
    cum = sum(sums[:, n * LANES:(n + 1) * LANES] for n in range(BIAS_TERMS))
    bias = cum * (-LOG2E)
    tot_ref[0, 0] = jnp.broadcast_to(-bias[bm - 1:bm, :], tot_ref.shape[2:])
    parts = jnp.concatenate([part.astype(BF16) for part in _split3(bias)], axis=1)
    ext = jnp.dot(parts, sel_ref[...], preferred_element_type=F32)
    for hh in range(N_HEADS):
        kext_ref[0, hh] = ext[:, hh * LANES:(hh + 1) * LANES].astype(BF16)


def _bias_selection():
    sel = np.zeros((BIAS_TERMS, LANES, N_HEADS, LANES), np.float32)
    for n in range(BIAS_TERMS):
        for hh in range(N_HEADS):
            sel[n, hh, hh, n] = 1.0
    return sel.reshape(BIAS_TERMS * LANES, N_HEADS * LANES)


def _prenorm_cum_call(x, g, scale, shift, wf_pad, bf_pad, bm):
    b, s, d = x.shape
    sel = jnp.asarray(_bias_selection(), BF16)
    row = pl.BlockSpec((1, bm, d), lambda bi, i: (bi, i, 0))
    vec = pl.BlockSpec((1, 1, d), lambda bi, i: (bi, 0, 0))
    return pl.pallas_call(
        _prenorm_cum_kernel,
        grid=(b, s // bm),
        in_specs=[row, pl.BlockSpec((1, d), lambda bi, i: (0, 0)), vec, vec,
                  pl.BlockSpec((LANES, d), lambda bi, i: (0, 0)),
                  pl.BlockSpec((1, LANES), lambda bi, i: (0, 0)),
                  pl.BlockSpec(sel.shape, lambda bi, i: (0, 0))],
        out_specs=[row,
                   pl.BlockSpec((1, N_HEADS, bm, LANES), lambda bi, i: (bi, 0, i, 0)),
                   pl.BlockSpec((1, 1, SUBLANES, LANES), lambda bi, i: (bi, i, 0, 0))],
        out_shape=[jax.ShapeDtypeStruct((b, s, d), BF16),
                   jax.ShapeDtypeStruct((b, N_HEADS, s, LANES), BF16),
                   jax.ShapeDtypeStruct((b, s // bm, SUBLANES, LANES), F32)],
        compiler_params=_params("parallel", "parallel"),
    )(x, g, scale, shift, wf_pad, bf_pad, sel)


PIPE_SLOTS = 3
PV_LAG = 2
DENOM_ROWS = 16


def _attend(i, n_maps, scores, next_scores, vt_ref, visible, frame_shift, finish, m, acc, *bufs):
    s_buf, c_buf, p_buf, a_buf = (bufs[k * PIPE_SLOTS:(k + 1) * PIPE_SLOTS] for k in range(4))
    hd = acc.shape[1] - DENOM_ROWS
    ones = jnp.ones((DENOM_ROWS, vt_ref.shape[-1]), BF16)
    m[...] = jnp.full(m.shape, NEG_BIG, F32)

    def put_scores(slot, tiles):
        for n, s in enumerate(tiles):
            s_buf[slot][n] = s
            c_buf[slot][n] = jnp.max(s, axis=0, keepdims=True)

    def softmax(s, s_max, n):
        m_prev = m[n]
        m_new = jnp.maximum(m_prev, s_max)
        alpha = jnp.exp2(m_prev - m_new)
        p = jnp.exp2(s - m_new).astype(BF16)
        return m_new, alpha, p

    def value_product(prev, alpha, j, p):
        vt_ones = jnp.concatenate([vt_ref[0, 0, j], ones], axis=0)
        return alpha * prev + jnp.dot(vt_ones, p, preferred_element_type=F32)

    def step(j, slot, prefetch=True):
        if prefetch:
            put_scores((slot + 2) % PIPE_SLOTS, scores(jnp.minimum(j + 2, i)))
        old = (slot - PV_LAG) % PIPE_SLOTS
        for n in range(n_maps):
            acc[n] = value_product(acc[n], a_buf[old][n], jnp.maximum(j - PV_LAG, 0), p_buf[old][n])
            m_new, alpha, p = softmax(s_buf[slot][n], c_buf[slot][n], n)
            p_buf[slot][n] = p
            a_buf[slot][n] = alpha
            m[n] = m_new + frame_shift(j)

    def last(slot):
        results = []
        for n in range(n_maps):
            out = acc[n]
            for back in range(PV_LAG, 0, -1):
                old = (slot - back) % PIPE_SLOTS
                out = value_product(out, a_buf[old][n], jnp.maximum(i - back, 0), p_buf[old][n])
            s = jnp.where(visible(), s_buf[slot][n], -jnp.inf)
            _, alpha, p = softmax(s, jnp.max(s, axis=0, keepdims=True), n)
            out = value_product(out, alpha, i, p)
            results.append((out[:hd], jnp.mean(out[hd:], axis=0, keepdims=True)))
        put_scores(0, next_scores(0))
        put_scores(1, next_scores(1))
        finish(results)

    @pl.when(i == 0)
    def _():
        acc[...] = jnp.zeros(acc.shape, F32)
        for slot in range(PIPE_SLOTS - PV_LAG, PIPE_SLOTS):
            p_buf[slot][...] = jnp.zeros(p_buf[slot].shape, BF16)
            a_buf[slot][...] = jnp.ones(a_buf[slot].shape, F32)
        put_scores(0, scores(0))

    def rotation(jj, carry):
        for slot in range(PIPE_SLOTS):
            step(PIPE_SLOTS * jj + slot, slot)
        return carry

    trip = 2 * PIPE_SLOTS

    def two_rotations(u, carry):
        return rotation(2 * u + 1, rotation(2 * u, carry))

    lax.fori_loop(0, i // trip, two_rotations, 0)

    for rem in range(trip):
        @pl.when(i % trip == rem)
        def _(rem=rem):
            for k in range(rem):
                step(i - rem + k, k % PIPE_SLOTS, prefetch=k + 2 <= rem)
            last(rem % PIPE_SLOTS)


def _attend_scratch(n_maps, hd, tk, tq):
    return ([pltpu.VMEM((n_maps, 1, tq), F32),
             pltpu.VMEM((n_maps, hd + DENOM_ROWS, tq), F32)]
            + [pltpu.VMEM((n_maps, tk, tq), F32)] * PIPE_SLOTS
            + [pltpu.VMEM((n_maps, 1, tq), F32)] * PIPE_SLOTS
            + [pltpu.VMEM((n_maps, tk, tq), BF16)] * PIPE_SLOTS
            + [pltpu.VMEM((n_maps, 1, tq), F32)] * PIPE_SLOTS)


def _diff_kernel(lam_ref, qt_ref, k_ref, vt_ref, g_ref, o_ref, *scratch):
    i = pl.program_id(2)
    nq, t = qt_ref.shape[2], qt_ref.shape[-1]

    def both_maps(block):
        qt = qt_ref[0, 0, block]
        row = lax.broadcasted_iota(jnp.int32, qt.shape, 0)
        map1 = (row // ROPE_HALF) % 2 == 0
        zero = jnp.zeros_like(qt)
        return jnp.concatenate([jnp.where(map1, qt, zero), jnp.where(map1, zero, qt)], axis=1)

    q12 = both_maps(i)

    def scores(j):
        return (jnp.dot(k_ref[0, 0, j], q12, preferred_element_type=F32),)

    def next_scores(j):
        return (jnp.dot(k_ref[0, 0, j], both_maps(jnp.minimum(i + 1, nq - 1)), preferred_element_type=F32),)

    def visible():
        kc = lax.broadcasted_iota(jnp.int32, (t, 2 * t), 0) // CHUNK
        qc = (lax.broadcasted_iota(jnp.int32, (t, 2 * t), 1) % t) // CHUNK
        return kc <= qc

    def finish(results):
        ((out, l),) = results
        o = out / l
        o = o[:, :t] - lam_ref[0] * o[:, t:]
        ms = jnp.mean(o * o, axis=0, keepdims=True)
        g = jnp.concatenate([g_ref[...]] * (t // LANES), axis=1)
        y = o * lax.rsqrt(ms + SUBLN_EPS) * g
        o_ref[0, 0] = y.T.astype(BF16)

    _attend(i, 1, scores, next_scores, vt_ref, visible, lambda j: 0.0, finish, *scratch)


def _diff_call(lam, qt, k, vt, g_rep):
    b, nh, nq, hd, t = qt.shape
    return pl.pallas_call(
        _diff_kernel,
        grid=(b, nh, nq),
        in_specs=[pl.BlockSpec(memory_space=pltpu.SMEM),
                  pl.BlockSpec((1, 1, nq, hd, t), lambda bi, h, i: (bi, h, 0, 0, 0)),
                  pl.BlockSpec((1, 1, nq, t, hd), lambda bi, h, i: (bi, h, 0, 0, 0)),
                  pl.BlockSpec((1, 1, nq, hd, t), lambda bi, h, i: (bi, h, 0, 0, 0)),
                  pl.BlockSpec((hd, LANES), lambda bi, h, i: (0, 0))],
        out_specs=pl.BlockSpec((1, 1, t, hd), lambda bi, h, i: (bi, h, i, 0)),
        out_shape=jax.ShapeDtypeStruct((b, nh, nq * t, hd), BF16),
        scratch_shapes=_attend_scratch(1, hd, t, 2 * t),
        compiler_params=_params("parallel", "parallel", "arbitrary"),
    )(lam, qt, k, vt, g_rep)


def _fox_kernel(tot_ref, qt_ref, k_ref, kext_ref, vt_ref, o_ref, *scratch):
    bi, h, i = pl.program_id(0), pl.program_id(1), pl.program_id(2)
    nq, t = qt_ref.shape[2], qt_ref.shape[-1]
    tot_base = (bi * N_HEADS + h) * nq

    def with_ones(block):
        qt = qt_ref[0, 0, block]
        row = lax.broadcasted_iota(jnp.int32, qt.shape, 0)
        return jnp.concatenate([qt, jnp.where(row < BIAS_TERMS, 1.0, 0.0).astype(BF16)], axis=0)

    q_cat = with_ones(i)

    def scores_of(q, j):
        k_cat = jnp.concatenate([k_ref[0, 0, j], kext_ref[0, 0, j]], axis=1)
        return (jnp.dot(k_cat, q, preferred_element_type=F32),)

    def scores(j):
        return scores_of(q_cat, j)

    def next_scores(j):
        return scores_of(with_ones(jnp.minimum(i + 1, nq - 1)), j)

    def visible():
        return lax.broadcasted_iota(jnp.int32, (t, t), 0) <= lax.broadcasted_iota(jnp.int32, (t, t), 1)

    def finish(results):
        ((o, l),) = results
        o_ref[0, 0] = (o / l).T.astype(BF16)

    _attend(i, 1, scores, next_scores, vt_ref, visible, lambda j: tot_ref[tot_base + j], finish, *scratch)


def _fox_call(tot, qt, k, kext, vt):
    b, nh, nq, hd, t = qt.shape
    return pl.pallas_call(
        _fox_kernel,
        grid=(b, nh, nq),
        in_specs=[pl.BlockSpec(memory_space=pltpu.SMEM),
                  pl.BlockSpec((1, 1, nq, hd, t), lambda bi, h, i: (bi, h, 0, 0, 0)),
                  pl.BlockSpec((1, 1, nq, t, hd), lambda bi, h, i: (bi, h, 0, 0, 0)),
                  pl.BlockSpec((1, 1, nq, t, LANES), lambda bi, h, i: (bi, h, 0, 0, 0)),
                  pl.BlockSpec((1, 1, nq, hd, t), lambda bi, h, i: (bi, h, 0, 0, 0))],
        out_specs=pl.BlockSpec((1, 1, t, hd), lambda bi, h, i: (bi, h, i, 0)),
        out_shape=jax.ShapeDtypeStruct((b, nh, nq * t, hd), BF16),
        scratch_shapes=_attend_scratch(1, hd, t, t),
        compiler_params=_params("parallel", "parallel", "arbitrary"),
    )(tot, qt, k, kext, vt)


def _out_kernel(oa_ref, za_ref, ob_ref, zb_ref, ma_ref, mb_ref, x_ref, gate_ref,
                wa_ref, wb_ref, wo_ref, gf_ref, o_ref):
    def gated(o_ref_, z_ref_):
        o = jnp.concatenate([o_ref_[0, hh] for hh in range(N_HEADS)], axis=1)
        return (o.astype(F32) * z_ref_[0].astype(F32)).astype(BF16)

    ya = jnp.dot(gated(oa_ref, za_ref), wa_ref[...], preferred_element_type=F32)
    yb = jnp.dot(gated(ob_ref, zb_ref), wb_ref[...], preferred_element_type=F32)
    merged = ma_ref[0].astype(F32) * ya + mb_ref[0].astype(F32) * yb
    out = jnp.dot(merged.astype(BF16), wo_ref[...], preferred_element_type=F32)
    xn = x_ref[0] + gate_ref[0] * out
    ms = jnp.mean(xn * xn, axis=-1, keepdims=True)
    o_ref[0] = xn * lax.rsqrt(ms + NORM_EPS) * gf_ref[...]


def _out_call(oa, za, ob, zb, ma, mb, x, gate, wa, wb, wo, gf, bm):
    b, s, d = x.shape
    heads = pl.BlockSpec((1, N_HEADS, bm, HEAD_DIM), lambda bi, i: (bi, 0, i, 0))
    row = pl.BlockSpec((1, bm, d), lambda bi, i: (bi, i, 0))
    wspec = pl.BlockSpec((d, d), lambda bi, i: (0, 0))
    return pl.pallas_call(
        _out_kernel,
        grid=(b, s // bm),
        in_specs=[heads, row, heads, row, row, row, row,
                  pl.BlockSpec((1, 1, d), lambda bi, i: (bi, 0, 0)),
                  wspec, wspec, wspec,
                  pl.BlockSpec((1, d), lambda bi, i: (0, 0))],
        out_specs=row,
        out_shape=jax.ShapeDtypeStruct((b, s, d), F32),
        compiler_params=_params("parallel", "parallel"),
    )(oa, za, ob, zb, ma, mb, x, gate, wa, wb, wo, gf)


def kernel(x, c, positions, w_ada, b_ada, g_norm, w_in, b_forget, lambda_q1, lambda_k1, lambda_q2,
           lambda_k2, g_subln, w_branch_a, w_branch_b, w_out, g_final):
    b, s, d = x.shape
    assert d == D_MODEL and s % ATT_TILE == 0 and b <= SUBLANES
    assert w_ada.shape[0] == 1, "single-layer block"
    t = ATT_TILE
    nq = s // t
    n_main = 8 * D_MODEL

    c_pad = jnp.zeros((SUBLANES, d), F32).at[:b].set(c)
    w_t = jnp.swapaxes(w_in, 1, 2)
    w_qk = w_t[:, :2 * D_MODEL].reshape(2 * N_HEADS, 2, 2, ROPE_HALF, d).transpose(0, 2, 1, 3, 4)
    w_qk = w_qk.reshape(1, 2 * D_MODEL, d)
    w_merge = w_t[:, n_main + N_HEADS:]
    wf_pad = jnp.zeros((LANES, d), F32).at[:N_HEADS].set(w_t[0, n_main:n_main + N_HEADS]).astype(BF16)
    bf_pad = jnp.zeros((1, LANES), F32).at[0, :N_HEADS].set(b_forget[0])
    inv_freq = ROPE_THETA ** (-jnp.arange(ROPE_HALF, dtype=F32) / ROPE_HALF)
    invf = jnp.tile(inv_freq, LANES // ROPE_HALF)[None, :]
    sgn = jnp.concatenate([-jnp.ones((HEAD_DIM // 2,), F32), jnp.ones((HEAD_DIM // 2,), F32)])[None, :]
    pos_b = jnp.broadcast_to(positions.astype(F32)[:, :, None], (b, s, LANES))
    g_rep = jnp.broadcast_to((g_subln[0] * (1.0 - LAMBDA_INIT))[:, None], (HEAD_DIM, LANES))

    mod, lam_tile = _mod_call(c_pad, w_ada, b_ada, lambda_q1, lambda_k1, lambda_q2, lambda_k2)
    shift = mod[:b, None, 0:d]
    scale = mod[:b, None, d:2 * d]
    gate = mod[:b, None, 2 * d:3 * d]
    lam = lam_tile[0, :1]

    h, kext, tot_tile = _prenorm_cum_call(x, g_norm, scale, shift, wf_pad, bf_pad, t)
    tables = _rope_table_call(pos_b, invf, sgn, t)

    pm = PROJ_ROWS if s % PROJ_ROWS == 0 else t
    proj = functools.partial(_proj_call, h)
    qa_t = proj(w_qk, 0, tables, pm, rope=True, scale=LOG2E * DA_QK_DIM ** -0.5, layout="heads_t")
    ka = proj(w_qk, 1, tables, pm, rope=True, layout="heads")
    va_t = proj(w_t, 2, None, pm, layout="heads_t")
    za = proj(w_t, 3, None, pm, act="silu")
    qb_t = proj(w_t, 4, None, pm, scale=LOG2E * HEAD_DIM ** -0.5, layout="heads_t")
    kb = proj(w_t, 5, None, pm, layout="heads")
    vb_t = proj(w_t, 6, None, pm, layout="heads_t")
    zb = proj(w_t, 7, None, pm, act="silu")
    ma = proj(w_merge, 0, None, pm, act="sigmoid")
    mb = proj(w_merge, 1, None, pm, act="sigmoid")

    tot = tot_tile[:, :, 0, :N_HEADS].transpose(0, 2, 1).reshape(-1)

    def tiles(a):
        return a.reshape(b, N_HEADS, nq, t, a.shape[-1])

    oa = _diff_call(lam, qa_t, tiles(ka), va_t, g_rep)
    ob = _fox_call(tot, qb_t, tiles(kb), tiles(kext), vb_t)

    return _out_call(oa, za, ob, zb, ma, mb, x, gate,
                     w_branch_a[0].astype(BF16), w_branch_b[0].astype(BF16), w_out[0].astype(BF16),
                     g_final[None, :], t)
```

```python
import functools
import math

import jax
import jax.numpy as jnp
import numpy as np
from jax import lax
from jax.experimental import pallas as pl
from jax.experimental.pallas import tpu as pltpu

F32 = jnp.float32
BF16 = jnp.bfloat16

D_MODEL = 1024
N_HEADS = 8
HEAD_DIM = 128
DA_QK_DIM = 64
ROPE_HALF = DA_QK_DIM // 2
CHUNK = 64
ROPE_THETA = 10000.0
NORM_EPS = 1e-6
SUBLN_EPS = 1e-5
LAMBDA_INIT = 0.8 - 0.6 * math.exp(-0.3 * 0)
LOG2E = math.log2(math.e)

LANES = 128
SUBLANES = 8
ATT_TILE = 512
PROJ_ROWS = 1024
BIAS_TERMS = 3
VMEM_LIMIT = 48 * 1024 * 1024
NEG_BIG = -1e30


def _params(*sem):
    return pltpu.CompilerParams(dimension_semantics=sem, vmem_limit_bytes=VMEM_LIMIT)


def _mod_kernel(c_ref, w_ref, b_ref, lq1_ref, lk1_ref, lq2_ref, lk2_ref, mod_ref, lam_ref):
    c = c_ref[...]
    c_act = c * jax.nn.sigmoid(c)
    mod_ref[...] = jnp.dot(c_act, w_ref[0], precision=lax.Precision.HIGHEST,
                           preferred_element_type=F32) + b_ref[...]
    s1 = jnp.sum(lq1_ref[...] * lk1_ref[...], axis=-1, keepdims=True)
    s2 = jnp.sum(lq2_ref[...] * lk2_ref[...], axis=-1, keepdims=True)
    lam = jnp.exp(s1) - jnp.exp(s2) + LAMBDA_INIT
    lam_ref[...] = jnp.broadcast_to(lam, lam_ref.shape)


def _mod_call(c_pad, w_ada, b_ada, lq1, lk1, lq2, lk2):
    d = D_MODEL
    lam_spec = pl.BlockSpec((1, DA_QK_DIM), lambda j: (0, 0))
    return pl.pallas_call(
        _mod_kernel,
        grid=(3,),
        in_specs=[pl.BlockSpec((SUBLANES, d), lambda j: (0, 0)),
                  pl.BlockSpec((1, d, d), lambda j: (0, 0, j)),
                  pl.BlockSpec((1, d), lambda j: (0, j)),
                  lam_spec, lam_spec, lam_spec, lam_spec],
        out_specs=[pl.BlockSpec((SUBLANES, d), lambda j: (0, j)),
                   pl.BlockSpec((SUBLANES, LANES), lambda j: (0, 0))],
        out_shape=[jax.ShapeDtypeStruct((SUBLANES, 3 * d), F32),
                   jax.ShapeDtypeStruct((SUBLANES, LANES), F32)],
        compiler_params=_params("arbitrary"),
    )(c_pad, w_ada, b_ada, lq1, lk1, lq2, lk2)


def _rope_table_kernel(pos_ref, invf_ref, sgn_ref, cos_ref, sin_ref):
    ang = pos_ref[0] * invf_ref[...]
    cos_ref[0] = jnp.cos(ang)
    sin_ref[0] = jnp.sin(ang) * sgn_ref[...]


def _rope_table_call(pos_b, invf, sgn, bm):
    b, s, _ = pos_b.shape
    row = pl.BlockSpec((1, bm, LANES), lambda bi, i: (bi, i, 0))
    vec = pl.BlockSpec((1, LANES), lambda bi, i: (0, 0))
    shape = jax.ShapeDtypeStruct((b, s, LANES), F32)
    return pl.pallas_call(
        _rope_table_kernel,
        grid=(b, s // bm),
        in_specs=[row, vec, vec],
        out_specs=[row, row],
        out_shape=[shape, shape],
        compiler_params=_params("parallel", "parallel"),
    )(pos_b, invf, sgn)


def _proj_kernel(*refs, rope, scale, act, layout):
    if rope:
        h_ref, w_ref, cos_ref, sin_ref, o_ref = refs
    else:
        h_ref, w_ref, o_ref = refs
    acc = lax.dot_general(h_ref[0], w_ref[0].astype(BF16), (((1,), (1,)), ((), ())),
                          preferred_element_type=F32)
    if act in ("silu", "sigmoid"):
        gate = 0.5 * jnp.tanh(0.5 * acc) + 0.5
        acc = acc * gate if act == "silu" else gate
    if layout == "nat":
        o_ref[0] = acc.astype(BF16)
        return
    if rope:
        cos = cos_ref[0]
        sin = sin_ref[0]
    for hh in range(N_HEADS):
        t = acc[:, hh * HEAD_DIM:(hh + 1) * HEAD_DIM]
        if rope:
            t = t * cos + pltpu.roll(t, HEAD_DIM // 2, 1) * sin
        if scale != 1.0:
            t = t * scale
        if layout == "heads":
            o_ref[0, hh] = t.astype(BF16)
        else:
            tile = o_ref.shape[-1]
            for part in range(o_ref.shape[2]):
                o_ref[0, hh, part] = t[part * tile:(part + 1) * tile].T.astype(BF16)


def _proj_call(h, w_all, col_block, tables, bm, *, rope=False, scale=1.0, act=None, layout="nat"):
    b, s, d = h.shape
    n = D_MODEL
    in_specs = [pl.BlockSpec((1, bm, d), lambda bi, i: (bi, i, 0)),
                pl.BlockSpec((1, n, d), lambda bi, i: (0, col_block, 0))]
    args = [h, w_all]
    if rope:
        tab = pl.BlockSpec((1, bm, LANES), lambda bi, i: (bi, i, 0))
        in_specs += [tab, tab]
        args += list(tables)
    if layout == "nat":
        out_spec = pl.BlockSpec((1, bm, n), lambda bi, i: (bi, i, 0))
        out_shape = jax.ShapeDtypeStruct((b, s, n), BF16)
    elif layout == "heads":
        out_spec = pl.BlockSpec((1, N_HEADS, bm, HEAD_DIM), lambda bi, i: (bi, 0, i, 0))
        out_shape = jax.ShapeDtypeStruct((b, N_HEADS, s, HEAD_DIM), BF16)
    else:
        parts = bm // ATT_TILE
        out_spec = pl.BlockSpec((1, N_HEADS, parts, HEAD_DIM, ATT_TILE), lambda bi, i: (bi, 0, i, 0, 0))
        out_shape = jax.ShapeDtypeStruct((b, N_HEADS, s // ATT_TILE, HEAD_DIM, ATT_TILE), BF16)
    return pl.pallas_call(
        functools.partial(_proj_kernel, rope=rope, scale=scale, act=act, layout=layout),
        grid=(b, s // bm),
        in_specs=in_specs,
        out_specs=out_spec,
        out_shape=out_shape,
        compiler_params=_params("parallel", "parallel"),
    )(*args)


def _split3(v):
    hi = v.astype(BF16).astype(F32)
    r1 = v - hi
    mid = r1.astype(BF16).astype(F32)
    return hi, mid, r1 - mid


def _prenorm_cum_kernel(x_ref, g_ref, scale_ref, shift_ref, wf_ref, bf_ref, sel_ref, h_ref, kext_ref, tot_ref):
    t = ATT_TILE
    r = lax.broadcasted_iota(jnp.int32, (t, t), 0)
    c = lax.broadcasted_iota(jnp.int32, (t, t), 1)
    tri = jnp.where(c <= r, 1.0, 0.0).astype(BF16)
    for part in range(x_ref.shape[1] // t):
        rows = slice(part * t, (part + 1) * t)
        x = x_ref[0, rows]
        ms = jnp.mean(x * x, axis=-1, keepdims=True)
        y = x * lax.rsqrt(ms + NORM_EPS) * g_ref[...]
        h = (y * (1.0 + scale_ref[0]) + shift_ref[0]).astype(BF16)
        h_ref[0, rows] = h
        logit = lax.dot_general(h, wf_ref[...], (((1,), (1,)), ((), ())),
                                preferred_element_type=F32) + bf_ref[...]
        log_f = jnp.minimum(logit, 0.0) - jnp.log(1.0 + jnp.exp(-jnp.abs(logit)))
        terms = jnp.concatenate([v.astype(BF16) for v in _split3(log_f)], axis=1)
        sums = jnp.dot(tri, terms, preferred_element_type=F32)
        cum = sum(sums[:, n * LANES:(n + 1) * LANES] for n in range(BIAS_TERMS))
        bias = cum * (-LOG2E)
        tot_ref[0, part] = jnp.broadcast_to(-bias[t - 1:t, :], tot_ref.shape[2:])
        split = jnp.concatenate([v.astype(BF16) for v in _split3(bias)], axis=1)
        ext = jnp.dot(split, sel_ref[...], preferred_element_type=F32)
        for hh in range(N_HEADS):
            kext_ref[0, hh, rows] = ext[:, hh * LANES:(hh + 1) * LANES].astype(BF16)


def _bias_selection():
    sel = np.zeros((BIAS_TERMS, LANES, N_HEADS, LANES), np.float32)
    for n in range(BIAS_TERMS):
        for hh in range(N_HEADS):
            sel[n, hh, hh, n] = 1.0
    return sel.reshape(BIAS_TERMS * LANES, N_HEADS * LANES)


def _prenorm_cum_call(x, g, scale, shift, wf_pad, bf_pad, bm):
    b, s, d = x.shape
    sel = jnp.asarray(_bias_selection(), BF16)
    row = pl.BlockSpec((1, bm, d), lambda bi, i: (bi, i, 0))
    vec = pl.BlockSpec((1, 1, d), lambda bi, i: (bi, 0, 0))
    return pl.pallas_call(
        _prenorm_cum_kernel,
        grid=(b, s // bm),
        in_specs=[row, pl.BlockSpec((1, d), lambda bi, i: (0, 0)), vec, vec,
                  pl.BlockSpec((LANES, d), lambda bi, i: (0, 0)),
                  pl.BlockSpec((1, LANES), lambda bi, i: (0, 0)),
                  pl.BlockSpec(sel.shape, lambda bi, i: (0, 0))],
        out_specs=[row,
                   pl.BlockSpec((1, N_HEADS, bm, LANES), lambda bi, i: (bi, 0, i, 0)),
                   pl.BlockSpec((1, bm // ATT_TILE, SUBLANES, LANES), lambda bi, i: (bi, i, 0, 0))],
        out_shape=[jax.ShapeDtypeStruct((b, s, d), BF16),
                   jax.ShapeDtypeStruct((b, N_HEADS, s, LANES), BF16),
                   jax.ShapeDtypeStruct((b, s // ATT_TILE, SUBLANES, LANES), F32)],
        compiler_params=_params("parallel", "parallel"),
    )(x, g, scale, shift, wf_pad, bf_pad, sel)


PIPE_SLOTS = 3
PV_LAG = 2
DENOM_ROWS = 16


def _attend(i, n_maps, scores, next_scores, vt_ref, visible, frame_shift, finish, m, acc, *bufs):
    s_buf, c_buf, p_buf, a_buf = (bufs[k * PIPE_SLOTS:(k + 1) * PIPE_SLOTS] for k in range(4))
    hd = acc.shape[1] - DENOM_ROWS
    ones = jnp.ones((DENOM_ROWS, vt_ref.shape[-1]), BF16)
    m[...] = jnp.full(m.shape, NEG_BIG, F32)

    def put_scores(slot, tiles):
        for n, s in enumerate(tiles):
            s_buf[slot][n] = s
            c_buf[slot][n] = jnp.max(s, axis=0, keepdims=True)

    def softmax(s, s_max, n):
        m_prev = m[n]
        m_new = jnp.maximum(m_prev, s_max)
        alpha = jnp.exp2(m_prev - m_new)
        p = jnp.exp2(s - m_new).astype(BF16)
        return m_new, alpha, p

    def value_product(prev, alpha, j, p):
        vt_ones = jnp.concatenate([vt_ref[0, 0, j], ones], axis=0)
        return alpha * prev + jnp.dot(vt_ones, p, preferred_element_type=F32)

    def step(j, slot, prefetch=True):
        if prefetch:
            put_scores((slot + 2) % PIPE_SLOTS, scores(jnp.minimum(j + 2, i)))
        old = (slot - PV_LAG) % PIPE_SLOTS
        for n in range(n_maps):
            acc[n] = value_product(acc[n], a_buf[old][n], jnp.maximum(j - PV_LAG, 0), p_buf[old][n])
            m_new, alpha, p = softmax(s_buf[slot][n], c_buf[slot][n], n)
            p_buf[slot][n] = p
            a_buf[slot][n] = alpha
            m[n] = m_new + frame_shift(j)

    def last(slot):
        results = []
        for n in range(n_maps):
            out = acc[n]
            for back in range(PV_LAG, 0, -1):
                old = (slot - back) % PIPE_SLOTS
                out = value_product(out, a_buf[old][n], jnp.maximum(i - back, 0), p_buf[old][n])
            s = jnp.where(visible(), s_buf[slot][n], -jnp.inf)
            _, alpha, p = softmax(s, jnp.max(s, axis=0, keepdims=True), n)
            out = value_product(out, alpha, i, p)
            results.append((out[:hd], jnp.mean(out[hd:], axis=0, keepdims=True)))
        put_scores(0, next_scores(0))
        put_scores(1, next_scores(1))
        finish(results)

    @pl.when(i == 0)
    def _():
        acc[...] = jnp.zeros(acc.shape, F32)
        for slot in range(PIPE_SLOTS - PV_LAG, PIPE_SLOTS):
            p_buf[slot][...] = jnp.zeros(p_buf[slot].shape, BF16)
            a_buf[slot][...] = jnp.ones(a_buf[slot].shape, F32)
        put_scores(0, scores(0))

    def rotation(jj, carry):
        for slot in range(PIPE_SLOTS):
            step(PIPE_SLOTS * jj + slot, slot)
        return carry

    trip = 2 * PIPE_SLOTS

    def two_rotations(u, carry):
        return rotation(2 * u + 1, rotation(2 * u, carry))

    lax.fori_loop(0, i // trip, two_rotations, 0)

    for rem in range(trip):
        @pl.when(i % trip == rem)
        def _(rem=rem):
            for k in range(rem):
                step(i - rem + k, k % PIPE_SLOTS, prefetch=k + 2 <= rem)
            last(rem % PIPE_SLOTS)


def _attend_scratch(n_maps, hd, tk, tq):
    return ([pltpu.VMEM((n_maps, 1, tq), F32),
             pltpu.VMEM((n_maps, hd + DENOM_ROWS, tq), F32)]
            + [pltpu.VMEM((n_maps, tk, tq), F32)] * PIPE_SLOTS
            + [pltpu.VMEM((n_maps, 1, tq), F32)] * PIPE_SLOTS
            + [pltpu.VMEM((n_maps, tk, tq), BF16)] * PIPE_SLOTS
            + [pltpu.VMEM((n_maps, 1, tq), F32)] * PIPE_SLOTS)


def _diff_kernel(lam_ref, qt_ref, k_ref, vt_ref, g_ref, o_ref, *scratch):
    i = pl.program_id(2)
    nq, t = qt_ref.shape[2], qt_ref.shape[-1]

    def both_maps(block):
        qt = qt_ref[0, 0, block]
        row = lax.broadcasted_iota(jnp.int32, qt.shape, 0)
        map1 = (row // ROPE_HALF) % 2 == 0
        zero = jnp.zeros_like(qt)
        return jnp.concatenate([jnp.where(map1, qt, zero), jnp.where(map1, zero, qt)], axis=1)

    q12 = both_maps(i)

    def scores(j):
        return (jnp.dot(k_ref[0, 0, j], q12, preferred_element_type=F32),)

    def next_scores(j):
        return (jnp.dot(k_ref[0, 0, j], both_maps(jnp.minimum(i + 1, nq - 1)), preferred_element_type=F32),)

    def visible():
        kc = lax.broadcasted_iota(jnp.int32, (t, 2 * t), 0) // CHUNK
        qc = (lax.broadcasted_iota(jnp.int32, (t, 2 * t), 1) % t) // CHUNK
        return kc <= qc

    def finish(results):
        ((out, l),) = results
        o = out / l
        o = o[:, :t] - lam_ref[0] * o[:, t:]
        ms = jnp.mean(o * o, axis=0, keepdims=True)
        g = jnp.concatenate([g_ref[...]] * (t // LANES), axis=1)
        y = o * lax.rsqrt(ms + SUBLN_EPS) * g
        o_ref[0, 0] = y.T.astype(BF16)

    _attend(i, 1, scores, next_scores, vt_ref, visible, lambda j: 0.0, finish, *scratch)


def _diff_call(lam, qt, k, vt, g_rep):
    b, nh, nq, hd, t = qt.shape
    return pl.pallas_call(
        _diff_kernel,
        grid=(b, nh, nq),
        in_specs=[pl.BlockSpec(memory_space=pltpu.SMEM),
                  pl.BlockSpec((1, 1, nq, hd, t), lambda bi, h, i: (bi, h, 0, 0, 0)),
                  pl.BlockSpec((1, 1, nq, t, hd), lambda bi, h, i: (bi, h, 0, 0, 0)),
                  pl.BlockSpec((1, 1, nq, hd, t), lambda bi, h, i: (bi, h, 0, 0, 0)),
                  pl.BlockSpec((hd, LANES), lambda bi, h, i: (0, 0))],
        out_specs=pl.BlockSpec((1, 1, t, hd), lambda bi, h, i: (bi, h, i, 0)),
        out_shape=jax.ShapeDtypeStruct((b, nh, nq * t, hd), BF16),
        scratch_shapes=_attend_scratch(1, hd, t, 2 * t),
        compiler_params=_params("parallel", "parallel", "arbitrary"),
    )(lam, qt, k, vt, g_rep)


def _fox_kernel(tot_ref, qt_ref, k_ref, kext_ref, vt_ref, o_ref, *scratch):
    bi, h, i = pl.program_id(0), pl.program_id(1), pl.program_id(2)
    nq, t = qt_ref.shape[2], qt_ref.shape[-1]
    tot_base = (bi * N_HEADS + h) * nq

    def with_ones(block):
        qt = qt_ref[0, 0, block]
        row = lax.broadcasted_iota(jnp.int32, qt.shape, 0)
        return jnp.concatenate([qt, jnp.where(row < BIAS_TERMS, 1.0, 0.0).astype(BF16)], axis=0)

    q_cat = with_ones(i)

    def scores_of(q, j):
        k_cat = jnp.concatenate([k_ref[0, 0, j], kext_ref[0, 0, j]], axis=1)
        return (jnp.dot(k_cat, q, preferred_element_type=F32),)

    def scores(j):
        return scores_of(q_cat, j)

    def next_scores(j):
        return scores_of(with_ones(jnp.minimum(i + 1, nq - 1)), j)

    def visible():
        return lax.broadcasted_iota(jnp.int32, (t, t), 0) <= lax.broadcasted_iota(jnp.int32, (t, t), 1)

    def finish(results):
        ((o, l),) = results
        o_ref[0, 0] = (o / l).T.astype(BF16)

    _attend(i, 1, scores, next_scores, vt_ref, visible, lambda j: tot_ref[tot_base + j], finish, *scratch)


def _fox_call(tot, qt, k, kext, vt):
    b, nh, nq, hd, t = qt.shape
    return pl.pallas_call(
        _fox_kernel,
        grid=(b, nh, nq),
        in_specs=[pl.BlockSpec(memory_space=pltpu.SMEM),
                  pl.BlockSpec((1, 1, nq, hd, t), lambda bi, h, i: (bi, h, 0, 0, 0)),
                  pl.BlockSpec((1, 1, nq, t, hd), lambda bi, h, i: (bi, h, 0, 0, 0)),
                  pl.BlockSpec((1, 1, nq, t, LANES), lambda bi, h, i: (bi, h, 0, 0, 0)),
                  pl.BlockSpec((1, 1, nq, hd, t), lambda bi, h, i: (bi, h, 0, 0, 0))],
        out_specs=pl.BlockSpec((1, 1, t, hd), lambda bi, h, i: (bi, h, i, 0)),
        out_shape=jax.ShapeDtypeStruct((b, nh, nq * t, hd), BF16),
        scratch_shapes=_attend_scratch(1, hd, t, t),
        compiler_params=_params("parallel", "parallel", "arbitrary"),
    )(tot, qt, k, kext, vt)


def _out_kernel(oa_ref, za_ref, ob_ref, zb_ref, ma_ref, mb_ref, x_ref, gate_ref,
                wa_ref, wb_ref, wo_ref, gf_ref, o_ref):
    def gated(o_ref_, z_ref_):
        o = jnp.concatenate([o_ref_[0, hh] for hh in range(N_HEADS)], axis=1)
        return (o.astype(F32) * z_ref_[0].astype(F32)).astype(BF16)

    ya = jnp.dot(gated(oa_ref, za_ref), wa_ref[...], preferred_element_type=F32)
    yb = jnp.dot(gated(ob_ref, zb_ref), wb_ref[...], preferred_element_type=F32)
    merged = ma_ref[0].astype(F32) * ya + mb_ref[0].astype(F32) * yb
    out = jnp.dot(merged.astype(BF16), wo_ref[...], preferred_element_type=F32)
    xn = x_ref[0] + gate_ref[0] * out
    ms = jnp.mean(xn * xn, axis=-1, keepdims=True)
    o_ref[0] = xn * lax.rsqrt(ms + NORM_EPS) * gf_ref[...]


def _out_call(oa, za, ob, zb, ma, mb, x, gate, wa, wb, wo, gf, bm):
    b, s, d = x.shape
    heads = pl.BlockSpec((1, N_HEADS, bm, HEAD_DIM), lambda bi, i: (bi, 0, i, 0))
    row = pl.BlockSpec((1, bm, d), lambda bi, i: (bi, i, 0))
    wspec = pl.BlockSpec((d, d), lambda bi, i: (0, 0))
    return pl.pallas_call(
        _out_kernel,
        grid=(b, s // bm),
        in_specs=[heads, row, heads, row, row, row, row,
                  pl.BlockSpec((1, 1, d), lambda bi, i: (bi, 0, 0)),
                  wspec, wspec, wspec,
                  pl.BlockSpec((1, d), lambda bi, i: (0, 0))],
        out_specs=row,
        out_shape=jax.ShapeDtypeStruct((b, s, d), F32),
        compiler_params=_params("parallel", "parallel"),
    )(oa, za, ob, zb, ma, mb, x, gate, wa, wb, wo, gf)


def kernel(x, c, positions, w_ada, b_ada, g_norm, w_in, b_forget, lambda_q1, lambda_k1, lambda_q2,
           lambda_k2, g_subln, w_branch_a, w_branch_b, w_out, g_final):
    b, s, d = x.shape
    assert d == D_MODEL and s % ATT_TILE == 0 and b <= SUBLANES
    assert w_ada.shape[0] == 1, "single-layer block"
    t = ATT_TILE
    nq = s // t
    n_main = 8 * D_MODEL

    c_pad = jnp.zeros((SUBLANES, d), F32).at[:b].set(c)
    w_t = jnp.swapaxes(w_in, 1, 2)
    w_qk = w_t[:, :2 * D_MODEL].reshape(2 * N_HEADS, 2, 2, ROPE_HALF, d).transpose(0, 2, 1, 3, 4)
    w_qk = w_qk.reshape(1, 2 * D_MODEL, d)
    w_merge = w_t[:, n_main + N_HEADS:]
    wf_pad = jnp.zeros((LANES, d), F32).at[:N_HEADS].set(w_t[0, n_main:n_main + N_HEADS]).astype(BF16)
    bf_pad = jnp.zeros((1, LANES), F32).at[0, :N_HEADS].set(b_forget[0])
    inv_freq = ROPE_THETA ** (-jnp.arange(ROPE_HALF, dtype=F32) / ROPE_HALF)
    invf = jnp.tile(inv_freq, LANES // ROPE_HALF)[None, :]
    sgn = jnp.concatenate([-jnp.ones((HEAD_DIM // 2,), F32), jnp.ones((HEAD_DIM // 2,), F32)])[None, :]
    pos_b = jnp.broadcast_to(positions.astype(F32)[:, :, None], (b, s, LANES))
    g_rep = jnp.broadcast_to((g_subln[0] * (1.0 - LAMBDA_INIT))[:, None], (HEAD_DIM, LANES))

    mod, lam_tile = _mod_call(c_pad, w_ada, b_ada, lambda_q1, lambda_k1, lambda_q2, lambda_k2)
    shift = mod[:b, None, 0:d]
    scale = mod[:b, None, d:2 * d]
    gate = mod[:b, None, 2 * d:3 * d]
    lam = lam_tile[0, :1]

    h, kext, tot_tile = _prenorm_cum_call(x, g_norm, scale, shift, wf_pad, bf_pad, 2 * t if nq % 2 == 0 else t)
    tables = _rope_table_call(pos_b, invf, sgn, t)

    pm = PROJ_ROWS if s % PROJ_ROWS == 0 else t
    proj = functools.partial(_proj_call, h)
    qa_t = proj(w_qk, 0, tables, pm, rope=True, scale=LOG2E * DA_QK_DIM ** -0.5, layout="heads_t")
    ka = proj(w_qk, 1, tables, pm, rope=True, layout="heads")
    va_t = proj(w_t, 2, None, pm, layout="heads_t")
    za = proj(w_t, 3, None, pm, act="silu")
    qb_t = proj(w_t, 4, None, pm, scale=LOG2E * HEAD_DIM ** -0.5, layout="heads_t")
    kb = proj(w_t, 5, None, pm, layout="heads")
    vb_t = proj(w_t, 6, None, pm, layout="heads_t")
    zb = proj(w_t, 7, None, pm, act="silu")
    ma = proj(w_merge, 0, None, pm, act="sigmoid")
    mb = proj(w_merge, 1, None, pm, act="sigmoid")

    tot = tot_tile[:, :, 0, :N_HEADS].transpose(0, 2, 1).reshape(-1)

    def tiles(a):
        return a.reshape(b, N_HEADS, nq, t, a.shape[-1])

    oa = _diff_call(lam, qa_t, tiles(ka), va_t, g_rep)
    ob = _fox_call(tot, qb_t, tiles(kb), tiles(kext), vb_t)

    return _out_call(oa, za, ob, zb, ma, mb, x, gate,
                     w_branch_a[0].astype(BF16), w_branch_b[0].astype(BF16), w_out[0].astype(BF16),
                     g_final[None, :], t)
```

```python
import functools
import math

import jax
import jax.numpy as jnp
import numpy as np
from jax import lax
from jax.experimental import pallas as pl
from jax.experimental.pallas import tpu as pltpu

F32 = jnp.float32
BF16 = jnp.bfloat16

D_MODEL = 1024
N_HEADS = 8
HEAD_DIM = 128
DA_QK_DIM = 64
ROPE_HALF = DA_QK_DIM // 2
CHUNK = 64
ROPE_THETA = 10000.0
NORM_EPS = 1e-6
SUBLN_EPS = 1e-5
LAMBDA_INIT = 0.8 - 0.6 * math.exp(-0.3 * 0)
LOG2E = math.log2(math.e)

LANES = 128
SUBLANES = 8
ATT_TILE = 512
PROJ_ROWS = 1024
BIAS_TERMS = 3
VMEM_LIMIT = 48 * 1024 * 1024
NEG_BIG = -1e30


def _params(*sem):
    return pltpu.CompilerParams(dimension_semantics=sem, vmem_limit_bytes=VMEM_LIMIT)


def _mod_kernel(c_ref, w_ref, b_ref, lq1_ref, lk1_ref, lq2_ref, lk2_ref, mod_ref, lam_ref):
    c = c_ref[...]
    c_act = c * jax.nn.sigmoid(c)
    mod_ref[...] = jnp.dot(c_act, w_ref[0], precision=lax.Precision.HIGHEST,
                           preferred_element_type=F32) + b_ref[...]
    s1 = jnp.sum(lq1_ref[...] * lk1_ref[...], axis=-1, keepdims=True)
    s2 = jnp.sum(lq2_ref[...] * lk2_ref[...], axis=-1, keepdims=True)
    lam = jnp.exp(s1) - jnp.exp(s2) + LAMBDA_INIT
    lam_ref[...] = jnp.broadcast_to(lam, lam_ref.shape)


def _mod_call(c_pad, w_ada, b_ada, lq1, lk1, lq2, lk2):
    d = D_MODEL
    lam_spec = pl.BlockSpec((1, DA_QK_DIM), lambda j: (0, 0))
    return pl.pallas_call(
        _mod_kernel,
        grid=(3,),
        in_specs=[pl.BlockSpec((SUBLANES, d), lambda j: (0, 0)),
                  pl.BlockSpec((1, d, d), lambda j: (0, 0, j)),
                  pl.BlockSpec((1, d), lambda j: (0, j)),
                  lam_spec, lam_spec, lam_spec, lam_spec],
        out_specs=[pl.BlockSpec((SUBLANES, d), lambda j: (0, j)),
                   pl.BlockSpec((SUBLANES, LANES), lambda j: (0, 0))],
        out_shape=[jax.ShapeDtypeStruct((SUBLANES, 3 * d), F32),
                   jax.ShapeDtypeStruct((SUBLANES, LANES), F32)],
        compiler_params=_params("arbitrary"),
    )(c_pad, w_ada, b_ada, lq1, lk1, lq2, lk2)


def _rope_table_kernel(pos_ref, invf_ref, sgn_ref, cos_ref, sin_ref):
    ang = pos_ref[0] * invf_ref[...]
    cos_ref[0] = jnp.cos(ang)
    sin_ref[0] = jnp.sin(ang) * sgn_ref[...]


def _rope_table_call(pos_b, invf, sgn, bm):
    b, s, _ = pos_b.shape
    row = pl.BlockSpec((1, bm, LANES), lambda bi, i: (bi, i, 0))
    vec = pl.BlockSpec((1, LANES), lambda bi, i: (0, 0))
    shape = jax.ShapeDtypeStruct((b, s, LANES), F32)
    return pl.pallas_call(
        _rope_table_kernel,
        grid=(b, s // bm),
        in_specs=[row, vec, vec],
        out_specs=[row, row],
        out_shape=[shape, shape],
        compiler_params=_params("parallel", "parallel"),
    )(pos_b, invf, sgn)


def _proj_kernel(*refs, rope, scale, act, layout):
    if rope:
        h_ref, w_ref, cos_ref, sin_ref, o_ref = refs
    else:
        h_ref, w_ref, o_ref = refs
    acc = lax.dot_general(h_ref[0], w_ref[0].astype(BF16), (((1,), (1,)), ((), ())),
                          preferred_element_type=F32)
    if act in ("silu", "sigmoid"):
        gate = 0.5 * jnp.tanh(0.5 * acc) + 0.5
        acc = acc * gate if act == "silu" else gate
    if layout == "nat":
        o_ref[0] = acc.astype(BF16)
        return
    if rope:
        cos = cos_ref[0]
        sin = sin_ref[0]
    for hh in range(N_HEADS):
        t = acc[:, hh * HEAD_DIM:(hh + 1) * HEAD_DIM]
        if rope:
            t = t * cos + pltpu.roll(t, HEAD_DIM // 2, 1) * sin
        if scale != 1.0:
            t = t * scale
        if layout == "heads":
            o_ref[0, hh] = t.astype(BF16)
        else:
            tile = o_ref.shape[-1]
            for part in range(o_ref.shape[2]):
                o_ref[0, hh, part] = t[part * tile:(part + 1) * tile].T.astype(BF16)


def _proj_call(h, w_all, col_block, tables, bm, *, rope=False, scale=1.0, act=None, layout="nat"):
    b, s, d = h.shape
    n = D_MODEL
    in_specs = [pl.BlockSpec((1, bm, d), lambda bi, i: (bi, i, 0)),
                pl.BlockSpec((1, n, d), lambda bi, i: (0, col_block, 0))]
    args = [h, w_all]
    if rope:
        tab = pl.BlockSpec((1, bm, LANES), lambda bi, i: (bi, i, 0))
        in_specs += [tab, tab]
        args += list(tables)
    if layout == "nat":
        out_spec = pl.BlockSpec((1, bm, n), lambda bi, i: (bi, i, 0))
        out_shape = jax.ShapeDtypeStruct((b, s, n), BF16)
    elif layout == "heads":
        out_spec = pl.BlockSpec((1, N_HEADS, bm, HEAD_DIM), lambda bi, i: (bi, 0, i, 0))
        out_shape = jax.ShapeDtypeStruct((b, N_HEADS, s, HEAD_DIM), BF16)
    else:
        parts = bm // ATT_TILE
        out_spec = pl.BlockSpec((1, N_HEADS, parts, HEAD_DIM, ATT_TILE), lambda bi, i: (bi, 0, i, 0, 0))
        out_shape = jax.ShapeDtypeStruct((b, N_HEADS, s // ATT_TILE, HEAD_DIM, ATT_TILE), BF16)
    return pl.pallas_call(
        functools.partial(_proj_kernel, rope=rope, scale=scale, act=act, layout=layout),
        grid=(b, s // bm),
        in_specs=in_specs,
        out_specs=out_spec,
        out_shape=out_shape,
        compiler_params=_params("parallel", "parallel"),
    )(*args)


def _split3(v):
    hi = v.astype(BF16).astype(F32)
    r1 = v - hi
    mid = r1.astype(BF16).astype(F32)
    return hi, mid, r1 - mid


def _prenorm_cum_kernel(x_ref, g_ref, scale_ref, shift_ref, wf_ref, bf_ref, sel_ref, h_ref, kext_ref, tot_ref):
    t = ATT_TILE
    r = lax.broadcasted_iota(jnp.int32, (t, t), 0)
    c = lax.broadcasted_iota(jnp.int32, (t, t), 1)
    tri = jnp.where(c <= r, 1.0, 0.0).astype(BF16)
    for part in range(x_ref.shape[1] // t):
        rows = slice(part * t, (part + 1) * t)
        x = x_ref[0, rows]
        ms = jnp.mean(x * x, axis=-1, keepdims=True)
        y = x * lax.rsqrt(ms + NORM_EPS) * g_ref[...]
        h = (y * (1.0 + scale_ref[0]) + shift_ref[0]).astype(BF16)
        h_ref[0, rows] = h
        logit = lax.dot_general(h, wf_ref[...], (((1,), (1,)), ((), ())),
                                preferred_element_type=F32) + bf_ref[...]
        log_f = jnp.minimum(logit, 0.0) - jnp.log(1.0 + jnp.exp(-jnp.abs(logit)))
        terms = jnp.concatenate([v.astype(BF16) for v in _split3(log_f)], axis=1)
        sums = jnp.dot(tri, terms, preferred_element_type=F32)
        cum = sum(sums[:, n * LANES:(n + 1) * LANES] for n in range(BIAS_TERMS))
        bias = cum * (-LOG2E)
        tot_ref[0, part] = jnp.broadcast_to(-bias[t - 1:t, :], tot_ref.shape[2:])
        split = jnp.concatenate([v.astype(BF16) for v in _split3(bias)], axis=1)
        ext = jnp.dot(split, sel_ref[...], preferred_element_type=F32)
        for hh in range(N_HEADS):
            kext_ref[0, hh, rows] = ext[:, hh * LANES:(hh + 1) * LANES].astype(BF16)


def _bias_selection():
    sel = np.zeros((BIAS_TERMS, LANES, N_HEADS, LANES), np.float32)
    for n in range(BIAS_TERMS):
        for hh in range(N_HEADS):
            sel[n, hh, hh, n] = 1.0
    return sel.reshape(BIAS_TERMS * LANES, N_HEADS * LANES)


def _prenorm_cum_call(x, g, scale, shift, wf_pad, bf_pad, bm):
    b, s, d = x.shape
    sel = jnp.asarray(_bias_selection(), BF16)
    row = pl.BlockSpec((1, bm, d), lambda bi, i: (bi, i, 0))
    vec = pl.BlockSpec((1, 1, d), lambda bi, i: (bi, 0, 0))
    return pl.pallas_call(
        _prenorm_cum_kernel,
        grid=(b, s // bm),
        in_specs=[row, pl.BlockSpec((1, d), lambda bi, i: (0, 0)), vec, vec,
                  pl.BlockSpec((LANES, d), lambda bi, i: (0, 0)),
                  pl.BlockSpec((1, LANES), lambda bi, i: (0, 0)),
                  pl.BlockSpec(sel.shape, lambda bi, i: (0, 0))],
        out_specs=[row,
                   pl.BlockSpec((1, N_HEADS, bm, LANES), lambda bi, i: (bi, 0, i, 0)),
                   pl.BlockSpec((1, bm // ATT_TILE, SUBLANES, LANES), lambda bi, i: (bi, i, 0, 0))],
        out_shape=[jax.ShapeDtypeStruct((b, s, d), BF16),
                   jax.ShapeDtypeStruct((b, N_HEADS, s, LANES), BF16),
                   jax.ShapeDtypeStruct((b, s // ATT_TILE, SUBLANES, LANES), F32)],
        compiler_params=_params("parallel", "parallel"),
    )(x, g, scale, shift, wf_pad, bf_pad, sel)


PIPE_SLOTS = 3
TRIP_ROTATIONS = 3
PV_LAG = 2
DENOM_ROWS = 16


def _attend(i, n_maps, scores, next_scores, vt_ref, visible, frame_shift, finish, m, acc, *bufs):
    s_buf, c_buf, p_buf, a_buf = (bufs[k * PIPE_SLOTS:(k + 1) * PIPE_SLOTS] for k in range(4))
    hd = acc.shape[1] - DENOM_ROWS
    ones = jnp.ones((DENOM_ROWS, vt_ref.shape[-1]), BF16)
    m[...] = jnp.full(m.shape, NEG_BIG, F32)

    def put_scores(slot, tiles):
        for n, s in enumerate(tiles):
            s_buf[slot][n] = s
            c_buf[slot][n] = jnp.max(s, axis=0, keepdims=True)

    def softmax(s, s_max, n):
        m_prev = m[n]
        m_new = jnp.maximum(m_prev, s_max)
        alpha = jnp.exp2(m_prev - m_new)
        p = jnp.exp2(s - m_new).astype(BF16)
        return m_new, alpha, p

    def value_product(prev, alpha, j, p):
        vt_ones = jnp.concatenate([vt_ref[0, 0, j], ones], axis=0)
        return alpha * prev + jnp.dot(vt_ones, p, preferred_element_type=F32)

    def step(j, slot, prefetch=True):
        if prefetch:
            put_scores((slot + 2) % PIPE_SLOTS, scores(jnp.minimum(j + 2, i)))
        old = (slot - PV_LAG) % PIPE_SLOTS
        for n in range(n_maps):
            acc[n] = value_product(acc[n], a_buf[old][n], jnp.maximum(j - PV_LAG, 0), p_buf[old][n])
            m_new, alpha, p = softmax(s_buf[slot][n], c_buf[slot][n], n)
            p_buf[slot][n] = p
            a_buf[slot][n] = alpha
            m[n] = m_new + frame_shift(j)

    def last(slot):
        results = []
        for n in range(n_maps):
            out = acc[n]
            for back in range(PV_LAG, 0, -1):
                old = (slot - back) % PIPE_SLOTS
                out = value_product(out, a_buf[old][n], jnp.maximum(i - back, 0), p_buf[old][n])
            s = jnp.where(visible(), s_buf[slot][n], -jnp.inf)
            _, alpha, p = softmax(s, jnp.max(s, axis=0, keepdims=True), n)
            out = value_product(out, alpha, i, p)
            results.append((out[:hd], jnp.mean(out[hd:], axis=0, keepdims=True)))
        put_scores(0, next_scores(0))
        put_scores(1, next_scores(1))
        finish(results)

    @pl.when(i == 0)
    def _():
        acc[...] = jnp.zeros(acc.shape, F32)
        for slot in range(PIPE_SLOTS - PV_LAG, PIPE_SLOTS):
            p_buf[slot][...] = jnp.zeros(p_buf[slot].shape, BF16)
            a_buf[slot][...] = jnp.ones(a_buf[slot].shape, F32)
        put_scores(0, scores(0))

    def rotation(jj, carry):
        for slot in range(PIPE_SLOTS):
            step(PIPE_SLOTS * jj + slot, slot)
        return carry

    trip = TRIP_ROTATIONS * PIPE_SLOTS

    def rotations(u, carry):
        for r in range(TRIP_ROTATIONS):
            carry = rotation(TRIP_ROTATIONS * u + r, carry)
        return carry

    lax.fori_loop(0, i // trip, rotations, 0)

    for rem in range(trip):
        @pl.when(i % trip == rem)
        def _(rem=rem):
            for k in range(rem):
                step(i - rem + k, k % PIPE_SLOTS, prefetch=k + 2 <= rem)
            last(rem % PIPE_SLOTS)


def _attend_scratch(n_maps, hd, tk, tq):
    return ([pltpu.VMEM((n_maps, 1, tq), F32),
             pltpu.VMEM((n_maps, hd + DENOM_ROWS, tq), F32)]
            + [pltpu.VMEM((n_maps, tk, tq), F32)] * PIPE_SLOTS
            + [pltpu.VMEM((n_maps, 1, tq), F32)] * PIPE_SLOTS
            + [pltpu.VMEM((n_maps, tk, tq), BF16)] * PIPE_SLOTS
            + [pltpu.VMEM((n_maps, 1, tq), F32)] * PIPE_SLOTS)


def _diff_kernel(lam_ref, qt_ref, k_ref, vt_ref, g_ref, o_ref, *scratch):
    i = pl.program_id(2)
    nq, t = qt_ref.shape[2], qt_ref.shape[-1]

    def both_maps(block):
        qt = qt_ref[0, 0, block]
        row = lax.broadcasted_iota(jnp.int32, qt.shape, 0)
        map1 = (row // ROPE_HALF) % 2 == 0
        zero = jnp.zeros_like(qt)
        return jnp.concatenate([jnp.where(map1, qt, zero), jnp.where(map1, zero, qt)], axis=1)

    q12 = both_maps(i)

    def scores(j):
        return (jnp.dot(k_ref[0, 0, j], q12, preferred_element_type=F32),)

    def next_scores(j):
        return (jnp.dot(k_ref[0, 0, j], both_maps(jnp.minimum(i + 1, nq - 1)), preferred_element_type=F32),)

    def visible():
        kc = lax.broadcasted_iota(jnp.int32, (t, 2 * t), 0) // CHUNK
        qc = (lax.broadcasted_iota(jnp.int32, (t, 2 * t), 1) % t) // CHUNK
        return kc <= qc

    def finish(results):
        ((out, l),) = results
        o = out / l
        o = o[:, :t] - lam_ref[0] * o[:, t:]
        ms = jnp.mean(o * o, axis=0, keepdims=True)
        g = jnp.concatenate([g_ref[...]] * (t // LANES), axis=1)
        y = o * lax.rsqrt(ms + SUBLN_EPS) * g
        o_ref[0, 0] = y.T.astype(BF16)

    _attend(i, 1, scores, next_scores, vt_ref, visible, lambda j: 0.0, finish, *scratch)


def _diff_call(lam, qt, k, vt, g_rep):
    b, nh, nq, hd, t = qt.shape
    return pl.pallas_call(
        _diff_kernel,
        grid=(b, nh, nq),
        in_specs=[pl.BlockSpec(memory_space=pltpu.SMEM),
                  pl.BlockSpec((1, 1, nq, hd, t), lambda bi, h, i: (bi, h, 0, 0, 0)),
                  pl.BlockSpec((1, 1, nq, t, hd), lambda bi, h, i: (bi, h, 0, 0, 0)),
                  pl.BlockSpec((1, 1, nq, hd, t), lambda bi, h, i: (bi, h, 0, 0, 0)),
                  pl.BlockSpec((hd, LANES), lambda bi, h, i: (0, 0))],
        out_specs=pl.BlockSpec((1, 1, t, hd), lambda bi, h, i: (bi, h, i, 0)),
        out_shape=jax.ShapeDtypeStruct((b, nh, nq * t, hd), BF16),
        scratch_shapes=_attend_scratch(1, hd, t, 2 * t),
        compiler_params=_params("parallel", "parallel", "arbitrary"),
    )(lam, qt, k, vt, g_rep)


def _fox_kernel(tot_ref, qt_ref, k_ref, kext_ref, vt_ref, o_ref, *scratch):
    bi, h, i = pl.program_id(0), pl.program_id(1), pl.program_id(2)
    nq, t = qt_ref.shape[2], qt_ref.shape[-1]
    tot_base = (bi * N_HEADS + h) * nq

    def with_ones(block):
        qt = qt_ref[0, 0, block]
        row = lax.broadcasted_iota(jnp.int32, qt.shape, 0)
        return jnp.concatenate([qt, jnp.where(row < BIAS_TERMS, 1.0, 0.0).astype(BF16)], axis=0)

    q_cat = with_ones(i)

    def scores_of(q, j):
        k_cat = jnp.concatenate([k_ref[0, 0, j], kext_ref[0, 0, j]], axis=1)
        return (jnp.dot(k_cat, q, preferred_element_type=F32),)

    def scores(j):
        return scores_of(q_cat, j)

    def next_scores(j):
        return scores_of(with_ones(jnp.minimum(i + 1, nq - 1)), j)

    def visible():
        return lax.broadcasted_iota(jnp.int32, (t, t), 0) <= lax.broadcasted_iota(jnp.int32, (t, t), 1)

    def finish(results):
        ((o, l),) = results
        o_ref[0, 0] = (o / l).T.astype(BF16)

    _attend(i, 1, scores, next_scores, vt_ref, visible, lambda j: tot_ref[tot_base + j], finish, *scratch)


def _fox_call(tot, qt, k, kext, vt):
    b, nh, nq, hd, t = qt.shape
    return pl.pallas_call(
        _fox_kernel,
        grid=(b, nh, nq),
        in_specs=[pl.BlockSpec(memory_space=pltpu.SMEM),
                  pl.BlockSpec((1, 1, nq, hd, t), lambda bi, h, i: (bi, h, 0, 0, 0)),
                  pl.BlockSpec((1, 1, nq, t, hd), lambda bi, h, i: (bi, h, 0, 0, 0)),
                  pl.BlockSpec((1, 1, nq, t, LANES), lambda bi, h, i: (bi, h, 0, 0, 0)),
                  pl.BlockSpec((1, 1, nq, hd, t), lambda bi, h, i: (bi, h, 0, 0, 0))],
        out_specs=pl.BlockSpec((1, 1, t, hd), lambda bi, h, i: (bi, h, i, 0)),
        out_shape=jax.ShapeDtypeStruct((b, nh, nq * t, hd), BF16),
        scratch_shapes=_attend_scratch(1, hd, t, t),
        compiler_params=_params("parallel", "parallel", "arbitrary"),
    )(tot, qt, k, kext, vt)


def _out_kernel(oa_ref, za_ref, ob_ref, zb_ref, ma_ref, mb_ref, x_ref, gate_ref,
                wa_ref, wb_ref, wo_ref, gf_ref, o_ref):
    def gated(o_ref_, z_ref_):
        o = jnp.concatenate([o_ref_[0, hh] for hh in range(N_HEADS)], axis=1)
        return (o.astype(F32) * z_ref_[0].astype(F32)).astype(BF16)

    ya = jnp.dot(gated(oa_ref, za_ref), wa_ref[...], preferred_element_type=F32)
    yb = jnp.dot(gated(ob_ref, zb_ref), wb_ref[...], preferred_element_type=F32)
    merged = ma_ref[0].astype(F32) * ya + mb_ref[0].astype(F32) * yb
    out = jnp.dot(merged.astype(BF16), wo_ref[...], preferred_element_type=F32)
    xn = x_ref[0] + gate_ref[0] * out
    ms = jnp.mean(xn * xn, axis=-1, keepdims=True)
    o_ref[0] = xn * lax.rsqrt(ms + NORM_EPS) * gf_ref[...]


def _out_call(oa, za, ob, zb, ma, mb, x, gate, wa, wb, wo, gf, bm):
    b, s, d = x.shape
    heads = pl.BlockSpec((1, N_HEADS, bm, HEAD_DIM), lambda bi, i: (bi, 0, i, 0))
    row = pl.BlockSpec((1, bm, d), lambda bi, i: (bi, i, 0))
    wspec = pl.BlockSpec((d, d), lambda bi, i: (0, 0))
    return pl.pallas_call(
        _out_kernel,
        grid=(b, s // bm),
        in_specs=[heads, row, heads, row, row, row, row,
                  pl.BlockSpec((1, 1, d), lambda bi, i: (bi, 0, 0)),
                  wspec, wspec, wspec,
                  pl.BlockSpec((1, d), lambda bi, i: (0, 0))],
        out_specs=row,
        out_shape=jax.ShapeDtypeStruct((b, s, d), F32),
        compiler_params=_params("parallel", "parallel"),
    )(oa, za, ob, zb, ma, mb, x, gate, wa, wb, wo, gf)


def kernel(x, c, positions, w_ada, b_ada, g_norm, w_in, b_forget, lambda_q1, lambda_k1, lambda_q2,
           lambda_k2, g_subln, w_branch_a, w_branch_b, w_out, g_final):
    b, s, d = x.shape
    assert d == D_MODEL and s % ATT_TILE == 0 and b <= SUBLANES
    assert w_ada.shape[0] == 1, "single-layer block"
    t = ATT_TILE
    nq = s // t
    n_main = 8 * D_MODEL

    c_pad = jnp.zeros((SUBLANES, d), F32).at[:b].set(c)
    w_t = jnp.swapaxes(w_in, 1, 2)
    w_qk = w_t[:, :2 * D_MODEL].reshape(2 * N_HEADS, 2, 2, ROPE_HALF, d).transpose(0, 2, 1, 3, 4)
    w_qk = w_qk.reshape(1, 2 * D_MODEL, d)
    w_merge = w_t[:, n_main + N_HEADS:]
    wf_pad = jnp.zeros((LANES, d), F32).at[:N_HEADS].set(w_t[0, n_main:n_main + N_HEADS]).astype(BF16)
    bf_pad = jnp.zeros((1, LANES), F32).at[0, :N_HEADS].set(b_forget[0])
    inv_freq = ROPE_THETA ** (-jnp.arange(ROPE_HALF, dtype=F32) / ROPE_HALF)
    invf = jnp.tile(inv_freq, LANES // ROPE_HALF)[None, :]
    sgn = jnp.concatenate([-jnp.ones((HEAD_DIM // 2,), F32), jnp.ones((HEAD_DIM // 2,), F32)])[None, :]
    pos_b = jnp.broadcast_to(positions.astype(F32)[:, :, None], (b, s, LANES))
    g_rep = jnp.broadcast_to((g_subln[0] * (1.0 - LAMBDA_INIT))[:, None], (HEAD_DIM, LANES))

    mod, lam_tile = _mod_call(c_pad, w_ada, b_ada, lambda_q1, lambda_k1, lambda_q2, lambda_k2)
    shift = mod[:b, None, 0:d]
    scale = mod[:b, None, d:2 * d]
    gate = mod[:b, None, 2 * d:3 * d]
    lam = lam_tile[0, :1]

    h, kext, tot_tile = _prenorm_cum_call(x, g_norm, scale, shift, wf_pad, bf_pad, 2 * t if nq % 2 == 0 else t)
    tables = _rope_table_call(pos_b, invf, sgn, t)

    pm = PROJ_ROWS if s % PROJ_ROWS == 0 else t
    proj = functools.partial(_proj_call, h)
    qa_t = proj(w_qk, 0, tables, pm, rope=True, scale=LOG2E * DA_QK_DIM ** -0.5, layout="heads_t")
    ka = proj(w_qk, 1, tables, pm, rope=True, layout="heads")
    va_t = proj(w_t, 2, None, pm, layout="heads_t")
    za = proj(w_t, 3, None, pm, act="silu")
    qb_t = proj(w_t, 4, None, pm, scale=LOG2E * HEAD_DIM ** -0.5, layout="heads_t")
    kb = proj(w_t, 5, None, pm, layout="heads")
    vb_t = proj(w_t, 6, None, pm, layout="heads_t")
    zb = proj(w_t, 7, None, pm, act="silu")
    ma = proj(w_merge, 0, None, pm, act="sigmoid")
    mb = proj(w_merge, 1, None, pm, act="sigmoid")

    tot = tot_tile[:, :, 0, :N_HEADS].transpose(0, 2, 1).reshape(-1)

    def tiles(a):
        return a.reshape(b, N_HEADS, nq, t, a.shape[-1])

    oa = _diff_call(lam, qa_t, tiles(ka), va_t, g_rep)
    ob = _fox_call(tot, qb_t, tiles(kb), tiles(kext), vb_t)

    return _out_call(oa, za, ob, zb, ma, mb, x, gate,
                     w_branch_a[0].astype(BF16), w_branch_b[0].astype(BF16), w_out[0].astype(BF16),
                     g_final[None, :], t)
```

```python
import functools
import math

import jax
import jax.numpy as jnp
import numpy as np
from jax import lax
from jax.experimental import pallas as pl
from jax.experimental.pallas import tpu as pltpu

F32 = jnp.float32
BF16 = jnp.bfloat16

D_MODEL = 1024
N_HEADS = 8
HEAD_DIM = 128
DA_QK_DIM = 64
ROPE_HALF = DA_QK_DIM // 2
CHUNK = 64
ROPE_THETA = 10000.0
NORM_EPS = 1e-6
SUBLN_EPS = 1e-5
LAMBDA_INIT = 0.8 - 0.6 * math.exp(-0.3 * 0)
LOG2E = math.log2(math.e)

LANES = 128
SUBLANES = 8
ATT_TILE = 512
PROJ_ROWS = 1024
PROJ_ROWS_LIGHT = 2048
BIAS_TERMS = 3
VMEM_LIMIT = 48 * 1024 * 1024
NEG_BIG = -1e30


def _params(*sem):
    return pltpu.CompilerParams(dimension_semantics=sem, vmem_limit_bytes=VMEM_LIMIT)


def _mod_kernel(c_ref, w_ref, b_ref, lq1_ref, lk1_ref, lq2_ref, lk2_ref, mod_ref, lam_ref):
    c = c_ref[...]
    c_act = c * jax.nn.sigmoid(c)
    mod_ref[...] = jnp.dot(c_act, w_ref[0], precision=lax.Precision.HIGHEST,
                           preferred_element_type=F32) + b_ref[...]
    s1 = jnp.sum(lq1_ref[...] * lk1_ref[...], axis=-1, keepdims=True)
    s2 = jnp.sum(lq2_ref[...] * lk2_ref[...], axis=-1, keepdims=True)
    lam = jnp.exp(s1) - jnp.exp(s2) + LAMBDA_INIT
    lam_ref[...] = jnp.broadcast_to(lam, lam_ref.shape)


def _mod_call(c_pad, w_ada, b_ada, lq1, lk1, lq2, lk2):
    d = D_MODEL
    lam_spec = pl.BlockSpec((1, DA_QK_DIM), lambda j: (0, 0))
    return pl.pallas_call(
        _mod_kernel,
        grid=(3,),
        in_specs=[pl.BlockSpec((SUBLANES, d), lambda j: (0, 0)),
                  pl.BlockSpec((1, d, d), lambda j: (0, 0, j)),
                  pl.BlockSpec((1, d), lambda j: (0, j)),
                  lam_spec, lam_spec, lam_spec, lam_spec],
        out_specs=[pl.BlockSpec((SUBLANES, d), lambda j: (0, j)),
                   pl.BlockSpec((SUBLANES, LANES), lambda j: (0, 0))],
        out_shape=[jax.ShapeDtypeStruct((SUBLANES, 3 * d), F32),
                   jax.ShapeDtypeStruct((SUBLANES, LANES), F32)],
        compiler_params=_params("arbitrary"),
    )(c_pad, w_ada, b_ada, lq1, lk1, lq2, lk2)


def _rope_table_kernel(pos_ref, invf_ref, sgn_ref, cos_ref, sin_ref):
    ang = pos_ref[0] * invf_ref[...]
    cos_ref[0] = jnp.cos(ang)
    sin_ref[0] = jnp.sin(ang) * sgn_ref[...]


def _rope_table_call(pos_b, invf, sgn, bm):
    b, s, _ = pos_b.shape
    row = pl.BlockSpec((1, bm, LANES), lambda bi, i: (bi, i, 0))
    vec = pl.BlockSpec((1, LANES), lambda bi, i: (0, 0))
    shape = jax.ShapeDtypeStruct((b, s, LANES), F32)
    return pl.pallas_call(
        _rope_table_kernel,
        grid=(b, s // bm),
        in_specs=[row, vec, vec],
        out_specs=[row, row],
        out_shape=[shape, shape],
        compiler_params=_params("parallel", "parallel"),
    )(pos_b, invf, sgn)


def _proj_kernel(*refs, rope, scale, act, layout):
    if rope:
        h_ref, w_ref, cos_ref, sin_ref, o_ref = refs
    else:
        h_ref, w_ref, o_ref = refs
    acc = lax.dot_general(h_ref[0], w_ref[0].astype(BF16), (((1,), (1,)), ((), ())),
                          preferred_element_type=F32)
    if act in ("silu", "sigmoid"):
        gate = 0.5 * jnp.tanh(0.5 * acc) + 0.5
        acc = acc * gate if act == "silu" else gate
    if layout == "nat":
        o_ref[0] = acc.astype(BF16)
        return
    if rope:
        cos = cos_ref[0]
        sin = sin_ref[0]
    for hh in range(N_HEADS):
        t = acc[:, hh * HEAD_DIM:(hh + 1) * HEAD_DIM]
        if rope:
            t = t * cos + pltpu.roll(t, HEAD_DIM // 2, 1) * sin
        if scale != 1.0:
            t = t * scale
        if layout == "heads":
            o_ref[0, hh] = t.astype(BF16)
        else:
            tile = o_ref.shape[-1]
            for part in range(o_ref.shape[2]):
                o_ref[0, hh, part] = t[part * tile:(part + 1) * tile].T.astype(BF16)


def _proj_call(h, w_all, col_block, tables, bm, *, rope=False, scale=1.0, act=None, layout="nat"):
    b, s, d = h.shape
    n = D_MODEL
    in_specs = [pl.BlockSpec((1, bm, d), lambda bi, i: (bi, i, 0)),
                pl.BlockSpec((1, n, d), lambda bi, i: (0, col_block, 0))]
    args = [h, w_all]
    if rope:
        tab = pl.BlockSpec((1, bm, LANES), lambda bi, i: (bi, i, 0))
        in_specs += [tab, tab]
        args += list(tables)
    if layout == "nat":
        out_spec = pl.BlockSpec((1, bm, n), lambda bi, i: (bi, i, 0))
        out_shape = jax.ShapeDtypeStruct((b, s, n), BF16)
    elif layout == "heads":
        out_spec = pl.BlockSpec((1, N_HEADS, bm, HEAD_DIM), lambda bi, i: (bi, 0, i, 0))
        out_shape = jax.ShapeDtypeStruct((b, N_HEADS, s, HEAD_DIM), BF16)
    else:
        parts = bm // ATT_TILE
        out_spec = pl.BlockSpec((1, N_HEADS, parts, HEAD_DIM, ATT_TILE), lambda bi, i: (bi, 0, i, 0, 0))
        out_shape = jax.ShapeDtypeStruct((b, N_HEADS, s // ATT_TILE, HEAD_DIM, ATT_TILE), BF16)
    return pl.pallas_call(
        functools.partial(_proj_kernel, rope=rope, scale=scale, act=act, layout=layout),
        grid=(b, s // bm),
        in_specs=in_specs,
        out_specs=out_spec,
        out_shape=out_shape,
        compiler_params=_params("parallel", "parallel"),
    )(*args)


def _split3(v):
    hi = v.astype(BF16).astype(F32)
    r1 = v - hi
    mid = r1.astype(BF16).astype(F32)
    return hi, mid, r1 - mid


def _prenorm_cum_kernel(x_ref, g_ref, scale_ref, shift_ref, wf_ref, bf_ref, sel_ref, h_ref, kext_ref, tot_ref):
    t = ATT_TILE
    r = lax.broadcasted_iota(jnp.int32, (t, t), 0)
    c = lax.broadcasted_iota(jnp.int32, (t, t), 1)
    tri = jnp.where(c <= r, 1.0, 0.0).astype(BF16)
    for part in range(x_ref.shape[1] // t):
        rows = slice(part * t, (part + 1) * t)
        x = x_ref[0, rows]
        ms = jnp.mean(x * x, axis=-1, keepdims=True)
        y = x * lax.rsqrt(ms + NORM_EPS) * g_ref[...]
        h = (y * (1.0 + scale_ref[0]) + shift_ref[0]).astype(BF16)
        h_ref[0, rows] = h
        logit = lax.dot_general(h, wf_ref[...], (((1,), (1,)), ((), ())),
                                preferred_element_type=F32) + bf_ref[...]
        log_f = jnp.minimum(logit, 0.0) - jnp.log(1.0 + jnp.exp(-jnp.abs(logit)))
        terms = jnp.concatenate([v.astype(BF16) for v in _split3(log_f)], axis=1)
        sums = jnp.dot(tri, terms, preferred_element_type=F32)
        cum = sum(sums[:, n * LANES:(n + 1) * LANES] for n in range(BIAS_TERMS))
        bias = cum * (-LOG2E)
        tot_ref[0, part] = jnp.broadcast_to(-bias[t - 1:t, :], tot_ref.shape[2:])
        split = jnp.concatenate([v.astype(BF16) for v in _split3(bias)], axis=1)
        ext = jnp.dot(split, sel_ref[...], preferred_element_type=F32)
        for hh in range(N_HEADS):
            kext_ref[0, hh, rows] = ext[:, hh * LANES:(hh + 1) * LANES].astype(BF16)


def _bias_selection():
    sel = np.zeros((BIAS_TERMS, LANES, N_HEADS, LANES), np.float32)
    for n in range(BIAS_TERMS):
        for hh in range(N_HEADS):
            sel[n, hh, hh, n] = 1.0
    return sel.reshape(BIAS_TERMS * LANES, N_HEADS * LANES)


def _prenorm_cum_call(x, g, scale, shift, wf_pad, bf_pad, bm):
    b, s, d = x.shape
    sel = jnp.asarray(_bias_selection(), BF16)
    row = pl.BlockSpec((1, bm, d), lambda bi, i: (bi, i, 0))
    vec = pl.BlockSpec((1, 1, d), lambda bi, i: (bi, 0, 0))
    return pl.pallas_call(
        _prenorm_cum_kernel,
        grid=(b, s // bm),
        in_specs=[row, pl.BlockSpec((1, d), lambda bi, i: (0, 0)), vec, vec,
                  pl.BlockSpec((LANES, d), lambda bi, i: (0, 0)),
                  pl.BlockSpec((1, LANES), lambda bi, i: (0, 0)),
                  pl.BlockSpec(sel.shape, lambda bi, i: (0, 0))],
        out_specs=[row,
                   pl.BlockSpec((1, N_HEADS, bm, LANES), lambda bi, i: (bi, 0, i, 0)),
                   pl.BlockSpec((1, bm // ATT_TILE, SUBLANES, LANES), lambda bi, i: (bi, i, 0, 0))],
        out_shape=[jax.ShapeDtypeStruct((b, s, d), BF16),
                   jax.ShapeDtypeStruct((b, N_HEADS, s, LANES), BF16),
                   jax.ShapeDtypeStruct((b, s // ATT_TILE, SUBLANES, LANES), F32)],
        compiler_params=_params("parallel", "parallel"),
    )(x, g, scale, shift, wf_pad, bf_pad, sel)


PIPE_SLOTS = 3
PV_LAG = 2
DENOM_ROWS = 16


def _attend(i, n_maps, scores, next_scores, vt_ref, visible, frame_shift, finish, m, acc, *bufs):
    s_buf, c_buf, p_buf, a_buf = (bufs[k * PIPE_SLOTS:(k + 1) * PIPE_SLOTS] for k in range(4))
    hd = acc.shape[1] - DENOM_ROWS
    ones = jnp.ones((DENOM_ROWS, vt_ref.shape[-1]), BF16)
    m[...] = jnp.full(m.shape, NEG_BIG, F32)

    def put_scores(slot, tiles):
        for n, s in enumerate(tiles):
            s_buf[slot][n] = s
            c_buf[slot][n] = jnp.max(s, axis=0, keepdims=True)

    def softmax(s, s_max, n):
        m_prev = m[n]
        m_new = jnp.maximum(m_prev, s_max)
        alpha = jnp.exp2(m_prev - m_new)
        p = jnp.exp2(s - m_new).astype(BF16)
        return m_new, alpha, p

    def value_product(prev, alpha, j, p):
        vt_ones = jnp.concatenate([vt_ref[0, 0, j], ones], axis=0)
        return alpha * prev + jnp.dot(vt_ones, p, preferred_element_type=F32)

    def step(j, slot, prefetch=True):
        if prefetch:
            put_scores((slot + 2) % PIPE_SLOTS, scores(jnp.minimum(j + 2, i)))
        old = (slot - PV_LAG) % PIPE_SLOTS
        for n in range(n_maps):
            acc[n] = value_product(acc[n], a_buf[old][n], jnp.maximum(j - PV_LAG, 0), p_buf[old][n])
            m_new, alpha, p = softmax(s_buf[slot][n], c_buf[slot][n], n)
            p_buf[slot][n] = p
            a_buf[slot][n] = alpha
            m[n] = m_new + frame_shift(j)

    def last(slot):
        results = []
        for n in range(n_maps):
            out = acc[n]
            for back in range(PV_LAG, 0, -1):
                old = (slot - back) % PIPE_SLOTS
                out = value_product(out, a_buf[old][n], jnp.maximum(i - back, 0), p_buf[old][n])
            s = jnp.where(visible(), s_buf[slot][n], -jnp.inf)
            _, alpha, p = softmax(s, jnp.max(s, axis=0, keepdims=True), n)
            out = value_product(out, alpha, i, p)
            results.append((out[:hd], jnp.mean(out[hd:], axis=0, keepdims=True)))
        put_scores(0, next_scores(0))
        put_scores(1, next_scores(1))
        finish(results)

    @pl.when(i == 0)
    def _():
        acc[...] = jnp.zeros(acc.shape, F32)
        for slot in range(PIPE_SLOTS - PV_LAG, PIPE_SLOTS):
            p_buf[slot][...] = jnp.zeros(p_buf[slot].shape, BF16)
            a_buf[slot][...] = jnp.ones(a_buf[slot].shape, F32)
        put_scores(0, scores(0))

    def rotation(jj, carry):
        for slot in range(PIPE_SLOTS):
            step(PIPE_SLOTS * jj + slot, slot)
        return carry

    trip = 2 * PIPE_SLOTS

    def two_rotations(u, carry):
        return rotation(2 * u + 1, rotation(2 * u, carry))

    lax.fori_loop(0, i // trip, two_rotations, 0)

    for rem in range(trip):
        @pl.when(i % trip == rem)
        def _(rem=rem):
            for k in range(rem):
                step(i - rem + k, k % PIPE_SLOTS, prefetch=k + 2 <= rem)
            last(rem % PIPE_SLOTS)


def _attend_scratch(n_maps, hd, tk, tq):
    return ([pltpu.VMEM((n_maps, 1, tq), F32),
             pltpu.VMEM((n_maps, hd + DENOM_ROWS, tq), F32)]
            + [pltpu.VMEM((n_maps, tk, tq), F32)] * PIPE_SLOTS
            + [pltpu.VMEM((n_maps, 1, tq), F32)] * PIPE_SLOTS
            + [pltpu.VMEM((n_maps, tk, tq), BF16)] * PIPE_SLOTS
            + [pltpu.VMEM((n_maps, 1, tq), F32)] * PIPE_SLOTS)


def _diff_kernel(lam_ref, qt_ref, k_ref, vt_ref, g_ref, o_ref, *scratch):
    i = pl.program_id(2)
    nq, t = qt_ref.shape[2], qt_ref.shape[-1]

    def both_maps(block):
        qt = qt_ref[0, 0, block]
        row = lax.broadcasted_iota(jnp.int32, qt.shape, 0)
        map1 = (row // ROPE_HALF) % 2 == 0
        zero = jnp.zeros_like(qt)
        return jnp.concatenate([jnp.where(map1, qt, zero), jnp.where(map1, zero, qt)], axis=1)

    q12 = both_maps(i)

    def scores(j):
        return (jnp.dot(k_ref[0, 0, j], q12, preferred_element_type=F32),)

    def next_scores(j):
        return (jnp.dot(k_ref[0, 0, j], both_maps(jnp.minimum(i + 1, nq - 1)), preferred_element_type=F32),)

    def visible():
        kc = lax.broadcasted_iota(jnp.int32, (t, 2 * t), 0) // CHUNK
        qc = (lax.broadcasted_iota(jnp.int32, (t, 2 * t), 1) % t) // CHUNK
        return kc <= qc

    def finish(results):
        ((out, l),) = results
        o = out / l
        o = o[:, :t] - lam_ref[0] * o[:, t:]
        ms = jnp.mean(o * o, axis=0, keepdims=True)
        g = jnp.concatenate([g_ref[...]] * (t // LANES), axis=1)
        y = o * lax.rsqrt(ms + SUBLN_EPS) * g
        o_ref[0, 0] = y.T.astype(BF16)

    _attend(i, 1, scores, next_scores, vt_ref, visible, lambda j: 0.0, finish, *scratch)


def _diff_call(lam, qt, k, vt, g_rep):
    b, nh, nq, hd, t = qt.shape
    return pl.pallas_call(
        _diff_kernel,
        grid=(b, nh, nq),
        in_specs=[pl.BlockSpec(memory_space=pltpu.SMEM),
                  pl.BlockSpec((1, 1, nq, hd, t), lambda bi, h, i: (bi, h, 0, 0, 0)),
                  pl.BlockSpec((1, 1, nq, t, hd), lambda bi, h, i: (bi, h, 0, 0, 0)),
                  pl.BlockSpec((1, 1, nq, hd, t), lambda bi, h, i: (bi, h, 0, 0, 0)),
                  pl.BlockSpec((hd, LANES), lambda bi, h, i: (0, 0))],
        out_specs=pl.BlockSpec((1, 1, t, hd), lambda bi, h, i: (bi, h, i, 0)),
        out_shape=jax.ShapeDtypeStruct((b, nh, nq * t, hd), BF16),
        scratch_shapes=_attend_scratch(1, hd, t, 2 * t),
        compiler_params=_params("parallel", "parallel", "arbitrary"),
    )(lam, qt, k, vt, g_rep)


def _fox_kernel(tot_ref, qt_ref, k_ref, kext_ref, vt_ref, o_ref, *scratch):
    bi, h, i = pl.program_id(0), pl.program_id(1), pl.program_id(2)
    nq, t = qt_ref.shape[2], qt_ref.shape[-1]
    tot_base = (bi * N_HEADS + h) * nq

    def with_ones(block):
        qt = qt_ref[0, 0, block]
        row = lax.broadcasted_iota(jnp.int32, qt.shape, 0)
        return jnp.concatenate([qt, jnp.where(row < BIAS_TERMS, 1.0, 0.0).astype(BF16)], axis=0)

    q_cat = with_ones(i)

    def scores_of(q, j):
        k_cat = jnp.concatenate([k_ref[0, 0, j], kext_ref[0, 0, j]], axis=1)
        return (jnp.dot(k_cat, q, preferred_element_type=F32),)

    def scores(j):
        return scores_of(q_cat, j)

    def next_scores(j):
        return scores_of(with_ones(jnp.minimum(i + 1, nq - 1)), j)

    def visible():
        return lax.broadcasted_iota(jnp.int32, (t, t), 0) <= lax.broadcasted_iota(jnp.int32, (t, t), 1)

    def finish(results):
        ((o, l),) = results
        o_ref[0, 0] = (o / l).T.astype(BF16)

    _attend(i, 1, scores, next_scores, vt_ref, visible, lambda j: tot_ref[tot_base + j], finish, *scratch)


def _fox_call(tot, qt, k, kext, vt):
    b, nh, nq, hd, t = qt.shape
    return pl.pallas_call(
        _fox_kernel,
        grid=(b, nh, nq),
        in_specs=[pl.BlockSpec(memory_space=pltpu.SMEM),
                  pl.BlockSpec((1, 1, nq, hd, t), lambda bi, h, i: (bi, h, 0, 0, 0)),
                  pl.BlockSpec((1, 1, nq, t, hd), lambda bi, h, i: (bi, h, 0, 0, 0)),
                  pl.BlockSpec((1, 1, nq, t, LANES), lambda bi, h, i: (bi, h, 0, 0, 0)),
                  pl.BlockSpec((1, 1, nq, hd, t), lambda bi, h, i: (bi, h, 0, 0, 0))],
        out_specs=pl.BlockSpec((1, 1, t, hd), lambda bi, h, i: (bi, h, i, 0)),
        out_shape=jax.ShapeDtypeStruct((b, nh, nq * t, hd), BF16),
        scratch_shapes=_attend_scratch(1, hd, t, t),
        compiler_params=_params("parallel", "parallel", "arbitrary"),
    )(tot, qt, k, kext, vt)


def _out_kernel(oa_ref, za_ref, ob_ref, zb_ref, ma_ref, mb_ref, x_ref, gate_ref,
                wa_ref, wb_ref, wo_ref, gf_ref, o_ref):
    def gated(o_ref_, z_ref_):
        o = jnp.concatenate([o_ref_[0, hh] for hh in range(N_HEADS)], axis=1)
        return (o.astype(F32) * z_ref_[0].astype(F32)).astype(BF16)

    ya = jnp.dot(gated(oa_ref, za_ref), wa_ref[...], preferred_element_type=F32)
    yb = jnp.dot(gated(ob_ref, zb_ref), wb_ref[...], preferred_element_type=F32)
    merged = ma_ref[0].astype(F32) * ya + mb_ref[0].astype(F32) * yb
    out = jnp.dot(merged.astype(BF16), wo_ref[...], preferred_element_type=F32)
    xn = x_ref[0] + gate_ref[0] * out
    ms = jnp.mean(xn * xn, axis=-1, keepdims=True)
    o_ref[0] = xn * lax.rsqrt(ms + NORM_EPS) * gf_ref[...]


def _out_call(oa, za, ob, zb, ma, mb, x, gate, wa, wb, wo, gf, bm):
    b, s, d = x.shape
    heads = pl.BlockSpec((1, N_HEADS, bm, HEAD_DIM), lambda bi, i: (bi, 0, i, 0))
    row = pl.BlockSpec((1, bm, d), lambda bi, i: (bi, i, 0))
    wspec = pl.BlockSpec((d, d), lambda bi, i: (0, 0))
    return pl.pallas_call(
        _out_kernel,
        grid=(b, s // bm),
        in_specs=[heads, row, heads, row, row, row, row,
                  pl.BlockSpec((1, 1, d), lambda bi, i: (bi, 0, 0)),
                  wspec, wspec, wspec,
                  pl.BlockSpec((1, d), lambda bi, i: (0, 0))],
        out_specs=row,
        out_shape=jax.ShapeDtypeStruct((b, s, d), F32),
        compiler_params=_params("parallel", "parallel"),
    )(oa, za, ob, zb, ma, mb, x, gate, wa, wb, wo, gf)


def kernel(x, c, positions, w_ada, b_ada, g_norm, w_in, b_forget, lambda_q1, lambda_k1, lambda_q2,
           lambda_k2, g_subln, w_branch_a, w_branch_b, w_out, g_final):
    b, s, d = x.shape
    assert d == D_MODEL and s % ATT_TILE == 0 and b <= SUBLANES
    assert w_ada.shape[0] == 1, "single-layer block"
    t = ATT_TILE
    nq = s // t
    n_main = 8 * D_MODEL

    c_pad = jnp.zeros((SUBLANES, d), F32).at[:b].set(c)
    w_t = jnp.swapaxes(w_in, 1, 2)
    w_qk = w_t[:, :2 * D_MODEL].reshape(2 * N_HEADS, 2, 2, ROPE_HALF, d).transpose(0, 2, 1, 3, 4)
    w_qk = w_qk.reshape(1, 2 * D_MODEL, d)
    w_merge = w_t[:, n_main + N_HEADS:]
    wf_pad = jnp.zeros((LANES, d), F32).at[:N_HEADS].set(w_t[0, n_main:n_main + N_HEADS]).astype(BF16)
    bf_pad = jnp.zeros((1, LANES), F32).at[0, :N_HEADS].set(b_forget[0])
    inv_freq = ROPE_THETA ** (-jnp.arange(ROPE_HALF, dtype=F32) / ROPE_HALF)
    invf = jnp.tile(inv_freq, LANES // ROPE_HALF)[None, :]
    sgn = jnp.concatenate([-jnp.ones((HEAD_DIM // 2,), F32), jnp.ones((HEAD_DIM // 2,), F32)])[None, :]
    pos_b = jnp.broadcast_to(positions.astype(F32)[:, :, None], (b, s, LANES))
    g_rep = jnp.broadcast_to((g_subln[0] * (1.0 - LAMBDA_INIT))[:, None], (HEAD_DIM, LANES))

    mod, lam_tile = _mod_call(c_pad, w_ada, b_ada, lambda_q1, lambda_k1, lambda_q2, lambda_k2)
    shift = mod[:b, None, 0:d]
    scale = mod[:b, None, d:2 * d]
    gate = mod[:b, None, 2 * d:3 * d]
    lam = lam_tile[0, :1]

    h, kext, tot_tile = _prenorm_cum_call(x, g_norm, scale, shift, wf_pad, bf_pad, 2 * t if nq % 2 == 0 else t)
    tables = _rope_table_call(pos_b, invf, sgn, t)

    pm = PROJ_ROWS if s % PROJ_ROWS == 0 else t
    pw = PROJ_ROWS_LIGHT if s % PROJ_ROWS_LIGHT == 0 else pm
    proj = functools.partial(_proj_call, h)
    qa_t = proj(w_qk, 0, tables, pm, rope=True, scale=LOG2E * DA_QK_DIM ** -0.5, layout="heads_t")
    ka = proj(w_qk, 1, tables, pw, rope=True, layout="heads")
    va_t = proj(w_t, 2, None, pw, layout="heads_t")
    za = proj(w_t, 3, None, pm, act="silu")
    qb_t = proj(w_t, 4, None, pw, scale=LOG2E * HEAD_DIM ** -0.5, layout="heads_t")
    kb = proj(w_t, 5, None, pw, layout="heads")
    vb_t = proj(w_t, 6, None, pw, layout="heads_t")
    zb = proj(w_t, 7, None, pm, act="silu")
    ma = proj(w_merge, 0, None, pm, act="sigmoid")
    mb = proj(w_merge, 1, None, pm, act="sigmoid")

    tot = tot_tile[:, :, 0, :N_HEADS].transpose(0, 2, 1).reshape(-1)

    def tiles(a):
        return a.reshape(b, N_HEADS, nq, t, a.shape[-1])

    oa = _diff_call(lam, qa_t, tiles(ka), va_t, g_rep)
    ob = _fox_call(tot, qb_t, tiles(kb), tiles(kext), vb_t)

    return _out_call(oa, za, ob, zb, ma, mb, x, gate,
                     w_branch_a[0].astype(BF16), w_branch_b[0].astype(BF16), w_out[0].astype(BF16),
                     g_final[None, :], t)
```

```python
import functools
import math

import jax
import jax.numpy as jnp
import numpy as np
from jax import lax
from jax.experimental import pallas as pl
from jax.experimental.pallas import tpu as pltpu

F32 = jnp.float32
BF16 = jnp.bfloat16

D_MODEL = 1024
N_HEADS = 8
HEAD_DIM = 128
DA_QK_DIM = 64
ROPE_HALF = DA_QK_DIM // 2
CHUNK = 64
ROPE_THETA = 10000.0
NORM_EPS = 1e-6
SUBLN_EPS = 1e-5
LAMBDA_INIT = 0.8 - 0.6 * math.exp(-0.3 * 0)
LOG2E = math.log2(math.e)

LANES = 128
SUBLANES = 8
ATT_TILE = 512
PROJ_ROWS = 1024
BIAS_TERMS = 3
VMEM_LIMIT = 48 * 1024 * 1024
NEG_BIG = -1e30


def _params(*sem):
    return pltpu.CompilerParams(dimension_semantics=sem, vmem_limit_bytes=VMEM_LIMIT)


def _mod_kernel(c_ref, w_ref, b_ref, lq1_ref, lk1_ref, lq2_ref, lk2_ref, mod_ref, lam_ref):
    c = c_ref[...]
    c_act = c * jax.nn.sigmoid(c)
    mod_ref[...] = jnp.dot(c_act, w_ref[0], precision=lax.Precision.HIGHEST,
                           preferred_element_type=F32) + b_ref[...]
    s1 = jnp.sum(lq1_ref[...] * lk1_ref[...], axis=-1, keepdims=True)
    s2 = jnp.sum(lq2_ref[...] * lk2_ref[...], axis=-1, keepdims=True)
    lam = jnp.exp(s1) - jnp.exp(s2) + LAMBDA_INIT
    lam_ref[...] = jnp.broadcast_to(lam, lam_ref.shape)


def _mod_call(c_pad, w_ada, b_ada, lq1, lk1, lq2, lk2):
    d = D_MODEL
    lam_spec = pl.BlockSpec((1, DA_QK_DIM), lambda j: (0, 0))
    return pl.pallas_call(
        _mod_kernel,
        grid=(3,),
        in_specs=[pl.BlockSpec((SUBLANES, d), lambda j: (0, 0)),
                  pl.BlockSpec((1, d, d), lambda j: (0, 0, j)),
                  pl.BlockSpec((1, d), lambda j: (0, j)),
                  lam_spec, lam_spec, lam_spec, lam_spec],
        out_specs=[pl.BlockSpec((SUBLANES, d), lambda j: (0, j)),
                   pl.BlockSpec((SUBLANES, LANES), lambda j: (0, 0))],
        out_shape=[jax.ShapeDtypeStruct((SUBLANES, 3 * d), F32),
                   jax.ShapeDtypeStruct((SUBLANES, LANES), F32)],
        compiler_params=_params("arbitrary"),
    )(c_pad, w_ada, b_ada, lq1, lk1, lq2, lk2)


def _rope_table_kernel(pos_ref, invf_ref, sgn_ref, cos_ref, sin_ref):
    ang = pos_ref[0] * invf_ref[...]
    cos_ref[0] = jnp.cos(ang)
    sin_ref[0] = jnp.sin(ang) * sgn_ref[...]


def _rope_table_call(pos_b, invf, sgn, bm):
    b, s, _ = pos_b.shape
    row = pl.BlockSpec((1, bm, LANES), lambda bi, i: (bi, i, 0))
    vec = pl.BlockSpec((1, LANES), lambda bi, i: (0, 0))
    shape = jax.ShapeDtypeStruct((b, s, LANES), F32)
    return pl.pallas_call(
        _rope_table_kernel,
        grid=(b, s // bm),
        in_specs=[row, vec, vec],
        out_specs=[row, row],
        out_shape=[shape, shape],
        compiler_params=_params("parallel", "parallel"),
    )(pos_b, invf, sgn)


def _proj_kernel(*refs, rope, scale, act, layout):
    if rope:
        h_ref, w_ref, cos_ref, sin_ref, o_ref = refs
    else:
        h_ref, w_ref, o_ref = refs
    if isinstance(scale, tuple):
        g = pl.program_id(0)
        apply_scale = any(v != 1.0 for v in scale)
        slab_scale = jnp.float32(scale[0])
        for k in range(1, len(scale)):
            slab_scale = jnp.where(g == k, jnp.float32(scale[k]), slab_scale)
        scale = slab_scale
        o_ref = o_ref.at[0]
    else:
        apply_scale = scale != 1.0
    acc = lax.dot_general(h_ref[0], w_ref[0].astype(BF16), (((1,), (1,)), ((), ())),
                          preferred_element_type=F32)
    if act in ("silu", "sigmoid"):
        gate = 0.5 * jnp.tanh(0.5 * acc) + 0.5
        acc = acc * gate if act == "silu" else gate
    if layout == "nat":
        o_ref[0] = acc.astype(BF16)
        return
    if rope:
        cos = cos_ref[0]
        sin = sin_ref[0]
    for hh in range(N_HEADS):
        t = acc[:, hh * HEAD_DIM:(hh + 1) * HEAD_DIM]
        if rope:
            t = t * cos + pltpu.roll(t, HEAD_DIM // 2, 1) * sin
        if apply_scale:
            t = t * scale
        if layout == "heads":
            o_ref[0, hh] = t.astype(BF16)
        else:
            tile = o_ref.shape[-1]
            for part in range(o_ref.shape[2]):
                o_ref[0, hh, part] = t[part * tile:(part + 1) * tile].T.astype(BF16)


def _proj_call(h, w_all, col_block, tables, bm, *, rope=False, scale=1.0, act=None, layout="nat", col_stride=0):
    if isinstance(scale, tuple):
        return _proj_slabs_call(h, w_all, col_block, col_stride, bm, scale, act, layout)
    b, s, d = h.shape
    n = D_MODEL
    in_specs = [pl.BlockSpec((1, bm, d), lambda bi, i: (bi, i, 0)),
                pl.BlockSpec((1, n, d), lambda bi, i: (0, col_block, 0))]
    args = [h, w_all]
    if rope:
        tab = pl.BlockSpec((1, bm, LANES), lambda bi, i: (bi, i, 0))
        in_specs += [tab, tab]
        args += list(tables)
    if layout == "nat":
        out_spec = pl.BlockSpec((1, bm, n), lambda bi, i: (bi, i, 0))
        out_shape = jax.ShapeDtypeStruct((b, s, n), BF16)
    elif layout == "heads":
        out_spec = pl.BlockSpec((1, N_HEADS, bm, HEAD_DIM), lambda bi, i: (bi, 0, i, 0))
        out_shape = jax.ShapeDtypeStruct((b, N_HEADS, s, HEAD_DIM), BF16)
    else:
        parts = bm // ATT_TILE
        out_spec = pl.BlockSpec((1, N_HEADS, parts, HEAD_DIM, ATT_TILE), lambda bi, i: (bi, 0, i, 0, 0))
        out_shape = jax.ShapeDtypeStruct((b, N_HEADS, s // ATT_TILE, HEAD_DIM, ATT_TILE), BF16)
    return pl.pallas_call(
        functools.partial(_proj_kernel, rope=rope, scale=scale, act=act, layout=layout),
        grid=(b, s // bm),
        in_specs=in_specs,
        out_specs=out_spec,
        out_shape=out_shape,
        compiler_params=_params("parallel", "parallel"),
    )(*args)


def _proj_slabs_call(h, w_all, col_block, col_stride, bm, scales, act, layout):
    b, s, d = h.shape
    n, groups = D_MODEL, len(scales)
    in_specs = [pl.BlockSpec((1, bm, d), lambda g, bi, i: (bi, i, 0)),
                pl.BlockSpec((1, n, d), lambda g, bi, i: (0, col_block + col_stride * g, 0))]
    if layout == "nat":
        out_spec = pl.BlockSpec((1, 1, bm, n), lambda g, bi, i: (g, bi, i, 0))
        out_shape = jax.ShapeDtypeStruct((groups, b, s, n), BF16)
    else:
        parts = bm // ATT_TILE
        out_spec = pl.BlockSpec((1, 1, N_HEADS, parts, HEAD_DIM, ATT_TILE), lambda g, bi, i: (g, bi, 0, i, 0, 0))
        out_shape = jax.ShapeDtypeStruct((groups, b, N_HEADS, s // ATT_TILE, HEAD_DIM, ATT_TILE), BF16)
    return pl.pallas_call(
        functools.partial(_proj_kernel, rope=False, scale=scales, act=act, layout=layout),
        grid=(groups, b, s // bm),
        in_specs=in_specs,
        out_specs=out_spec,
        out_shape=out_shape,
        compiler_params=_params("parallel", "parallel", "parallel"),
    )(h, w_all)


def _split3(v):
    hi = v.astype(BF16).astype(F32)
    r1 = v - hi
    mid = r1.astype(BF16).astype(F32)
    return hi, mid, r1 - mid


def _prenorm_cum_kernel(x_ref, g_ref, scale_ref, shift_ref, wf_ref, bf_ref, sel_ref, h_ref, kext_ref, tot_ref):
    t = ATT_TILE
    r = lax.broadcasted_iota(jnp.int32, (t, t), 0)
    c = lax.broadcasted_iota(jnp.int32, (t, t), 1)
    tri = jnp.where(c <= r, 1.0, 0.0).astype(BF16)
    for part in range(x_ref.shape[1] // t):
        rows = slice(part * t, (part + 1) * t)
        x = x_ref[0, rows]
        ms = jnp.mean(x * x, axis=-1, keepdims=True)
        y = x * lax.rsqrt(ms + NORM_EPS) * g_ref[...]
        h = (y * (1.0 + scale_ref[0]) + shift_ref[0]).astype(BF16)
        h_ref[0, rows] = h
        logit = lax.dot_general(h, wf_ref[...], (((1,), (1,)), ((), ())),
                                preferred_element_type=F32) + bf_ref[...]
        log_f = jnp.minimum(logit, 0.0) - jnp.log(1.0 + jnp.exp(-jnp.abs(logit)))
        terms = jnp.concatenate([v.astype(BF16) for v in _split3(log_f)], axis=1)
        sums = jnp.dot(tri, terms, preferred_element_type=F32)
        cum = sum(sums[:, n * LANES:(n + 1) * LANES] for n in range(BIAS_TERMS))
        bias = cum * (-LOG2E)
        tot_ref[0, part] = jnp.broadcast_to(-bias[t - 1:t, :], tot_ref.shape[2:])
        split = jnp.concatenate([v.astype(BF16) for v in _split3(bias)], axis=1)
        ext = jnp.dot(split, sel_ref[...], preferred_element_type=F32)
        for hh in range(N_HEADS):
            kext_ref[0, hh, rows] = ext[:, hh * LANES:(hh + 1) * LANES].astype(BF16)


def _bias_selection():
    sel = np.zeros((BIAS_TERMS, LANES, N_HEADS, LANES), np.float32)
    for n in range(BIAS_TERMS):
        for hh in range(N_HEADS):
            sel[n, hh, hh, n] = 1.0
    return sel.reshape(BIAS_TERMS * LANES, N_HEADS * LANES)


def _prenorm_cum_call(x, g, scale, shift, wf_pad, bf_pad, bm):
    b, s, d = x.shape
    sel = jnp.asarray(_bias_selection(), BF16)
    row = pl.BlockSpec((1, bm, d), lambda bi, i: (bi, i, 0))
    vec = pl.BlockSpec((1, 1, d), lambda bi, i: (bi, 0, 0))
    return pl.pallas_call(
        _prenorm_cum_kernel,
        grid=(b, s // bm),
        in_specs=[row, pl.BlockSpec((1, d), lambda bi, i: (0, 0)), vec, vec,
                  pl.BlockSpec((LANES, d), lambda bi, i: (0, 0)),
                  pl.BlockSpec((1, LANES), lambda bi, i: (0, 0)),
                  pl.BlockSpec(sel.shape, lambda bi, i: (0, 0))],
        out_specs=[row,
                   pl.BlockSpec((1, N_HEADS, bm, LANES), lambda bi, i: (bi, 0, i, 0)),
                   pl.BlockSpec((1, bm // ATT_TILE, SUBLANES, LANES), lambda bi, i: (bi, i, 0, 0))],
        out_shape=[jax.ShapeDtypeStruct((b, s, d), BF16),
                   jax.ShapeDtypeStruct((b, N_HEADS, s, LANES), BF16),
                   jax.ShapeDtypeStruct((b, s // ATT_TILE, SUBLANES, LANES), F32)],
        compiler_params=_params("parallel", "parallel"),
    )(x, g, scale, shift, wf_pad, bf_pad, sel)


def _slab_spec(operand, block, index_map):
    if isinstance(operand, tuple):
        slab = operand[1]
        return operand[0], pl.BlockSpec((None,) + block, lambda *g: (slab,) + index_map(*g))
    return operand, pl.BlockSpec(block, index_map)


PIPE_SLOTS = 3
PV_LAG = 2
DENOM_ROWS = 16


def _attend(i, n_maps, scores, next_scores, vt_ref, visible, frame_shift, finish, m, acc, *bufs):
    s_buf, c_buf, p_buf, a_buf = (bufs[k * PIPE_SLOTS:(k + 1) * PIPE_SLOTS] for k in range(4))
    hd = acc.shape[1] - DENOM_ROWS
    ones = jnp.ones((DENOM_ROWS, vt_ref.shape[-1]), BF16)
    m[...] = jnp.full(m.shape, NEG_BIG, F32)

    def put_scores(slot, tiles):
        for n, s in enumerate(tiles):
            s_buf[slot][n] = s
            c_buf[slot][n] = jnp.max(s, axis=0, keepdims=True)

    def softmax(s, s_max, n):
        m_prev = m[n]
        m_new = jnp.maximum(m_prev, s_max)
        alpha = jnp.exp2(m_prev - m_new)
        p = jnp.exp2(s - m_new).astype(BF16)
        return m_new, alpha, p

    def value_product(prev, alpha, j, p):
        vt_ones = jnp.concatenate([vt_ref[0, 0, j], ones], axis=0)
        return alpha * prev + jnp.dot(vt_ones, p, preferred_element_type=F32)

    def step(j, slot, prefetch=True):
        if prefetch:
            put_scores((slot + 2) % PIPE_SLOTS, scores(jnp.minimum(j + 2, i)))
        old = (slot - PV_LAG) % PIPE_SLOTS
        for n in range(n_maps):
            acc[n] = value_product(acc[n], a_buf[old][n], jnp.maximum(j - PV_LAG, 0), p_buf[old][n])
            m_new, alpha, p = softmax(s_buf[slot][n], c_buf[slot][n], n)
            p_buf[slot][n] = p
            a_buf[slot][n] = alpha
            m[n] = m_new + frame_shift(j)

    def last(slot):
        results = []
        for n in range(n_maps):
            out = acc[n]
            for back in range(PV_LAG, 0, -1):
                old = (slot - back) % PIPE_SLOTS
                out = value_product(out, a_buf[old][n], jnp.maximum(i - back, 0), p_buf[old][n])
            s = jnp.where(visible(), s_buf[slot][n], -jnp.inf)
            _, alpha, p = softmax(s, jnp.max(s, axis=0, keepdims=True), n)
            out = value_product(out, alpha, i, p)
            results.append((out[:hd], jnp.mean(out[hd:], axis=0, keepdims=True)))
        put_scores(0, next_scores(0))
        put_scores(1, next_scores(1))
        finish(results)

    @pl.when(i == 0)
    def _():
        acc[...] = jnp.zeros(acc.shape, F32)
        for slot in range(PIPE_SLOTS - PV_LAG, PIPE_SLOTS):
            p_buf[slot][...] = jnp.zeros(p_buf[slot].shape, BF16)
            a_buf[slot][...] = jnp.ones(a_buf[slot].shape, F32)
        put_scores(0, scores(0))

    def rotation(jj, carry):
        for slot in range(PIPE_SLOTS):
            step(PIPE_SLOTS * jj + slot, slot)
        return carry

    trip = 2 * PIPE_SLOTS

    def two_rotations(u, carry):
        return rotation(2 * u + 1, rotation(2 * u, carry))

    lax.fori_loop(0, i // trip, two_rotations, 0)

    for rem in range(trip):
        @pl.when(i % trip == rem)
        def _(rem=rem):
            for k in range(rem):
                step(i - rem + k, k % PIPE_SLOTS, prefetch=k + 2 <= rem)
            last(rem % PIPE_SLOTS)


def _attend_scratch(n_maps, hd, tk, tq):
    return ([pltpu.VMEM((n_maps, 1, tq), F32),
             pltpu.VMEM((n_maps, hd + DENOM_ROWS, tq), F32)]
            + [pltpu.VMEM((n_maps, tk, tq), F32)] * PIPE_SLOTS
            + [pltpu.VMEM((n_maps, 1, tq), F32)] * PIPE_SLOTS
            + [pltpu.VMEM((n_maps, tk, tq), BF16)] * PIPE_SLOTS
            + [pltpu.VMEM((n_maps, 1, tq), F32)] * PIPE_SLOTS)


def _diff_kernel(lam_ref, qt_ref, k_ref, vt_ref, g_ref, o_ref, *scratch):
    i = pl.program_id(2)
    nq, t = qt_ref.shape[2], qt_ref.shape[-1]

    def both_maps(block):
        qt = qt_ref[0, 0, block]
        row = lax.broadcasted_iota(jnp.int32, qt.shape, 0)
        map1 = (row // ROPE_HALF) % 2 == 0
        zero = jnp.zeros_like(qt)
        return jnp.concatenate([jnp.where(map1, qt, zero), jnp.where(map1, zero, qt)], axis=1)

    q12 = both_maps(i)

    def scores(j):
        return (jnp.dot(k_ref[0, 0, j], q12, preferred_element_type=F32),)

    def next_scores(j):
        return (jnp.dot(k_ref[0, 0, j], both_maps(jnp.minimum(i + 1, nq - 1)), preferred_element_type=F32),)

    def visible():
        kc = lax.broadcasted_iota(jnp.int32, (t, 2 * t), 0) // CHUNK
        qc = (lax.broadcasted_iota(jnp.int32, (t, 2 * t), 1) % t) // CHUNK
        return kc <= qc

    def finish(results):
        ((out, l),) = results
        o = out / l
        o = o[:, :t] - lam_ref[0] * o[:, t:]
        ms = jnp.mean(o * o, axis=0, keepdims=True)
        g = jnp.concatenate([g_ref[...]] * (t // LANES), axis=1)
        y = o * lax.rsqrt(ms + SUBLN_EPS) * g
        o_ref[0, 0] = y.T.astype(BF16)

    _attend(i, 1, scores, next_scores, vt_ref, visible, lambda j: 0.0, finish, *scratch)


def _diff_call(lam, qt, k, vt, g_rep):
    b, nh, nq, hd, t = qt.shape
    head = lambda bi, h, i: (bi, h, 0, 0, 0)
    vt, vt_spec = _slab_spec(vt, (1, 1, nq, hd, t), head)
    return pl.pallas_call(
        _diff_kernel,
        grid=(b, nh, nq),
        in_specs=[pl.BlockSpec(memory_space=pltpu.SMEM),
                  pl.BlockSpec((1, 1, nq, hd, t), head),
                  pl.BlockSpec((1, 1, nq, t, hd), head),
                  vt_spec,
                  pl.BlockSpec((hd, LANES), lambda bi, h, i: (0, 0))],
        out_specs=pl.BlockSpec((1, 1, t, hd), lambda bi, h, i: (bi, h, i, 0)),
        out_shape=jax.ShapeDtypeStruct((b, nh, nq * t, hd), BF16),
        scratch_shapes=_attend_scratch(1, hd, t, 2 * t),
        compiler_params=_params("parallel", "parallel", "arbitrary"),
    )(lam, qt, k, vt, g_rep)


def _fox_kernel(tot_ref, qt_ref, k_ref, kext_ref, vt_ref, o_ref, *scratch):
    bi, h, i = pl.program_id(0), pl.program_id(1), pl.program_id(2)
    nq, t = qt_ref.shape[2], qt_ref.shape[-1]
    tot_base = (bi * N_HEADS + h) * nq

    def with_ones(block):
        qt = qt_ref[0, 0, block]
        row = lax.broadcasted_iota(jnp.int32, qt.shape, 0)
        return jnp.concatenate([qt, jnp.where(row < BIAS_TERMS, 1.0, 0.0).astype(BF16)], axis=0)

    q_cat = with_ones(i)

    def scores_of(q, j):
        k_cat = jnp.concatenate([k_ref[0, 0, j], kext_ref[0, 0, j]], axis=1)
        return (jnp.dot(k_cat, q, preferred_element_type=F32),)

    def scores(j):
        return scores_of(q_cat, j)

    def next_scores(j):
        return scores_of(with_ones(jnp.minimum(i + 1, nq - 1)), j)

    def visible():
        return lax.broadcasted_iota(jnp.int32, (t, t), 0) <= lax.broadcasted_iota(jnp.int32, (t, t), 1)

    def finish(results):
        ((o, l),) = results
        o_ref[0, 0] = (o / l).T.astype(BF16)

    _attend(i, 1, scores, next_scores, vt_ref, visible, lambda j: tot_ref[tot_base + j], finish, *scratch)


def _fox_call(tot, qt, k, kext, vt):
    b, nh, nq, t, hd = k.shape
    head = lambda bi, h, i: (bi, h, 0, 0, 0)
    qt, qt_spec = _slab_spec(qt, (1, 1, nq, hd, t), head)
    vt, vt_spec = _slab_spec(vt, (1, 1, nq, hd, t), head)
    return pl.pallas_call(
        _fox_kernel,
        grid=(b, nh, nq),
        in_specs=[pl.BlockSpec(memory_space=pltpu.SMEM),
                  qt_spec,
                  pl.BlockSpec((1, 1, nq, t, hd), head),
                  pl.BlockSpec((1, 1, nq, t, LANES), head),
                  vt_spec],
        out_specs=pl.BlockSpec((1, 1, t, hd), lambda bi, h, i: (bi, h, i, 0)),
        out_shape=jax.ShapeDtypeStruct((b, nh, nq * t, hd), BF16),
        scratch_shapes=_attend_scratch(1, hd, t, t),
        compiler_params=_params("parallel", "parallel", "arbitrary"),
    )(tot, qt, k, kext, vt)


def _out_kernel(oa_ref, za_ref, ob_ref, zb_ref, ma_ref, mb_ref, x_ref, gate_ref,
                wa_ref, wb_ref, wo_ref, gf_ref, o_ref):
    def gated(o_ref_, z_ref_):
        o = jnp.concatenate([o_ref_[0, hh] for hh in range(N_HEADS)], axis=1)
        return (o.astype(F32) * z_ref_[0].astype(F32)).astype(BF16)

    ya = jnp.dot(gated(oa_ref, za_ref), wa_ref[...], preferred_element_type=F32)
    yb = jnp.dot(gated(ob_ref, zb_ref), wb_ref[...], preferred_element_type=F32)
    merged = ma_ref[0].astype(F32) * ya + mb_ref[0].astype(F32) * yb
    out = jnp.dot(merged.astype(BF16), wo_ref[...], preferred_element_type=F32)
    xn = x_ref[0] + gate_ref[0] * out
    ms = jnp.mean(xn * xn, axis=-1, keepdims=True)
    o_ref[0] = xn * lax.rsqrt(ms + NORM_EPS) * gf_ref[...]


def _out_call(oa, za, ob, zb, ma, mb, x, gate, wa, wb, wo, gf, bm):
    b, s, d = x.shape
    heads = pl.BlockSpec((1, N_HEADS, bm, HEAD_DIM), lambda bi, i: (bi, 0, i, 0))
    row = pl.BlockSpec((1, bm, d), lambda bi, i: (bi, i, 0))
    wspec = pl.BlockSpec((d, d), lambda bi, i: (0, 0))
    (za, za_spec), (zb, zb_spec), (ma, ma_spec), (mb, mb_spec) = (
        _slab_spec(a, (1, bm, d), lambda bi, i: (bi, i, 0)) for a in (za, zb, ma, mb))
    return pl.pallas_call(
        _out_kernel,
        grid=(b, s // bm),
        in_specs=[heads, za_spec, heads, zb_spec, ma_spec, mb_spec, row,
                  pl.BlockSpec((1, 1, d), lambda bi, i: (bi, 0, 0)),
                  wspec, wspec, wspec,
                  pl.BlockSpec((1, d), lambda bi, i: (0, 0))],
        out_specs=row,
        out_shape=jax.ShapeDtypeStruct((b, s, d), F32),
        compiler_params=_params("parallel", "parallel"),
    )(oa, za, ob, zb, ma, mb, x, gate, wa, wb, wo, gf)


def kernel(x, c, positions, w_ada, b_ada, g_norm, w_in, b_forget, lambda_q1, lambda_k1, lambda_q2,
           lambda_k2, g_subln, w_branch_a, w_branch_b, w_out, g_final):
    b, s, d = x.shape
    assert d == D_MODEL and s % ATT_TILE == 0 and b <= SUBLANES
    assert w_ada.shape[0] == 1, "single-layer block"
    t = ATT_TILE
    nq = s // t
    n_main = 8 * D_MODEL

    c_pad = jnp.zeros((SUBLANES, d), F32).at[:b].set(c)
    w_t = jnp.swapaxes(w_in, 1, 2)
    w_qk = w_t[:, :2 * D_MODEL].reshape(2 * N_HEADS, 2, 2, ROPE_HALF, d).transpose(0, 2, 1, 3, 4)
    w_qk = w_qk.reshape(1, 2 * D_MODEL, d)
    w_merge = w_t[:, n_main + N_HEADS:]
    wf_pad = jnp.zeros((LANES, d), F32).at[:N_HEADS].set(w_t[0, n_main:n_main + N_HEADS]).astype(BF16)
    bf_pad = jnp.zeros((1, LANES), F32).at[0, :N_HEADS].set(b_forget[0])
    inv_freq = ROPE_THETA ** (-jnp.arange(ROPE_HALF, dtype=F32) / ROPE_HALF)
    invf = jnp.tile(inv_freq, LANES // ROPE_HALF)[None, :]
    sgn = jnp.concatenate([-jnp.ones((HEAD_DIM // 2,), F32), jnp.ones((HEAD_DIM // 2,), F32)])[None, :]
    pos_b = jnp.broadcast_to(positions.astype(F32)[:, :, None], (b, s, LANES))
    g_rep = jnp.broadcast_to((g_subln[0] * (1.0 - LAMBDA_INIT))[:, None], (HEAD_DIM, LANES))

    mod, lam_tile = _mod_call(c_pad, w_ada, b_ada, lambda_q1, lambda_k1, lambda_q2, lambda_k2)
    shift = mod[:b, None, 0:d]
    scale = mod[:b, None, d:2 * d]
    gate = mod[:b, None, 2 * d:3 * d]
    lam = lam_tile[0, :1]

    h, kext, tot_tile = _prenorm_cum_call(x, g_norm, scale, shift, wf_pad, bf_pad, 2 * t if nq % 2 == 0 else t)
    tables = _rope_table_call(pos_b, invf, sgn, t)

    pm = PROJ_ROWS if s % PROJ_ROWS == 0 else t
    proj = functools.partial(_proj_call, h)
    qa_t = proj(w_qk, 0, tables, pm, rope=True, scale=LOG2E * DA_QK_DIM ** -0.5, layout="heads_t")
    ka = proj(w_qk, 1, tables, pm, rope=True, layout="heads")
    kb = proj(w_t, 5, None, pm, layout="heads")
    vqv_t = proj(w_t, 2, None, pm, scale=(1.0, LOG2E * HEAD_DIM ** -0.5, 1.0), layout="heads_t", col_stride=2)
    zz = proj(w_t, 3, None, pm, scale=(1.0, 1.0), act="silu", col_stride=4)
    mm = proj(w_merge, 0, None, pm, scale=(1.0, 1.0), act="sigmoid", col_stride=1)

    tot = tot_tile[:, :, 0, :N_HEADS].transpose(0, 2, 1).reshape(-1)

    def tiles(a):
        return a.reshape(b, N_HEADS, nq, t, a.shape[-1])

    oa = _diff_call(lam, qa_t, tiles(ka), (vqv_t, 0), g_rep)
    ob = _fox_call(tot, (vqv_t, 1), tiles(kb), tiles(kext), (vqv_t, 2))

    return _out_call(oa, (zz, 0), ob, (zz, 1), (mm, 0), (mm, 1), x, gate,
                     w_branch_a[0].astype(BF16), w_branch_b[0].astype(BF16), w_out[0].astype(BF16),
                     g_final[None, :], t)
```

```python
import functools
import math

import jax
import jax.numpy as jnp
import numpy as np
from jax import lax
from jax.experimental import pallas as pl
from jax.experimental.pallas import tpu as pltpu

F32 = jnp.float32
BF16 = jnp.bfloat16

D_MODEL = 1024
N_HEADS = 8
HEAD_DIM = 128
DA_QK_DIM = 64
ROPE_HALF = DA_QK_DIM // 2
CHUNK = 64
ROPE_THETA = 10000.0
NORM_EPS = 1e-6
SUBLN_EPS = 1e-5
LAMBDA_INIT = 0.8 - 0.6 * math.exp(-0.3 * 0)
LOG2E = math.log2(math.e)

LANES = 128
SUBLANES = 8
ATT_TILE = 512
PROJ_ROWS = 1024
BIAS_TERMS = 3
VMEM_LIMIT = 48 * 1024 * 1024
NEG_BIG = -1e30


def _params(*sem):
    return pltpu.CompilerParams(dimension_semantics=sem, vmem_limit_bytes=VMEM_LIMIT)


def _mod_kernel(c_ref, w_ref, b_ref, lq1_ref, lk1_ref, lq2_ref, lk2_ref, mod_ref, lam_ref):
    c = c_ref[...]
    c_act = c * jax.nn.sigmoid(c)
    mod_ref[...] = jnp.dot(c_act, w_ref[0], precision=lax.Precision.HIGHEST,
                           preferred_element_type=F32) + b_ref[...]
    s1 = jnp.sum(lq1_ref[...] * lk1_ref[...], axis=-1, keepdims=True)
    s2 = jnp.sum(lq2_ref[...] * lk2_ref[...], axis=-1, keepdims=True)
    lam = jnp.exp(s1) - jnp.exp(s2) + LAMBDA_INIT
    lam_ref[...] = jnp.broadcast_to(lam, lam_ref.shape)


def _mod_call(c_pad, w_ada, b_ada, lq1, lk1, lq2, lk2):
    d = D_MODEL
    lam_spec = pl.BlockSpec((1, DA_QK_DIM), lambda j: (0, 0))
    return pl.pallas_call(
        _mod_kernel,
        grid=(3,),
        in_specs=[pl.BlockSpec((SUBLANES, d), lambda j: (0, 0)),
                  pl.BlockSpec((1, d, d), lambda j: (0, 0, j)),
                  pl.BlockSpec((1, d), lambda j: (0, j)),
                  lam_spec, lam_spec, lam_spec, lam_spec],
        out_specs=[pl.BlockSpec((SUBLANES, d), lambda j: (0, j)),
                   pl.BlockSpec((SUBLANES, LANES), lambda j: (0, 0))],
        out_shape=[jax.ShapeDtypeStruct((SUBLANES, 3 * d), F32),
                   jax.ShapeDtypeStruct((SUBLANES, LANES), F32)],
        compiler_params=_params("arbitrary"),
    )(c_pad, w_ada, b_ada, lq1, lk1, lq2, lk2)


def _rope_table_kernel(pos_ref, invf_ref, sgn_ref, cos_ref, sin_ref):
    ang = pos_ref[0] * invf_ref[...]
    cos_ref[0] = jnp.cos(ang)
    sin_ref[0] = jnp.sin(ang) * sgn_ref[...]


def _rope_table_call(pos_b, invf, sgn, bm):
    b, s, _ = pos_b.shape
    row = pl.BlockSpec((1, bm, LANES), lambda bi, i: (bi, i, 0))
    vec = pl.BlockSpec((1, LANES), lambda bi, i: (0, 0))
    shape = jax.ShapeDtypeStruct((b, s, LANES), F32)
    return pl.pallas_call(
        _rope_table_kernel,
        grid=(b, s // bm),
        in_specs=[row, vec, vec],
        out_specs=[row, row],
        out_shape=[shape, shape],
        compiler_params=_params("parallel", "parallel"),
    )(pos_b, invf, sgn)


def _proj_kernel(*refs, rope, scale, act, layout):
    if rope:
        h_ref, w_ref, cos_ref, sin_ref, o_ref = refs
    else:
        h_ref, w_ref, o_ref = refs
    if isinstance(scale, tuple):
        g = pl.program_id(0)
        apply_scale = any(v != 1.0 for v in scale)
        slab_scale = jnp.float32(scale[0])
        for k in range(1, len(scale)):
            slab_scale = jnp.where(g == k, jnp.float32(scale[k]), slab_scale)
        scale = slab_scale
        o_ref = o_ref.at[0]
    else:
        apply_scale = scale != 1.0
    acc = lax.dot_general(h_ref[0], w_ref[0].astype(BF16), (((1,), (1,)), ((), ())),
                          preferred_element_type=F32)
    if act in ("silu", "sigmoid"):
        gate = 0.5 * jnp.tanh(0.5 * acc) + 0.5
        acc = acc * gate if act == "silu" else gate
    if layout == "nat":
        o_ref[0] = acc.astype(BF16)
        return
    if rope:
        cos = cos_ref[0]
        sin = sin_ref[0]
    for hh in range(N_HEADS):
        t = acc[:, hh * HEAD_DIM:(hh + 1) * HEAD_DIM]
        if rope:
            t = t * cos + pltpu.roll(t, HEAD_DIM // 2, 1) * sin
        if apply_scale:
            t = t * scale
        if layout == "heads":
            o_ref[0, hh] = t.astype(BF16)
        else:
            tile = o_ref.shape[-1]
            for part in range(o_ref.shape[2]):
                o_ref[0, hh, part] = t[part * tile:(part + 1) * tile].T.astype(BF16)


def _proj_call(h, w_all, col_block, tables, bm, *, rope=False, scale=1.0, act=None, layout="nat", col_stride=0):
    if isinstance(scale, tuple):
        return _proj_slabs_call(h, w_all, col_block, col_stride, bm, scale, act, layout)
    b, s, d = h.shape
    n = D_MODEL
    in_specs = [pl.BlockSpec((1, bm, d), lambda bi, i: (bi, i, 0)),
                pl.BlockSpec((1, n, d), lambda bi, i: (0, col_block, 0))]
    args = [h, w_all]
    if rope:
        tab = pl.BlockSpec((1, bm, LANES), lambda bi, i: (bi, i, 0))
        in_specs += [tab, tab]
        args += list(tables)
    if layout == "nat":
        out_spec = pl.BlockSpec((1, bm, n), lambda bi, i: (bi, i, 0))
        out_shape = jax.ShapeDtypeStruct((b, s, n), BF16)
    elif layout == "heads":
        out_spec = pl.BlockSpec((1, N_HEADS, bm, HEAD_DIM), lambda bi, i: (bi, 0, i, 0))
        out_shape = jax.ShapeDtypeStruct((b, N_HEADS, s, HEAD_DIM), BF16)
    else:
        parts = bm // ATT_TILE
        out_spec = pl.BlockSpec((1, N_HEADS, parts, HEAD_DIM, ATT_TILE), lambda bi, i: (bi, 0, i, 0, 0))
        out_shape = jax.ShapeDtypeStruct((b, N_HEADS, s // ATT_TILE, HEAD_DIM, ATT_TILE), BF16)
    return pl.pallas_call(
        functools.partial(_proj_kernel, rope=rope, scale=scale, act=act, layout=layout),
        grid=(b, s // bm),
        in_specs=in_specs,
        out_specs=out_spec,
        out_shape=out_shape,
        compiler_params=_params("parallel", "parallel"),
    )(*args)


def _proj_slabs_call(h, w_all, col_block, col_stride, bm, scales, act, layout):
    b, s, d = h.shape
    n, groups = D_MODEL, len(scales)
    in_specs = [pl.BlockSpec((1, bm, d), lambda g, bi, i: (bi, i, 0)),
                pl.BlockSpec((1, n, d), lambda g, bi, i: (0, col_block + col_stride * g, 0))]
    if layout == "nat":
        out_spec = pl.BlockSpec((1, 1, bm, n), lambda g, bi, i: (g, bi, i, 0))
        out_shape = jax.ShapeDtypeStruct((groups, b, s, n), BF16)
    else:
        parts = bm // ATT_TILE
        out_spec = pl.BlockSpec((1, 1, N_HEADS, parts, HEAD_DIM, ATT_TILE), lambda g, bi, i: (g, bi, 0, i, 0, 0))
        out_shape = jax.ShapeDtypeStruct((groups, b, N_HEADS, s // ATT_TILE, HEAD_DIM, ATT_TILE), BF16)
    return pl.pallas_call(
        functools.partial(_proj_kernel, rope=False, scale=scales, act=act, layout=layout),
        grid=(groups, b, s // bm),
        in_specs=in_specs,
        out_specs=out_spec,
        out_shape=out_shape,
        compiler_params=_params("parallel", "parallel", "parallel"),
    )(h, w_all)


def _split3(v):
    hi = v.astype(BF16).astype(F32)
    r1 = v - hi
    mid = r1.astype(BF16).astype(F32)
    return hi, mid, r1 - mid


def _prenorm_cum_kernel(x_ref, g_ref, scale_ref, shift_ref, wf_ref, bf_ref, sel_ref, h_ref, kext_ref, tot_ref):
    t = ATT_TILE
    r = lax.broadcasted_iota(jnp.int32, (t, t), 0)
    c = lax.broadcasted_iota(jnp.int32, (t, t), 1)
    tri = jnp.where(c <= r, 1.0, 0.0).astype(BF16)
    for part in range(x_ref.shape[1] // t):
        rows = slice(part * t, (part + 1) * t)
        x = x_ref[0, rows]
        ms = jnp.mean(x * x, axis=-1, keepdims=True)
        y = x * lax.rsqrt(ms + NORM_EPS) * g_ref[...]
        h = (y * (1.0 + scale_ref[0]) + shift_ref[0]).astype(BF16)
        h_ref[0, rows] = h
        logit = lax.dot_general(h, wf_ref[...], (((1,), (1,)), ((), ())),
                                preferred_element_type=F32) + bf_ref[...]
        log_f = jnp.minimum(logit, 0.0) - jnp.log(1.0 + jnp.exp(-jnp.abs(logit)))
        terms = jnp.concatenate([v.astype(BF16) for v in _split3(log_f)], axis=1)
        sums = jnp.dot(tri, terms, preferred_element_type=F32)
        cum = sum(sums[:, n * LANES:(n + 1) * LANES] for n in range(BIAS_TERMS))
        bias = cum * (-LOG2E)
        tot_ref[0, part] = jnp.broadcast_to(-bias[t - 1:t, :], tot_ref.shape[2:])
        split = jnp.concatenate([v.astype(BF16) for v in _split3(bias)], axis=1)
        ext = jnp.dot(split, sel_ref[...], preferred_element_type=F32)
        for hh in range(N_HEADS):
            kext_ref[0, hh, rows] = ext[:, hh * LANES:(hh + 1) * LANES].astype(BF16)


def _bias_selection():
    sel = np.zeros((BIAS_TERMS, LANES, N_HEADS, LANES), np.float32)
    for n in range(BIAS_TERMS):
        for hh in range(N_HEADS):
            sel[n, hh, hh, n] = 1.0
    return sel.reshape(BIAS_TERMS * LANES, N_HEADS * LANES)


def _prenorm_cum_call(x, g, scale, shift, wf_pad, bf_pad, bm):
    b, s, d = x.shape
    sel = jnp.asarray(_bias_selection(), BF16)
    row = pl.BlockSpec((1, bm, d), lambda bi, i: (bi, i, 0))
    vec = pl.BlockSpec((1, 1, d), lambda bi, i: (bi, 0, 0))
    return pl.pallas_call(
        _prenorm_cum_kernel,
        grid=(b, s // bm),
        in_specs=[row, pl.BlockSpec((1, d), lambda bi, i: (0, 0)), vec, vec,
                  pl.BlockSpec((LANES, d), lambda bi, i: (0, 0)),
                  pl.BlockSpec((1, LANES), lambda bi, i: (0, 0)),
                  pl.BlockSpec(sel.shape, lambda bi, i: (0, 0))],
        out_specs=[row,
                   pl.BlockSpec((1, N_HEADS, bm, LANES), lambda bi, i: (bi, 0, i, 0)),
                   pl.BlockSpec((1, bm // ATT_TILE, SUBLANES, LANES), lambda bi, i: (bi, i, 0, 0))],
        out_shape=[jax.ShapeDtypeStruct((b, s, d), BF16),
                   jax.ShapeDtypeStruct((b, N_HEADS, s, LANES), BF16),
                   jax.ShapeDtypeStruct((b, s // ATT_TILE, SUBLANES, LANES), F32)],
        compiler_params=_params("parallel", "parallel"),
    )(x, g, scale, shift, wf_pad, bf_pad, sel)


def _slab_spec(operand, block, index_map):
    if isinstance(operand, tuple):
        slab = operand[1]
        return operand[0], pl.BlockSpec((None,) + block, lambda *g: (slab,) + index_map(*g))
    return operand, pl.BlockSpec(block, index_map)


PIPE_SLOTS = 3
PV_LAG = 2
DENOM_ROWS = 16


def _attend(i, n_maps, scores, next_scores, vt_ref, visible, frame_shift, finish, m, acc, *bufs):
    s_buf, c_buf, p_buf, a_buf = (bufs[k * PIPE_SLOTS:(k + 1) * PIPE_SLOTS] for k in range(4))
    hd = acc.shape[1] - DENOM_ROWS
    ones = jnp.ones((DENOM_ROWS, vt_ref.shape[-1]), BF16)
    m[...] = jnp.full(m.shape, NEG_BIG, F32)

    def put_scores(slot, tiles):
        for n, s in enumerate(tiles):
            s_buf[slot][n] = s
            c_buf[slot][n] = jnp.max(s, axis=0, keepdims=True)

    def softmax(s, s_max, n):
        m_prev = m[n]
        m_new = jnp.maximum(m_prev, s_max)
        alpha = jnp.exp2(m_prev - m_new)
        p = jnp.exp2(s - m_new).astype(BF16)
        return m_new, alpha, p

    def value_product(prev, alpha, j, p):
        vt_ones = jnp.concatenate([vt_ref[0, 0, j], ones], axis=0)
        return alpha * prev + jnp.dot(vt_ones, p, preferred_element_type=F32)

    def step(j, slot, prefetch=True):
        if prefetch:
            put_scores((slot + 2) % PIPE_SLOTS, scores(jnp.minimum(j + 2, i)))
        old = (slot - PV_LAG) % PIPE_SLOTS
        for n in range(n_maps):
            acc[n] = value_product(acc[n], a_buf[old][n], jnp.maximum(j - PV_LAG, 0), p_buf[old][n])
            m_new, alpha, p = softmax(s_buf[slot][n], c_buf[slot][n], n)
            p_buf[slot][n] = p
            a_buf[slot][n] = alpha
            m[n] = m_new + frame_shift(j)

    def last(slot):
        results = []
        for n in range(n_maps):
            out = acc[n]
            for back in range(PV_LAG, 0, -1):
                old = (slot - back) % PIPE_SLOTS
                out = value_product(out, a_buf[old][n], jnp.maximum(i - back, 0), p_buf[old][n])
            s = jnp.where(visible(), s_buf[slot][n], -jnp.inf)
            _, alpha, p = softmax(s, jnp.max(s, axis=0, keepdims=True), n)
            out = value_product(out, alpha, i, p)
            results.append((out[:hd], jnp.mean(out[hd:], axis=0, keepdims=True)))
        put_scores(0, next_scores(0))
        put_scores(1, next_scores(1))
        finish(results)

    @pl.when(i == 0)
    def _():
        acc[...] = jnp.zeros(acc.shape, F32)
        for slot in range(PIPE_SLOTS - PV_LAG, PIPE_SLOTS):
            p_buf[slot][...] = jnp.zeros(p_buf[slot].shape, BF16)
            a_buf[slot][...] = jnp.ones(a_buf[slot].shape, F32)
        put_scores(0, scores(0))

    def rotation(jj, carry):
        for slot in range(PIPE_SLOTS):
            step(PIPE_SLOTS * jj + slot, slot)
        return carry

    trip = 2 * PIPE_SLOTS

    def two_rotations(u, carry):
        return rotation(2 * u + 1, rotation(2 * u, carry))

    lax.fori_loop(0, i // trip, two_rotations, 0)

    for rem in range(trip):
        @pl.when(i % trip == rem)
        def _(rem=rem):
            for k in range(rem):
                step(i - rem + k, k % PIPE_SLOTS, prefetch=k + 2 <= rem)
            last(rem % PIPE_SLOTS)


def _attend_scratch(n_maps, hd, tk, tq):
    return ([pltpu.VMEM((n_maps, 1, tq), F32),
             pltpu.VMEM((n_maps, hd + DENOM_ROWS, tq), F32)]
            + [pltpu.VMEM((n_maps, tk, tq), F32)] * PIPE_SLOTS
            + [pltpu.VMEM((n_maps, 1, tq), F32)] * PIPE_SLOTS
            + [pltpu.VMEM((n_maps, tk, tq), BF16)] * PIPE_SLOTS
            + [pltpu.VMEM((n_maps, 1, tq), F32)] * PIPE_SLOTS)


def _diff_kernel(lam_ref, qt_ref, k_ref, vt_ref, g_ref, z_ref, o_ref, *scratch):
    i = pl.program_id(2)
    nq, t = qt_ref.shape[2], qt_ref.shape[-1]

    def both_maps(block):
        qt = qt_ref[0, 0, block]
        row = lax.broadcasted_iota(jnp.int32, qt.shape, 0)
        map1 = (row // ROPE_HALF) % 2 == 0
        zero = jnp.zeros_like(qt)
        return jnp.concatenate([jnp.where(map1, qt, zero), jnp.where(map1, zero, qt)], axis=1)

    q12 = both_maps(i)

    def scores(j):
        return (jnp.dot(k_ref[0, 0, j], q12, preferred_element_type=F32),)

    def next_scores(j):
        return (jnp.dot(k_ref[0, 0, j], both_maps(jnp.minimum(i + 1, nq - 1)), preferred_element_type=F32),)

    def visible():
        kc = lax.broadcasted_iota(jnp.int32, (t, 2 * t), 0) // CHUNK
        qc = (lax.broadcasted_iota(jnp.int32, (t, 2 * t), 1) % t) // CHUNK
        return kc <= qc

    def finish(results):
        ((out, l),) = results
        o = out / l
        o = o[:, :t] - lam_ref[0] * o[:, t:]
        ms = jnp.mean(o * o, axis=0, keepdims=True)
        g = jnp.concatenate([g_ref[...]] * (t // LANES), axis=1)
        y = o * lax.rsqrt(ms + SUBLN_EPS) * g
        o_ref[0, 0] = (y.T * z_ref[0].astype(F32)).astype(BF16)

    _attend(i, 1, scores, next_scores, vt_ref, visible, lambda j: 0.0, finish, *scratch)


def _diff_call(lam, qt, k, vt, g_rep, z):
    b, nh, nq, hd, t = qt.shape
    head = lambda bi, h, i: (bi, h, 0, 0, 0)
    vt, vt_spec = _slab_spec(vt, (1, 1, nq, hd, t), head)
    z, z_spec = _slab_spec(z, (1, t, hd), lambda bi, h, i: (bi, i, h))
    return pl.pallas_call(
        _diff_kernel,
        grid=(b, nh, nq),
        in_specs=[pl.BlockSpec(memory_space=pltpu.SMEM),
                  pl.BlockSpec((1, 1, nq, hd, t), head),
                  pl.BlockSpec((1, 1, nq, t, hd), head),
                  vt_spec,
                  pl.BlockSpec((hd, LANES), lambda bi, h, i: (0, 0)),
                  z_spec],
        out_specs=pl.BlockSpec((1, 1, t, hd), lambda bi, h, i: (bi, h, i, 0)),
        out_shape=jax.ShapeDtypeStruct((b, nh, nq * t, hd), BF16),
        scratch_shapes=_attend_scratch(1, hd, t, 2 * t),
        compiler_params=_params("parallel", "parallel", "arbitrary"),
    )(lam, qt, k, vt, g_rep, z)


def _fox_kernel(tot_ref, qt_ref, k_ref, kext_ref, vt_ref, z_ref, o_ref, *scratch):
    bi, h, i = pl.program_id(0), pl.program_id(1), pl.program_id(2)
    nq, t = qt_ref.shape[2], qt_ref.shape[-1]
    tot_base = (bi * N_HEADS + h) * nq

    def with_ones(block):
        qt = qt_ref[0, 0, block]
        row = lax.broadcasted_iota(jnp.int32, qt.shape, 0)
        return jnp.concatenate([qt, jnp.where(row < BIAS_TERMS, 1.0, 0.0).astype(BF16)], axis=0)

    q_cat = with_ones(i)

    def scores_of(q, j):
        k_cat = jnp.concatenate([k_ref[0, 0, j], kext_ref[0, 0, j]], axis=1)
        return (jnp.dot(k_cat, q, preferred_element_type=F32),)

    def scores(j):
        return scores_of(q_cat, j)

    def next_scores(j):
        return scores_of(with_ones(jnp.minimum(i + 1, nq - 1)), j)

    def visible():
        return lax.broadcasted_iota(jnp.int32, (t, t), 0) <= lax.broadcasted_iota(jnp.int32, (t, t), 1)

    def finish(results):
        ((o, l),) = results
        o_ref[0, 0] = ((o / l).T * z_ref[0].astype(F32)).astype(BF16)

    _attend(i, 1, scores, next_scores, vt_ref, visible, lambda j: tot_ref[tot_base + j], finish, *scratch)


def _fox_call(tot, qt, k, kext, vt, z):
    b, nh, nq, t, hd = k.shape
    head = lambda bi, h, i: (bi, h, 0, 0, 0)
    qt, qt_spec = _slab_spec(qt, (1, 1, nq, hd, t), head)
    vt, vt_spec = _slab_spec(vt, (1, 1, nq, hd, t), head)
    z, z_spec = _slab_spec(z, (1, t, hd), lambda bi, h, i: (bi, i, h))
    return pl.pallas_call(
        _fox_kernel,
        grid=(b, nh, nq),
        in_specs=[pl.BlockSpec(memory_space=pltpu.SMEM),
                  qt_spec,
                  pl.BlockSpec((1, 1, nq, t, hd), head),
                  pl.BlockSpec((1, 1, nq, t, LANES), head),
                  vt_spec,
                  z_spec],
        out_specs=pl.BlockSpec((1, 1, t, hd), lambda bi, h, i: (bi, h, i, 0)),
        out_shape=jax.ShapeDtypeStruct((b, nh, nq * t, hd), BF16),
        scratch_shapes=_attend_scratch(1, hd, t, t),
        compiler_params=_params("parallel", "parallel", "arbitrary"),
    )(tot, qt, k, kext, vt, z)


def _out_kernel(oa_ref, ob_ref, ma_ref, mb_ref, x_ref, gate_ref, wa_ref, wb_ref, wo_ref, gf_ref, o_ref):
    def heads_merged(o_ref_):
        return jnp.concatenate([o_ref_[0, hh] for hh in range(N_HEADS)], axis=1)

    ya = jnp.dot(heads_merged(oa_ref), wa_ref[...], preferred_element_type=F32)
    yb = jnp.dot(heads_merged(ob_ref), wb_ref[...], preferred_element_type=F32)
    merged = ma_ref[0].astype(F32) * ya + mb_ref[0].astype(F32) * yb
    out = jnp.dot(merged.astype(BF16), wo_ref[...], preferred_element_type=F32)
    xn = x_ref[0] + gate_ref[0] * out
    ms = jnp.mean(xn * xn, axis=-1, keepdims=True)
    o_ref[0] = xn * lax.rsqrt(ms + NORM_EPS) * gf_ref[...]


def _out_call(oa, ob, ma, mb, x, gate, wa, wb, wo, gf, bm):
    b, s, d = x.shape
    heads = pl.BlockSpec((1, N_HEADS, bm, HEAD_DIM), lambda bi, i: (bi, 0, i, 0))
    row = pl.BlockSpec((1, bm, d), lambda bi, i: (bi, i, 0))
    wspec = pl.BlockSpec((d, d), lambda bi, i: (0, 0))
    (ma, ma_spec), (mb, mb_spec) = (_slab_spec(a, (1, bm, d), lambda bi, i: (bi, i, 0)) for a in (ma, mb))
    return pl.pallas_call(
        _out_kernel,
        grid=(b, s // bm),
        in_specs=[heads, heads, ma_spec, mb_spec, row,
                  pl.BlockSpec((1, 1, d), lambda bi, i: (bi, 0, 0)),
                  wspec, wspec, wspec,
                  pl.BlockSpec((1, d), lambda bi, i: (0, 0))],
        out_specs=row,
        out_shape=jax.ShapeDtypeStruct((b, s, d), F32),
        compiler_params=_params("parallel", "parallel"),
    )(oa, ob, ma, mb, x, gate, wa, wb, wo, gf)


def kernel(x, c, positions, w_ada, b_ada, g_norm, w_in, b_forget, lambda_q1, lambda_k1, lambda_q2,
           lambda_k2, g_subln, w_branch_a, w_branch_b, w_out, g_final):
    b, s, d = x.shape
    assert d == D_MODEL and s % ATT_TILE == 0 and b <= SUBLANES
    assert w_ada.shape[0] == 1, "single-layer block"
    t = ATT_TILE
    nq = s // t
    n_main = 8 * D_MODEL

    c_pad = jnp.zeros((SUBLANES, d), F32).at[:b].set(c)
    w_t = jnp.swapaxes(w_in, 1, 2)
    w_qk = w_t[:, :2 * D_MODEL].reshape(2 * N_HEADS, 2, 2, ROPE_HALF, d).transpose(0, 2, 1, 3, 4)
    w_qk = w_qk.reshape(1, 2 * D_MODEL, d)
    w_merge = w_t[:, n_main + N_HEADS:]
    wf_pad = jnp.zeros((LANES, d), F32).at[:N_HEADS].set(w_t[0, n_main:n_main + N_HEADS]).astype(BF16)
    bf_pad = jnp.zeros((1, LANES), F32).at[0, :N_HEADS].set(b_forget[0])
    inv_freq = ROPE_THETA ** (-jnp.arange(ROPE_HALF, dtype=F32) / ROPE_HALF)
    invf = jnp.tile(inv_freq, LANES // ROPE_HALF)[None, :]
    sgn = jnp.concatenate([-jnp.ones((HEAD_DIM // 2,), F32), jnp.ones((HEAD_DIM // 2,), F32)])[None, :]
    pos_b = jnp.broadcast_to(positions.astype(F32)[:, :, None], (b, s, LANES))
    g_rep = jnp.broadcast_to((g_subln[0] * (1.0 - LAMBDA_INIT))[:, None], (HEAD_DIM, LANES))

    mod, lam_tile = _mod_call(c_pad, w_ada, b_ada, lambda_q1, lambda_k1, lambda_q2, lambda_k2)
    shift = mod[:b, None, 0:d]
    scale = mod[:b, None, d:2 * d]
    gate = mod[:b, None, 2 * d:3 * d]
    lam = lam_tile[0, :1]

    h, kext, tot_tile = _prenorm_cum_call(x, g_norm, scale, shift, wf_pad, bf_pad, 2 * t if nq % 2 == 0 else t)
    tables = _rope_table_call(pos_b, invf, sgn, t)

    pm = PROJ_ROWS if s % PROJ_ROWS == 0 else t
    proj = functools.partial(_proj_call, h)
    qa_t = proj(w_qk, 0, tables, pm, rope=True, scale=LOG2E * DA_QK_DIM ** -0.5, layout="heads_t")
    ka = proj(w_qk, 1, tables, pm, rope=True, layout="heads")
    kb = proj(w_t, 5, None, pm, layout="heads")
    vqv_t = proj(w_t, 2, None, pm, scale=(1.0, LOG2E * HEAD_DIM ** -0.5, 1.0), layout="heads_t", col_stride=2)
    zz = proj(w_t, 3, None, pm, scale=(1.0, 1.0), act="silu", col_stride=4)
    mm = proj(w_merge, 0, None, pm, scale=(1.0, 1.0), act="sigmoid", col_stride=1)

    tot = tot_tile[:, :, 0, :N_HEADS].transpose(0, 2, 1).reshape(-1)

    def tiles(a):
        return a.reshape(b, N_HEADS, nq, t, a.shape[-1])

    oa = _diff_call(lam, qa_t, tiles(ka), (vqv_t, 0), g_rep, (zz, 0))
    ob = _fox_call(tot, (vqv_t, 1), tiles(kb), tiles(kext), (vqv_t, 2), (zz, 1))

    return _out_call(oa, ob, (mm, 0), (mm, 1), x, gate,
                     w_branch_a[0].astype(BF16), w_branch_b[0].astype(BF16), w_out[0].astype(BF16),
                     g_final[None, :], t)
```
